```python
import jax, jax.numpy as jnp
from jax import lax
import numpy as np

D_MODEL = 1024
BATCH = 4
SEQ = 8192
DEPTH = 2

CTX_LEN = 256
GRID_W = 64
D_MIX = D_MODEL
DN_HEADS = 4
DN_HEAD_DIM = 128
DN_WIDTH = DN_HEADS * DN_HEAD_DIM
DN_CHUNK = 64
CONV_W = 5
GLA_HEADS = 4
GLA_DK = 64
GLA_DV = 128
GLA_K_WIDTH = GLA_HEADS * GLA_DK
GLA_V_WIDTH = GLA_HEADS * GLA_DV
GLA_GATE_RANK = 16
GLA_GATE_NORMALIZER = 16.0
GLA_CHUNK = 16
D_FF = 2816
N_EXPERTS = 8
TOP_K = 2
D_EXPERT = 3584
N_MOD = 6
NORM_EPS = 1e-6
IN_SIZES = (3 * DN_WIDTH, DN_WIDTH, 2 * DN_HEADS, 2 * DN_HEADS, GLA_K_WIDTH, GLA_K_WIDTH, GLA_V_WIDTH, GLA_V_WIDTH, 2 * GLA_GATE_RANK)
IN_COLS = 4 * DN_WIDTH + 4 * DN_HEADS + 2 * GLA_K_WIDTH + 2 * GLA_V_WIDTH + 2 * GLA_GATE_RANK

kernel_name = 'hybrid_deltanet_gla_prefix_moe_dit'


def rms_norm(x, gain):
    xf = x.astype(jnp.float32)
    y = xf * lax.rsqrt(jnp.mean(xf * xf, axis=-1, keepdims=True) + NORM_EPS)
    return (y * gain.astype(jnp.float32)).astype(x.dtype)


def l2_norm(t):
    return t * lax.rsqrt(jnp.sum(t * t, axis=-1, keepdims=True) + NORM_EPS)


def modulate(h, shift, scale):
    return h * (1 + scale) + shift


def centred_depthwise_conv(t, w):
    ch = t.shape[-1]
    return lax.conv_general_dilated(t, w[:, None, :].astype(t.dtype), window_strides=(1,),
                                    padding=[(CONV_W // 2, CONV_W // 2)],
                                    dimension_numbers=('NWC', 'WIO', 'NWC'), feature_group_count=ch)


def raster_to_columns(t, axis, rows):
    shp = t.shape
    t = t.reshape(shp[:axis] + (rows, GRID_W) + shp[axis + 1:])
    return jnp.swapaxes(t, axis, axis + 1).reshape(shp)


def columns_to_raster(t, axis, rows):
    shp = t.shape
    t = t.reshape(shp[:axis] + (GRID_W, rows) + shp[axis + 1:])
    return jnp.swapaxes(t, axis, axis + 1).reshape(shp)


def gated_delta_chunked(q, k, v, g, beta, s0):
    b_, h_, L, _ = q.shape
    dv = v.shape[-1]
    C = DN_CHUNK
    n = L // C
    q, k, v = (t.reshape(b_, h_, n, C, t.shape[-1]) for t in (q, k, v))
    g = jnp.cumsum(g.reshape(b_, h_, n, C), axis=-1)
    beta = beta.reshape(b_, h_, n, C, 1)
    causal = jnp.tril(jnp.ones((C, C), bool))
    strict = jnp.tril(jnp.ones((C, C), bool), -1)
    decay = jnp.exp(jnp.where(causal, g[..., :, None] - g[..., None, :], -jnp.inf))
    k_beta = k * beta
    lower = jnp.where(strict, jnp.einsum('bhncd,bhnmd->bhncm', k_beta, k) * decay, 0.0)
    rhs = jnp.concatenate([v * beta, k_beta * jnp.exp(g)[..., None]], axis=-1)
    uw = lax.linalg.triangular_solve(jnp.eye(C, dtype=q.dtype) + lower, rhs, left_side=True, lower=True)
    u, w = uw[..., :dv], uw[..., dv:]
    attn = jnp.einsum('bhncd,bhnmd->bhncm', q, k) * decay
    q_dec = q * jnp.exp(g)[..., None]
    k_dec = k * jnp.exp(g[..., -1:] - g)[..., None]
    g_last = jnp.exp(g[..., -1])

    def step(s, xs):
        u_c, w_c, a_c, q_c, k_c, gl_c = xs
        v_new = u_c - jnp.einsum('bhcd,bhde->bhce', w_c, s)
        o = jnp.einsum('bhcd,bhde->bhce', q_c, s) + jnp.einsum('bhcm,bhme->bhce', a_c, v_new)
        s = s * gl_c[..., None, None] + jnp.einsum('bhcd,bhce->bhde', k_c, v_new)
        return s, o

    xs = tuple(jnp.moveaxis(t, 2, 0) for t in (u, w, attn, q_dec, k_dec, g_last))
    s_fin, o = lax.scan(step, s0, xs)
    return jnp.moveaxis(o, 0, 2).reshape(b_, h_, L, dv), s_fin


def gla_chunked(q, k, v, log_a, s0):
    b_, h_, L, _ = q.shape
    dv = v.shape[-1]
    C = GLA_CHUNK
    n = L // C
    chunks = lambda t: jnp.moveaxis(t.reshape(b_, h_, n, C, t.shape[-1]), 2, 0)
    b_cum = jnp.cumsum(chunks(log_a), axis=3)
    causal = jnp.tril(jnp.ones((C, C), bool))[:, :, None]

    def step(s, xs):
        q_c, k_c, v_c, bc = xs
        rel = jnp.exp(jnp.where(causal, bc[:, :, :, None, :] - bc[:, :, None, :, :], -jnp.inf))
        attn = jnp.einsum('bhcd,bhmd,bhcmd->bhcm', q_c, k_c, rel)
        b_last = bc[:, :, -1:, :]
        o = jnp.einsum('bhcd,bhde->bhce', q_c * jnp.exp(bc), s) + jnp.einsum('bhcm,bhme->bhce', attn, v_c)
        s = s * jnp.exp(b_last[:, :, 0, :, None]) + jnp.einsum('bhcd,bhce->bhde', k_c * jnp.exp(b_last - bc), v_c)
        return s, o

    s_fin, o = lax.scan(step, s0, (chunks(q), chunks(k), chunks(v), b_cum))
    return jnp.moveaxis(o, 0, 2).reshape(b_, h_, L, dv), s_fin


def bidirectional_prefix(chunk_fn, ctx_shared, ctx_dir, lat_shared, lat_dir, s0):
    o_ctx, o_lat = None, None
    for d in range(2):
        flip = (lambda t: jnp.flip(t, axis=2)) if d == 1 else (lambda t: t)
        c_args = [flip(t) for t in ctx_shared + tuple(p[d] for p in ctx_dir)]
        l_args = [flip(t) for t in lat_shared + tuple(p[d] for p in lat_dir)]
        oc, s_ctx = chunk_fn(*c_args, s0)
        ol, _ = chunk_fn(*l_args, s_ctx)
        oc, ol = flip(oc), flip(ol)
        o_ctx = oc if o_ctx is None else o_ctx + oc
        o_lat = ol if o_lat is None else o_lat + ol
    return o_ctx, o_lat


def project_tokens(h, w_in, conv_qkv, dn_a_log, dn_dt_bias, gla_w_gate2, gla_b_gate):
    b_, L, _ = h.shape
    f32 = jnp.float32
    dn_qkv, dn_z, dn_beta, dn_a, gq, gk, gv, gz, g_lr = jnp.split(h @ w_in, np.cumsum(IN_SIZES)[:-1].tolist(), axis=-1)
    dn_qkv = jax.nn.silu(centred_depthwise_conv(dn_qkv, conv_qkv))
    dq, dk, dv = jnp.split(dn_qkv, 3, axis=-1)
    heads = lambda t, nh: t.reshape(b_, L, nh, -1).transpose(0, 2, 1, 3).astype(f32)
    dq = l2_norm(heads(dq, DN_HEADS)) * DN_HEAD_DIM ** -0.5
    dk = l2_norm(heads(dk, DN_HEADS))
    dv = heads(dv, DN_HEADS)
    beta = jax.nn.sigmoid(dn_beta.astype(f32)).reshape(b_, L, 2, DN_HEADS).transpose(2, 0, 3, 1)
    a = dn_a.astype(f32).reshape(b_, L, 2, DN_HEADS).transpose(2, 0, 3, 1)
    g = -jnp.exp(dn_a_log.astype(f32))[:, None, :, None] * jax.nn.softplus(a + dn_dt_bias.astype(f32)[:, None, :, None])
    gq = heads(gq, GLA_HEADS) * GLA_DK ** -0.5
    gk = heads(gk, GLA_HEADS)
    gv = heads(gv, GLA_HEADS)
    g_lr = g_lr.reshape(b_, L, 2, GLA_GATE_RANK)
    gate = jnp.einsum('blnr,nrk->nblk', g_lr, gla_w_gate2) + gla_b_gate[:, None, None, :]
    log_a = jax.nn.log_sigmoid(gate.astype(f32)) / GLA_GATE_NORMALIZER
    log_a = log_a.reshape(2, b_, L, GLA_HEADS, GLA_DK).transpose(0, 1, 3, 2, 4)
    return (dq, dk, dv), (g, beta), dn_z, (gq, gk, gv), (log_a,), gz


def gated_head_norm(o, z, gain):
    b_, nh, L, dv = o.shape
    o = rms_norm(o.transpose(0, 2, 1, 3), gain)
    return (o * jax.nn.silu(z.astype(jnp.float32)).reshape(b_, L, nh, dv)).reshape(b_, L, nh * dv).astype(z.dtype)


def swiglu(t, w_gu, w_down):
    gate, up = jnp.split(t @ w_gu, 2, axis=-1)
    return (jax.nn.silu(gate) * up) @ w_down


def moe_swiglu(h, router, w_gu, w_down):
    shp = h.shape
    t = h.reshape(-1, shp[-1])
    probs = jax.nn.softmax((t @ router).astype(jnp.float32), axis=-1)
    top_p, top_i = lax.top_k(probs, TOP_K)
    top_p = top_p / jnp.sum(top_p, axis=-1, keepdims=True)
    combine = jnp.einsum('tk,tke->te', top_p, jax.nn.one_hot(top_i, N_EXPERTS, dtype=jnp.float32)).astype(t.dtype)
    out = jnp.zeros_like(t)
    for e in range(N_EXPERTS):
        out = out + combine[:, e:e + 1] * swiglu(t, w_gu[e], w_down[e])
    return out.reshape(shp)


def channel_mixer(h, i, ffn_w_gu, ffn_w_down, moe_router, moe_w_gu, moe_w_down):
    if i % 2 == 0:
        return swiglu(h, ffn_w_gu[i // 2], ffn_w_down[i // 2])
    return moe_swiglu(h, moe_router[i // 2], moe_w_gu[i // 2], moe_w_down[i // 2])


def setup_inputs(seed: int = 0) -> dict:
    key = jax.random.key(seed)
    ks = jax.random.split(key, 24)
    f32 = jnp.float32
    n_dense = (DEPTH + 1) // 2
    n_moe = DEPTH // 2
    nrm = lambda k, shape, fan_in: jax.random.normal(k, shape, f32) * fan_in ** -0.5
    gain = lambda k, shape: 1.0 + 0.02 * jax.random.normal(k, shape, f32)
    dt = jnp.exp(jax.random.uniform(ks[9], (DEPTH, 2, DN_HEADS), f32, np.log(1e-3), np.log(1e-1)))
    return {
        'x': jax.random.normal(ks[0], (BATCH, SEQ, D_MODEL), f32),
        'c': jax.random.normal(ks[1], (BATCH, D_MODEL), f32),
        'ctx': jax.random.normal(ks[2], (BATCH, CTX_LEN, D_MODEL), f32),
        'c_ctx': jax.random.normal(ks[3], (D_MODEL,), f32),
        'w_ada': 0.5 * nrm(ks[4], (DEPTH, D_MODEL, N_MOD * D_MODEL), D_MODEL),
        'b_ada': 0.02 * jax.random.normal(ks[5], (DEPTH, N_MOD * D_MODEL), f32),
        'norm_mix': gain(ks[6], (DEPTH, D_MODEL)),
        'norm_ffn': gain(ks[7], (DEPTH, D_MODEL)),
        'w_in': nrm(ks[8], (DEPTH, D_MODEL, IN_COLS), D_MODEL),
        'conv_qkv': nrm(ks[10], (DEPTH, CONV_W, 3 * DN_WIDTH), CONV_W),
        'dn_a_log': jnp.log(jax.random.uniform(ks[11], (DEPTH, 2, DN_HEADS), f32, 1.0, 16.0)),
        'dn_dt_bias': dt + jnp.log(-jnp.expm1(-dt)),
        'dn_norm': gain(ks[12], (DEPTH, DN_HEAD_DIM)),
        'gla_w_gate2': nrm(ks[13], (DEPTH, 2, GLA_GATE_RANK, GLA_K_WIDTH), GLA_GATE_RANK),
        'gla_b_gate': 0.1 * jax.random.normal(ks[14], (DEPTH, 2, GLA_K_WIDTH), f32),
        'gla_norm': gain(ks[15], (DEPTH, GLA_DV)),
        'w_out': nrm(ks[16], (DEPTH, D_MIX, D_MODEL), D_MIX),
        'ffn_w_gu': nrm(ks[17], (n_dense, D_MODEL, 2 * D_FF), D_MODEL),
        'ffn_w_down': nrm(ks[18], (n_dense, D_FF, D_MODEL), D_FF),
        'moe_router': nrm(ks[19], (n_moe, D_MODEL, N_EXPERTS), D_MODEL),
        'moe_w_gu': nrm(ks[20], (n_moe, N_EXPERTS, D_MODEL, 2 * D_EXPERT), D_MODEL),
        'moe_w_down': nrm(ks[21], (n_moe, N_EXPERTS, D_EXPERT, D_MODEL), D_EXPERT),
        'final_norm': gain(ks[22], (D_MODEL,)),
    }


def reference(x, c, ctx, c_ctx, w_ada, b_ada, norm_mix, norm_ffn, w_in, conv_qkv, dn_a_log, dn_dt_bias, dn_norm,
              gla_w_gate2, gla_b_gate, gla_norm, w_out, ffn_w_gu, ffn_w_down, moe_router, moe_w_gu, moe_w_down,
              final_norm):
    b_ = x.shape[0]
    rows = x.shape[1] // GRID_W
    s0_dn = jnp.zeros((b_, DN_HEADS, DN_HEAD_DIM, DN_HEAD_DIM), jnp.float32)
    s0_gla = jnp.zeros((b_, GLA_HEADS, GLA_DK, GLA_DV), jnp.float32)
    for i in range(DEPTH):
        last = i == DEPTH - 1
        mod_l = jnp.split((jax.nn.silu(c) @ w_ada[i] + b_ada[i])[:, None, :], N_MOD, axis=-1)
        mod_c = jnp.split((jax.nn.silu(c_ctx) @ w_ada[i] + b_ada[i])[None, None, :], N_MOD, axis=-1)
        h_l = modulate(rms_norm(x, norm_mix[i]), mod_l[0], mod_l[1])
        h_c = modulate(rms_norm(ctx, norm_mix[i]), mod_c[0], mod_c[1])
        params = (w_in[i], conv_qkv[i], dn_a_log[i], dn_dt_bias[i], gla_w_gate2[i], gla_b_gate[i])
        dn_sh_c, dn_dir_c, dn_z_c, gla_sh_c, gla_dir_c, gla_z_c = project_tokens(h_c, *params)
        dn_sh_l, dn_dir_l, dn_z_l, gla_sh_l, gla_dir_l, gla_z_l = project_tokens(h_l, *params)
        o_dn_c, o_dn_l = bidirectional_prefix(gated_delta_chunked, dn_sh_c, dn_dir_c, dn_sh_l, dn_dir_l, s0_dn)
        gla_sh_l = tuple(raster_to_columns(t, 2, rows) for t in gla_sh_l)
        gla_dir_l = tuple(raster_to_columns(t, 3, rows) for t in gla_dir_l)
        o_gla_c, o_gla_l = bidirectional_prefix(gla_chunked, gla_sh_c, gla_dir_c, gla_sh_l, gla_dir_l, s0_gla)
        o_gla_l = columns_to_raster(o_gla_l, 2, rows)
        m_l = jnp.concatenate([gated_head_norm(o_dn_l, dn_z_l, dn_norm[i]),
                               gated_head_norm(o_gla_l, gla_z_l, gla_norm[i])], axis=-1)
        x = x + mod_l[2] * (m_l @ w_out[i])
        h_l = modulate(rms_norm(x, norm_ffn[i]), mod_l[3], mod_l[4])
        x = x + mod_l[5] * channel_mixer(h_l, i, ffn_w_gu, ffn_w_down, moe_router, moe_w_gu, moe_w_down)
        if not last:
            m_c = jnp.concatenate([gated_head_norm(o_dn_c, dn_z_c, dn_norm[i]),
                                   gated_head_norm(o_gla_c, gla_z_c, gla_norm[i])], axis=-1)
            ctx = ctx + mod_c[2] * (m_c @ w_out[i])
            h_c = modulate(rms_norm(ctx, norm_ffn[i]), mod_c[3], mod_c[4])
            ctx = ctx + mod_c[5] * channel_mixer(h_c, i, ffn_w_gu, ffn_w_down, moe_router, moe_w_gu, moe_w_down)
    return rms_norm(x, final_norm)
```

```python
import functools

import jax
import jax.numpy as jnp
import numpy as np
from jax import lax
from jax.experimental import pallas as pl
from jax.experimental.pallas import tpu as pltpu

F32 = jnp.float32
BF16 = jnp.bfloat16
HIGHEST = lax.Precision.HIGHEST

D_MODEL = 1024
GRID_W = 64
DN_HEADS = 4
DN_HEAD_DIM = 128
DN_WIDTH = DN_HEADS * DN_HEAD_DIM
CONV_W = 5
GLA_HEADS = 4
GLA_DK = 64
GLA_DV = 128
GLA_K_WIDTH = GLA_HEADS * GLA_DK
GLA_V_WIDTH = GLA_HEADS * GLA_DV
GLA_GATE_RANK = 16
GLA_GATE_NORMALIZER = 16.0
N_EXPERTS = 8
N_MOD = 6
NORM_EPS = 1e-6

LANES = 128
SUBLANES = 8
CHUNK = 64
SUB = 16
EXP_CLAMP = 80.0
D_EXPERT_TILE = 512
VMEM_LIMIT = 56 * 1024 * 1024

QKV_W = 3 * DN_WIDTH
SMALL_W = LANES
PROJ_COLS = QKV_W + DN_WIDTH + 2 * GLA_K_WIDTH + 2 * GLA_V_WIDTH + SMALL_W
BETA_LANE = 0
GDEC_LANE = 2 * DN_HEADS
GLR_LANE = 4 * DN_HEADS


def _cparams(sem, vmem=VMEM_LIMIT):
    return pltpu.CompilerParams(dimension_semantics=sem, vmem_limit_bytes=vmem)


def _sigmoid(t):
    return 1.0 / (1.0 + jnp.exp(-t))


def _silu(t):
    return t * _sigmoid(t)


def _softplus(t):
    return jnp.maximum(t, 0.0) + jnp.log(1.0 + jnp.exp(-jnp.abs(t)))


def _bdot(a, b):
    return jnp.dot(a.astype(BF16), b.astype(BF16), preferred_element_type=F32)


def _bdot_nt(a, b):
    return lax.dot_general(a.astype(BF16), b.astype(BF16), (((1,), (1,)), ((), ())), preferred_element_type=F32)


def _bdot_tn(a, b):
    return lax.dot_general(a.astype(BF16), b.astype(BF16), (((0,), (0,)), ((), ())), preferred_element_type=F32)


def _rms(t, gain):
    return t * lax.rsqrt(jnp.mean(t * t, axis=-1, keepdims=True) + NORM_EPS) * gain


def _ada_kernel(c_ref, w_ref, b_ref, o_ref):
    s = _silu(c_ref[...])
    o_ref[...] = jnp.dot(s, w_ref[...], precision=HIGHEST, preferred_element_type=F32) + b_ref[...]


def _ada(c_rows, w, b):
    rows = c_rows.shape[0]
    n = w.shape[1]
    tn = D_MODEL
    return pl.pallas_call(
        _ada_kernel,
        grid=(n // tn,),
        in_specs=[pl.BlockSpec((rows, D_MODEL), lambda j: (0, 0)),
                  pl.BlockSpec((D_MODEL, tn), lambda j: (0, j)),
                  pl.BlockSpec((1, tn), lambda j: (0, j))],
        out_specs=pl.BlockSpec((rows, tn), lambda j: (0, j)),
        out_shape=jax.ShapeDtypeStruct((rows, n), F32),
        compiler_params=_cparams(("arbitrary",)),
        name="ada",
    )(c_rows, w, b.reshape(1, n))


def _proj_kernel(x_ref, mod_ref, gain_ref, w_ref, qkv_ref, dnz_ref, gqk_ref, gv_ref, gz_ref, sm_ref):
    x = x_ref[0]
    y = _rms(x, gain_ref[...])
    h = (y * (1.0 + mod_ref[0, 1:2, :]) + mod_ref[0, 0:1, :]).astype(BF16)
    r = jnp.dot(h, w_ref[...], preferred_element_type=F32)
    o = 0
    for ref in (qkv_ref, dnz_ref, gqk_ref, gv_ref, gz_ref, sm_ref):
        wd = ref.shape[2]
        ref[0] = r[:, o:o + wd]
        o += wd


def _proj(x, mods, mod_row, gain, w, tm):
    b_, t_, _ = x.shape
    widths = (QKV_W, DN_WIDTH, 2 * GLA_K_WIDTH, GLA_V_WIDTH, GLA_V_WIDTH, SMALL_W)
    tok = lambda wd: pl.BlockSpec((1, tm, wd), lambda b, j: (b, j, 0))
    return pl.pallas_call(
        _proj_kernel,
        grid=(b_, t_ // tm),
        in_specs=[tok(D_MODEL),
                  pl.BlockSpec((1, SUBLANES, D_MODEL), lambda b, j: (mod_row(b), 0, 0)),
                  pl.BlockSpec((1, D_MODEL), lambda b, j: (0, 0)),
                  pl.BlockSpec((D_MODEL, PROJ_COLS), lambda b, j: (0, 0))],
        out_specs=[tok(wd) for wd in widths],
        out_shape=[jax.ShapeDtypeStruct((b_, t_, wd), F32) for wd in widths],
        compiler_params=_cparams(("parallel", "parallel")),
        name="proj",
    )(x, mods, gain.reshape(1, D_MODEL), w)


def _prep_kernel(cur_ref, prev_ref, next_ref, sm_ref, cw_ref, ap_ref, q_ref, k_ref, v_ref, p_ref, ext_ref):
    j = pl.program_id(1)
    nj = pl.num_programs(1)
    tl = cur_ref.shape[1]
    halo = SUBLANES
    pad = CONV_W // 2
    ext_ref[0:halo, :] = jnp.where(j > 0, prev_ref[0], 0.0)
    ext_ref[halo:halo + tl, :] = cur_ref[0]
    ext_ref[halo + tl:2 * halo + tl, :] = jnp.where(j < nj - 1, next_ref[0], 0.0)
    acc = None
    for w in range(CONV_W):
        term = ext_ref[pl.ds(halo - pad + w, tl), :] * cw_ref[w:w + 1, :]
        acc = term if acc is None else acc + term
    s = _silu(acc)
    for h in range(DN_HEADS):
        lo = h * DN_HEAD_DIM
        qh = s[:, lo:lo + DN_HEAD_DIM]
        kh = s[:, DN_WIDTH + lo:DN_WIDTH + lo + DN_HEAD_DIM]
        q_ref[0, :, lo:lo + DN_HEAD_DIM] = (qh * lax.rsqrt(jnp.sum(qh * qh, axis=-1, keepdims=True) + NORM_EPS)
                                            * DN_HEAD_DIM ** -0.5)
        k_ref[0, :, lo:lo + DN_HEAD_DIM] = kh * lax.rsqrt(jnp.sum(kh * kh, axis=-1, keepdims=True) + NORM_EPS)
    v_ref[0] = s[:, 2 * DN_WIDTH:]
    sm = sm_ref[0]
    lane = lax.broadcasted_iota(jnp.int32, sm.shape, 1)
    beta = _sigmoid(sm)
    gdec = ap_ref[0:1, :] * _softplus(sm + ap_ref[1:2, :])
    p_ref[0] = jnp.where(lane < GDEC_LANE, beta, jnp.where(lane < GLR_LANE, gdec, sm))


def _prep(qkv, sm, cw, ap, tl):
    b_, t_, _ = qkv.shape
    nb = t_ // tl
    r8 = tl // SUBLANES
    last8 = t_ // SUBLANES - 1
    tok = lambda wd: pl.BlockSpec((1, tl, wd), lambda b, j: (b, j, 0))
    return pl.pallas_call(
        _prep_kernel,
        grid=(b_, nb),
        in_specs=[tok(QKV_W),
                  pl.BlockSpec((1, SUBLANES, QKV_W), lambda b, j: (b, jnp.maximum(j * r8 - 1, 0), 0)),
                  pl.BlockSpec((1, SUBLANES, QKV_W), lambda b, j: (b, jnp.minimum((j + 1) * r8, last8), 0)),
                  tok(SMALL_W),
                  pl.BlockSpec((SUBLANES, QKV_W), lambda b, j: (0, 0)),
                  pl.BlockSpec((SUBLANES, SMALL_W), lambda b, j: (0, 0))],
        out_specs=[tok(DN_WIDTH), tok(DN_WIDTH), tok(DN_WIDTH), tok(SMALL_W)],
        out_shape=[jax.ShapeDtypeStruct((b_, t_, DN_WIDTH), F32)] * 3 + [jax.ShapeDtypeStruct((b_, t_, SMALL_W), F32)],
        scratch_shapes=[pltpu.VMEM((tl + 2 * SUBLANES, QKV_W), F32)],
        compiler_params=_cparams(("parallel", "parallel")),
        name="prep",
    )(qkv, qkv, qkv, sm, cw, ap)


def _order_masks(n, rev):
    ri = lax.broadcasted_iota(jnp.int32, (n, n), 0)
    ci = lax.broadcasted_iota(jnp.int32, (n, n), 1)
    if rev:
        return ri, ci, ci >= ri, ci > ri
    return ri, ci, ci <= ri, ci < ri


def _dn_kernel(q_ref, k_ref, v_ref, p_ref, s0_ref, o_ref, st_ref, s_scr, *, rev):
    j = pl.program_id(1)
    nj = pl.num_programs(1)

    @pl.when(j == 0)
    def _():
        s_scr[...] = s0_ref[0]

    c_ = CHUNK
    tb = q_ref.shape[1]
    nchunk = tb // c_
    d = 1 if rev else 0
    ri, ci, incl, strict = _order_masks(c_, rev)
    eye = (ri == ci).astype(F32)
    rb, cb, incl_b, _ = _order_masks(tb, rev)
    cum_mat = (incl_b & ((rb // c_) == (cb // c_))).astype(F32)
    gcum_all = jnp.dot(cum_mat, p_ref[0], precision=HIGHEST, preferred_element_type=F32)
    gcum_all_t = gcum_all.T
    level_masks = []
    s = 1
    while s < c_:
        level_masks.append(strict & ((ri // (2 * s)) == (ci // (2 * s))) & ((ri // s) != (ci // s)))
        s *= 2
    last = 0 if rev else c_ - 1

    for cc in range(nchunk):
        c = nchunk - 1 - cc if rev else cc
        r0 = c * c_
        pc = p_ref[0, r0:r0 + c_, :]
        gcum = gcum_all[r0:r0 + c_, :]
        for h in range(DN_HEADS):
            lo = h * DN_HEAD_DIM
            lb = BETA_LANE + d * DN_HEADS + h
            lg = GDEC_LANE + d * DN_HEADS + h
            beta = pc[:, lb:lb + 1]
            gc = gcum[:, lg:lg + 1]
            gr = gcum_all_t[lg:lg + 1, r0:r0 + c_]
            glast = gcum[last:last + 1, lg:lg + 1]
            decay = jnp.exp(jnp.where(incl, gc - gr, -jnp.inf))
            qh = q_ref[0, r0:r0 + c_, lo:lo + DN_HEAD_DIM]
            kh = k_ref[0, r0:r0 + c_, lo:lo + DN_HEAD_DIM]
            vh = v_ref[0, r0:r0 + c_, lo:lo + DN_HEAD_DIM]
            kb = kh * beta
            kq = _bdot_nt(jnp.concatenate([kb, qh], axis=0), kh)
            a = jnp.where(strict, kq[:c_] * decay, 0.0)
            attn = kq[c_:] * decay
            t = eye - jnp.where(level_masks[0], a, 0.0)
            for m in level_masks[1:]:
                t = t - _bdot(_bdot(t, jnp.where(m, a, 0.0)), t)
            eg = jnp.exp(gc)
            uw = _bdot(t, jnp.concatenate([vh * beta, kb * eg], axis=1))
            st = s_scr[h]
            ws = _bdot(jnp.concatenate([uw[:, DN_HEAD_DIM:], qh * eg], axis=0), st)
            v_new = uw[:, :DN_HEAD_DIM] - ws[:c_]
            o_ref[0, r0:r0 + c_, lo:lo + DN_HEAD_DIM] = ws[c_:] + _bdot(attn, v_new)
            s_scr[h] = st * jnp.exp(glast) + _bdot_tn(kh * jnp.exp(glast - gc), v_new)

    @pl.when(j == nj - 1)
    def _():
        st_ref[0] = s_scr[...]


def _dn_scan(q, k, v, p, s0, rev, tb):
    b_, t_, _ = q.shape
    nb = t_ // tb
    blk = (lambda b, j: (b, nb - 1 - j, 0)) if rev else (lambda b, j: (b, j, 0))
    tok = lambda wd: pl.BlockSpec((1, tb, wd), blk)
    st_spec = pl.BlockSpec((1, DN_HEADS, DN_HEAD_DIM, DN_HEAD_DIM), lambda b, j: (b, 0, 0, 0))
    return pl.pallas_call(
        functools.partial(_dn_kernel, rev=rev),
        grid=(b_, nb),
        in_specs=[tok(DN_WIDTH), tok(DN_WIDTH), tok(DN_WIDTH), tok(SMALL_W), st_spec],
        out_specs=[tok(DN_WIDTH), st_spec],
        out_shape=[jax.ShapeDtypeStruct((b_, t_, DN_WIDTH), F32),
                   jax.ShapeDtypeStruct((b_, DN_HEADS, DN_HEAD_DIM, DN_HEAD_DIM), F32)],
        scratch_shapes=[pltpu.VMEM((DN_HEADS, DN_HEAD_DIM, DN_HEAD_DIM), F32)],
        compiler_params=_cparams(("parallel", "arbitrary")),
        name="dn_rev" if rev else "dn_fwd",
    )(q, k, v, p, s0)


def _log_sigmoid(t):
    return jnp.minimum(t, 0.0) - jnp.log(1.0 + jnp.exp(-jnp.abs(t)))


def _gla_kernel(qk_ref, v_ref, p_ref, wg_ref, bg_ref, s0_ref, o_ref, st_ref, s_scr, *, rev):
    col = pl.program_id(1)
    rb = pl.program_id(2)
    ncol = pl.num_programs(1)
    nrb = pl.num_programs(2)

    @pl.when((col == 0) & (rb == 0))
    def _():
        s_scr[...] = s0_ref[0]

    c_ = CHUNK
    nchunk = qk_ref.shape[1] // c_
    nsub = c_ // SUB
    ri, ci, incl, _ = _order_masks(c_, rev)
    incl_f = incl.astype(F32)
    pos_r = (c_ - 1 - ri) if rev else ri
    pos_c = (c_ - 1 - ci) if rev else ci
    mask_diag = incl & ((pos_r // SUB) == (pos_c // SUB))
    mask_off = (pos_c // SUB) < (pos_r // SUB)
    last = 0 if rev else c_ - 1

    gate = jnp.dot(p_ref[0], wg_ref[...], precision=HIGHEST, preferred_element_type=F32) + bg_ref[...]
    la_all = _log_sigmoid(gate) * (1.0 / GLA_GATE_NORMALIZER)

    for cc in range(nchunk):
        c = nchunk - 1 - cc if rev else cc
        r0 = c * c_
        la = la_all[r0:r0 + c_, :]
        bc = jnp.dot(incl_f, la, precision=HIGHEST, preferred_element_type=F32)
        bex = bc - la
        b_last = bc[last:last + 1, :]
        first_rows = [(c_ - 1 - SUB * i) if rev else SUB * i for i in range(nsub)]
        pieces = [jnp.broadcast_to(bex[fr:fr + 1, :], (SUB, GLA_K_WIDTH)) for fr in first_rows]
        refd = jnp.concatenate(pieces[::-1] if rev else pieces, axis=0)
        mid = first_rows[nsub // 2]
        rmid = bex[mid:mid + 1, :]
        qa = qk_ref[0, r0:r0 + c_, 0:GLA_K_WIDTH] * GLA_DK ** -0.5
        ka = qk_ref[0, r0:r0 + c_, GLA_K_WIDTH:2 * GLA_K_WIDTH]
        q_in = qa * jnp.exp(bc)
        k_st = ka * jnp.exp(b_last - bc)
        q_d = qa * jnp.exp(bc - refd)
        k_d = ka * jnp.exp(jnp.minimum(refd - bc, EXP_CLAMP))
        q_o = qa * jnp.exp(jnp.minimum(bc - rmid, EXP_CLAMP))
        k_o = ka * jnp.exp(jnp.minimum(rmid - bc, EXP_CLAMP))
        e_last = jnp.exp(b_last)
        for h in range(GLA_HEADS):
            ks = slice(h * GLA_DK, (h + 1) * GLA_DK)
            vs = slice(h * GLA_DV, (h + 1) * GLA_DV)
            vh = v_ref[0, r0:r0 + c_, vs]
            st = s_scr[h]
            attn = (jnp.where(mask_diag, _bdot_nt(q_d[:, ks], k_d[:, ks]), 0.0)
                    + jnp.where(mask_off, _bdot_nt(q_o[:, ks], k_o[:, ks]), 0.0))
            o_ref[0, r0:r0 + c_, vs] = _bdot_nt(q_in[:, ks], st) + _bdot(attn, vh)
            s_scr[h] = st * e_last[:, ks] + _bdot_tn(vh, k_st[:, ks])

    @pl.when((col == ncol - 1) & (rb == nrb - 1))
    def _():
        st_ref[0] = s_scr[...]


def _gla_scan(qk, v, p, wg, bg, s0, rev, ncol, tr):
    b_, rows, _ = qk.shape
    nrb = rows // tr
    if rev:
        blk = lambda b, c, r: (b, nrb - 1 - r, ncol - 1 - c)
    else:
        blk = lambda b, c, r: (b, r, c)
    tok = lambda wd: pl.BlockSpec((1, tr, wd), blk)
    st_spec = pl.BlockSpec((1, GLA_HEADS, GLA_DV, GLA_DK), lambda b, c, r: (b, 0, 0, 0))
    return pl.pallas_call(
        functools.partial(_gla_kernel, rev=rev),
        grid=(b_, ncol, nrb),
        in_specs=[tok(2 * GLA_K_WIDTH), tok(GLA_V_WIDTH), tok(SMALL_W),
                  pl.BlockSpec((SMALL_W, GLA_K_WIDTH), lambda b, c, r: (0, 0)),
                  pl.BlockSpec((1, GLA_K_WIDTH), lambda b, c, r: (0, 0)),
                  st_spec],
        out_specs=[tok(GLA_V_WIDTH), st_spec],
        out_shape=[jax.ShapeDtypeStruct((b_, rows, ncol * GLA_V_WIDTH), F32),
                   jax.ShapeDtypeStruct((b_, GLA_HEADS, GLA_DV, GLA_DK), F32)],
        scratch_shapes=[pltpu.VMEM((GLA_HEADS, GLA_DV, GLA_DK), F32)],
        compiler_params=_cparams(("parallel", "arbitrary", "arbitrary")),
        name="gla_rev" if rev else "gla_fwd",
    )(qk, v, p, wg, bg, s0)


def _head_norm_gate(o, z, gain):
    parts = []
    for h in range(o.shape[1] // LANES):
        hs = slice(h * LANES, (h + 1) * LANES)
        parts.append(_rms(o[:, hs], gain) * _silu(z[:, hs]))
    return jnp.concatenate(parts, axis=1)


def _mixout_kernel(*refs, with_router):
    (x_ref, odf_ref, odb_ref, dnz_ref, ogf_ref, ogb_ref, glz_ref, mod_ref, gd_ref, gg_ref, w_ref, gf_ref) = refs[:12]
    if with_router:
        rw_ref, xo_ref, h_ref, cmb_ref = refs[12:]
    else:
        xo_ref, h_ref = refs[12:]
    m_dn = _head_norm_gate(odf_ref[0] + odb_ref[0], dnz_ref[0], gd_ref[...])
    m_gl = _head_norm_gate(ogf_ref[0] + ogb_ref[0], glz_ref[0], gg_ref[...])
    m = jnp.concatenate([m_dn, m_gl], axis=1).astype(BF16)
    x = x_ref[0] + mod_ref[0, 2:3, :] * jnp.dot(m, w_ref[...], preferred_element_type=F32)
    xo_ref[0] = x
    h = _rms(x, gf_ref[...]) * (1.0 + mod_ref[0, 4:5, :]) + mod_ref[0, 3:4, :]
    h_ref[0] = h.astype(BF16)
    if with_router:
        logits = jnp.dot(h, rw_ref[...], precision=HIGHEST, preferred_element_type=F32)
        lane = lax.broadcasted_iota(jnp.int32, logits.shape, 1).astype(F32)
        logits = jnp.where(lane < N_EXPERTS, logits, -jnp.inf)
        e = jnp.exp(logits - jnp.max(logits, axis=-1, keepdims=True))
        probs = e / jnp.sum(e, axis=-1, keepdims=True)
        p1 = jnp.max(probs, axis=-1, keepdims=True)
        i1 = jnp.min(jnp.where(probs == p1, lane, float(LANES)), axis=-1, keepdims=True)
        rest = jnp.where(lane == i1, -1.0, probs)
        p2 = jnp.max(rest, axis=-1, keepdims=True)
        i2 = jnp.min(jnp.where(rest == p2, lane, float(LANES)), axis=-1, keepdims=True)
        cmb_ref[0] = (jnp.where(lane == i1, p1, 0.0) + jnp.where(lane == i2, p2, 0.0)) / (p1 + p2)


def _mixout(x, odf, odb, dnz, ogf, ogb, glz, mods, mod_row, gd, gg, w_out, gf, router, tm):
    b_, t_, _ = x.shape
    with_router = router is not None
    tok = lambda wd: pl.BlockSpec((1, tm, wd), lambda b, j: (b, j, 0))
    const = lambda shp: pl.BlockSpec(shp, lambda b, j: (0,) * len(shp))
    in_specs = [tok(D_MODEL)] + [tok(DN_WIDTH)] * 6 + [
        pl.BlockSpec((1, SUBLANES, D_MODEL), lambda b, j: (mod_row(b), 0, 0)),
        const((1, LANES)), const((1, LANES)), const((D_MODEL, D_MODEL)), const((1, D_MODEL))]
    args = [x, odf, odb, dnz, ogf, ogb, glz, mods, gd.reshape(1, LANES), gg.reshape(1, LANES), w_out,
            gf.reshape(1, D_MODEL)]
    out_specs = [tok(D_MODEL), tok(D_MODEL)]
    out_shape = [jax.ShapeDtypeStruct((b_, t_, D_MODEL), F32), jax.ShapeDtypeStruct((b_, t_, D_MODEL), BF16)]
    if with_router:
        in_specs.append(const((D_MODEL, LANES)))
        args.append(router)
        out_specs.append(tok(LANES))
        out_shape.append(jax.ShapeDtypeStruct((b_, t_, LANES), F32))
    return pl.pallas_call(
        functools.partial(_mixout_kernel, with_router=with_router),
        grid=(b_, t_ // tm),
        in_specs=in_specs,
        out_specs=out_specs,
        out_shape=out_shape,
        compiler_params=_cparams(("parallel", "parallel")),
        name="mixout",
    )(*args)


def _ffn_kernel(x_ref, h_ref, mod_ref, wg_ref, wu_ref, wd_ref, o_ref, *, nsplit):
    h = h_ref[0]
    f = wg_ref.shape[1]
    fs = f // nsplit
    acc = None
    for s in range(nsplit):
        g = jnp.dot(h, wg_ref[:, s * fs:(s + 1) * fs], preferred_element_type=F32)
        u = jnp.dot(h, wu_ref[:, s * fs:(s + 1) * fs], preferred_element_type=F32)
        part = jnp.dot((_silu(g) * u).astype(BF16), wd_ref[s * fs:(s + 1) * fs, :], preferred_element_type=F32)
        acc = part if acc is None else acc + part
    o_ref[0] = x_ref[0] + mod_ref[0, 5:6, :] * acc


def _ffn(x, h, mods, mod_row, w_gu, w_down, tm):
    b_, t_, _ = x.shape
    f = w_down.shape[0]
    tok = lambda: pl.BlockSpec((1, tm, D_MODEL), lambda b, j: (b, j, 0))
    return pl.pallas_call(
        functools.partial(_ffn_kernel, nsplit=2),
        grid=(b_, t_ // tm),
        in_specs=[tok(), tok(),
                  pl.BlockSpec((1, SUBLANES, D_MODEL), lambda b, j: (mod_row(b), 0, 0)),
                  pl.BlockSpec((D_MODEL, f), lambda b, j: (0, 0)),
                  pl.BlockSpec((D_MODEL, f), lambda b, j: (0, 1)),
                  pl.BlockSpec((f, D_MODEL), lambda b, j: (0, 0))],
        out_specs=tok(),
        out_shape=jax.ShapeDtypeStruct((b_, t_, D_MODEL), F32),
        compiler_params=_cparams(("parallel", "parallel")),
        name="ffn",
    )(x, h, mods, w_gu, w_gu, w_down)


def _moe_kernel(x_ref, h_ref, cmb_ref, mod_ref, wg_ref, wu_ref, wd_ref, fg_ref, o_ref, acc_ref, act_ref, *,
                fsub, final_norm):
    e = pl.program_id(2)
    f = pl.program_id(3)
    ne = pl.num_programs(2)
    nf = pl.num_programs(3)

    @pl.when((e == 0) & (f == 0))
    def _():
        acc_ref[...] = jnp.zeros_like(acc_ref)

    h = h_ref[0]
    tf = wg_ref.shape[2]
    for s in range(tf // fsub):
        g = jnp.dot(h, wg_ref[0, :, s * fsub:(s + 1) * fsub], preferred_element_type=F32)
        u = jnp.dot(h, wu_ref[0, :, s * fsub:(s + 1) * fsub], preferred_element_type=F32)
        act_ref[:, s * fsub:(s + 1) * fsub] = (_silu(g) * u).astype(BF16)
    part = jnp.dot(act_ref[...], wd_ref[0], preferred_element_type=F32)
    cmb = cmb_ref[0]
    lane = lax.broadcasted_iota(jnp.int32, cmb.shape, 1)
    ce = jnp.sum(jnp.where(lane == e, cmb, 0.0), axis=-1, keepdims=True)
    acc_ref[...] += ce * part

    @pl.when((e == ne - 1) & (f == nf - 1))
    def _():
        y = x_ref[0] + mod_ref[0, 5:6, :] * acc_ref[...]
        if final_norm:
            y = _rms(y, fg_ref[...])
        o_ref[0] = y


def _moe(x, h, cmb, mods, mod_row, w_gu, w_down, final_gain, tm, tf, fsub):
    b_, t_, _ = x.shape
    ne, _, f2 = w_gu.shape
    fe = f2 // 2
    assert fe % tf == 0 and tf % fsub == 0
    nf = fe // tf
    tok = lambda wd: pl.BlockSpec((1, tm, wd), lambda b, j, e, f: (b, j, 0))
    final_norm = final_gain is not None
    fg = (final_gain if final_norm else jnp.ones((D_MODEL,), F32)).reshape(1, D_MODEL)
    return pl.pallas_call(
        functools.partial(_moe_kernel, fsub=fsub, final_norm=final_norm),
        grid=(b_, t_ // tm, ne, nf),
        in_specs=[tok(D_MODEL), tok(D_MODEL), tok(LANES),
                  pl.BlockSpec((1, SUBLANES, D_MODEL), lambda b, j, e, f: (mod_row(b), 0, 0)),
                  pl.BlockSpec((1, D_MODEL, tf), lambda b, j, e, f: (e, 0, f)),
                  pl.BlockSpec((1, D_MODEL, tf), lambda b, j, e, f: (e, 0, nf + f)),
                  pl.BlockSpec((1, tf, D_MODEL), lambda b, j, e, f: (e, f, 0)),
                  pl.BlockSpec((1, D_MODEL), lambda b, j, e, f: (0, 0))],
        out_specs=tok(D_MODEL),
        out_shape=jax.ShapeDtypeStruct((b_, t_, D_MODEL), F32),
        scratch_shapes=[pltpu.VMEM((tm, D_MODEL), F32), pltpu.VMEM((tm, tf), BF16)],
        compiler_params=_cparams(("parallel", "parallel", "arbitrary", "arbitrary")),
        name="moe",
    )(x, h, cmb, mods, w_gu, w_gu, w_down, fg)


def _pack_w_in(w):
    offs = np.cumsum((0, QKV_W, DN_WIDTH, 2 * DN_HEADS, 2 * DN_HEADS, GLA_K_WIDTH, GLA_K_WIDTH, GLA_V_WIDTH,
                      GLA_V_WIDTH, 2 * GLA_GATE_RANK))
    seg = lambda i: w[:, offs[i]:offs[i + 1]]
    small = jnp.concatenate([seg(2), seg(3), seg(8)], axis=1)
    small = jnp.pad(small, ((0, 0), (0, SMALL_W - small.shape[1])))
    return jnp.concatenate([seg(0), seg(1), seg(4), seg(5), seg(6), seg(7), small], axis=1).astype(BF16)


def _pack_gate(w_gate2, b_gate):
    ws = []
    for d in range(2):
        lo = GLR_LANE + d * GLA_GATE_RANK
        ws.append(jnp.zeros((SMALL_W, GLA_K_WIDTH), F32).at[lo:lo + GLA_GATE_RANK].set(w_gate2[d]))
    return ws, [b_gate[d].reshape(1, GLA_K_WIDTH) for d in range(2)]


def _pack_dn_params(a_log, dt_bias):
    neg_a = jnp.zeros((SMALL_W,), F32).at[GDEC_LANE:GLR_LANE].set(-jnp.exp(a_log.reshape(-1)))
    dtb = jnp.zeros((SMALL_W,), F32).at[GDEC_LANE:GLR_LANE].set(dt_bias.reshape(-1))
    return jnp.zeros((SUBLANES, SMALL_W), F32).at[0].set(neg_a).at[1].set(dtb)


def _pad_rows(a, rows):
    return jnp.pad(a, ((0, rows - a.shape[0]),) + ((0, 0),) * (a.ndim - 1))


def _mixers(x_seq, mods, mod_row, gain_mix, w_in_p, cw, ap, wgs, bgs, states, ncol, tm, tb):
    b_, t_, _ = x_seq.shape
    qkv, dnz, gqk, gv, gz, sm = _proj(x_seq, mods, mod_row, gain_mix, w_in_p, tm)
    q, k, v, p = _prep(qkv, sm, cw, ap, tm)
    rows = t_ // ncol
    col_view = lambda a: a.reshape(b_, rows, ncol * a.shape[2])
    outs, new_states = [], []
    for d in range(2):
        o_dn, s_dn = _dn_scan(q, k, v, p, states[0][d], bool(d), tb)
        o_gl, s_gl = _gla_scan(col_view(gqk), col_view(gv), col_view(p), wgs[d], bgs[d], states[1][d], bool(d), ncol,
                               min(rows, 2 * CHUNK))
        outs.append((o_dn, o_gl.reshape(b_, t_, GLA_V_WIDTH)))
        new_states.append((s_dn, s_gl))
    states_out = ([new_states[0][0], new_states[1][0]], [new_states[0][1], new_states[1][1]])
    return outs, dnz, gz, states_out


def kernel(x, c, ctx, c_ctx, w_ada, b_ada, norm_mix, norm_ffn, w_in, conv_qkv, dn_a_log, dn_dt_bias, dn_norm,
           gla_w_gate2, gla_b_gate, gla_norm, w_out, ffn_w_gu, ffn_w_down, moe_router, moe_w_gu, moe_w_down,
           final_norm):
    b_, seq, _ = x.shape
    depth = w_ada.shape[0]
    ctx_len = ctx.shape[1]
    mod_rows = -(-(b_ + 1) // SUBLANES) * SUBLANES
    c_rows = _pad_rows(jnp.concatenate([c, c_ctx[None, :]], axis=0), mod_rows)
    lat_row = lambda b: b
    ctx_row = lambda b: b_
    tm_lat, tm_ctx = 512, ctx_len
    tb = 4 * CHUNK

    for i in range(depth):
        last = i == depth - 1
        mods = _ada(c_rows, w_ada[i], b_ada[i]).reshape(mod_rows, N_MOD, D_MODEL)
        mods = jnp.pad(mods, ((0, 0), (0, SUBLANES - N_MOD), (0, 0)))
        w_in_p = _pack_w_in(w_in[i])
        cw = _pad_rows(conv_qkv[i], SUBLANES)
        ap = _pack_dn_params(dn_a_log[i], dn_dt_bias[i])
        wgs, bgs = _pack_gate(gla_w_gate2[i], gla_b_gate[i])
        s0 = ([jnp.zeros((b_, DN_HEADS, DN_HEAD_DIM, DN_HEAD_DIM), F32)] * 2,
              [jnp.zeros((b_, GLA_HEADS, GLA_DV, GLA_DK), F32)] * 2)
        outs_c, dnz_c, gz_c, s_ctx = _mixers(ctx, mods, ctx_row, norm_mix[i], w_in_p, cw, ap, wgs, bgs, s0, 1,
                                             tm_ctx, tb)
        outs_l, dnz_l, gz_l, _ = _mixers(x, mods, lat_row, norm_mix[i], w_in_p, cw, ap, wgs, bgs, s_ctx, GRID_W,
                                         tm_lat, tb)
        w_out_b = w_out[i].astype(BF16)
        is_moe = i % 2 == 1
        if is_moe:
            router = jnp.pad(moe_router[i // 2], ((0, 0), (0, LANES - N_EXPERTS)))
            w_gu = moe_w_gu[i // 2].astype(BF16)
            w_dn = moe_w_down[i // 2].astype(BF16)
        else:
            router = None
            w_gu = ffn_w_gu[i // 2].astype(BF16)
            w_dn = ffn_w_down[i // 2].astype(BF16)

        def channel(tokens, outs, dnz, gz, row, tm, final_gain):
            res = _mixout(tokens, outs[0][0], outs[1][0], dnz, outs[0][1], outs[1][1], gz, mods, row, dn_norm[i],
                          gla_norm[i], w_out_b, norm_ffn[i], router, tm)
            if is_moe:
                x_mid, h, cmb = res
                tm_moe = min(1024, tokens.shape[1])
                return _moe(x_mid, h, cmb, mods, row, w_gu, w_dn, final_gain, tm_moe, D_EXPERT_TILE, D_EXPERT_TILE)
            x_mid, h = res
            y = _ffn(x_mid, h, mods, row, w_gu, w_dn, tm)
            if final_gain is not None:
                raise NotImplementedError("final norm is fused into the expert mixer only")
            return y

        x = channel(x, outs_l, dnz_l, gz_l, lat_row, tm_lat, final_norm if last else None)
        if not last:
            ctx = channel(ctx, outs_c, dnz_c, gz_c, ctx_row, tm_ctx, None)
    return x
```

```python
import functools

import jax
import jax.numpy as jnp
import numpy as np
from jax import lax
from jax.experimental import pallas as pl
from jax.experimental.pallas import tpu as pltpu

F32 = jnp.float32
BF16 = jnp.bfloat16
HIGHEST = lax.Precision.HIGHEST

D_MODEL = 1024
GRID_W = 64
DN_HEADS = 4
DN_HEAD_DIM = 128
DN_WIDTH = DN_HEADS * DN_HEAD_DIM
CONV_W = 5
GLA_HEADS = 4
GLA_DK = 64
GLA_DV = 128
GLA_K_WIDTH = GLA_HEADS * GLA_DK
GLA_V_WIDTH = GLA_HEADS * GLA_DV
GLA_GATE_RANK = 16
GLA_GATE_NORMALIZER = 16.0
N_EXPERTS = 8
N_MOD = 6
NORM_EPS = 1e-6

LANES = 128
SUBLANES = 8
CHUNK = 64
SUB = 16
EXP_CLAMP = 80.0
D_EXPERT_TILE = 512
VMEM_LIMIT = 56 * 1024 * 1024

QKV_W = 3 * DN_WIDTH
SMALL_W = LANES
PROJ_COLS = QKV_W + DN_WIDTH + 2 * GLA_K_WIDTH + 2 * GLA_V_WIDTH + SMALL_W
BETA_LANE = 0
GDEC_LANE = 2 * DN_HEADS
GLR_LANE = 4 * DN_HEADS


def _cparams(sem, vmem=VMEM_LIMIT):
    return pltpu.CompilerParams(dimension_semantics=sem, vmem_limit_bytes=vmem)


def _sigmoid(t):
    return 1.0 / (1.0 + jnp.exp(-t))


def _silu(t):
    return t * _sigmoid(t)


def _softplus(t):
    return jnp.maximum(t, 0.0) + jnp.log(1.0 + jnp.exp(-jnp.abs(t)))


def _bdot(a, b):
    return jnp.dot(a.astype(BF16), b.astype(BF16), preferred_element_type=F32)


def _bdot_nt(a, b):
    return lax.dot_general(a.astype(BF16), b.astype(BF16), (((1,), (1,)), ((), ())), preferred_element_type=F32)


def _bdot_tn(a, b):
    return lax.dot_general(a.astype(BF16), b.astype(BF16), (((0,), (0,)), ((), ())), preferred_element_type=F32)


def _rms(t, gain):
    return t * lax.rsqrt(jnp.mean(t * t, axis=-1, keepdims=True) + NORM_EPS) * gain


def _ada_kernel(c_ref, w_ref, b_ref, o_ref):
    s = _silu(c_ref[...])
    o_ref[...] = jnp.dot(s, w_ref[...], precision=HIGHEST, preferred_element_type=F32) + b_ref[...]


def _ada(c_rows, w, b):
    rows = c_rows.shape[0]
    n = w.shape[1]
    tn = D_MODEL
    return pl.pallas_call(
        _ada_kernel,
        grid=(n // tn,),
        in_specs=[pl.BlockSpec((rows, D_MODEL), lambda j: (0, 0)),
                  pl.BlockSpec((D_MODEL, tn), lambda j: (0, j)),
                  pl.BlockSpec((1, tn), lambda j: (0, j))],
        out_specs=pl.BlockSpec((rows, tn), lambda j: (0, j)),
        out_shape=jax.ShapeDtypeStruct((rows, n), F32),
        compiler_params=_cparams(("arbitrary",)),
        name="ada",
    )(c_rows, w, b.reshape(1, n))


def _proj_kernel(x_ref, mod_ref, gain_ref, w_ref, qkv_ref, dnz_ref, gqk_ref, gv_ref, gz_ref, sm_ref):
    x = x_ref[0]
    y = _rms(x, gain_ref[...])
    h = (y * (1.0 + mod_ref[0, 1:2, :]) + mod_ref[0, 0:1, :]).astype(BF16)
    r = jnp.dot(h, w_ref[...], preferred_element_type=F32)
    o = 0
    for ref in (qkv_ref, dnz_ref, gqk_ref, gv_ref, gz_ref, sm_ref):
        wd = ref.shape[2]
        ref[0] = r[:, o:o + wd]
        o += wd


def _proj(x, mods, mod_row, gain, w, tm):
    b_, t_, _ = x.shape
    widths = (QKV_W, DN_WIDTH, 2 * GLA_K_WIDTH, GLA_V_WIDTH, GLA_V_WIDTH, SMALL_W)
    tok = lambda wd: pl.BlockSpec((1, tm, wd), lambda b, j: (b, j, 0))
    return pl.pallas_call(
        _proj_kernel,
        grid=(b_, t_ // tm),
        in_specs=[tok(D_MODEL),
                  pl.BlockSpec((1, SUBLANES, D_MODEL), lambda b, j: (mod_row(b), 0, 0)),
                  pl.BlockSpec((1, D_MODEL), lambda b, j: (0, 0)),
                  pl.BlockSpec((D_MODEL, PROJ_COLS), lambda b, j: (0, 0))],
        out_specs=[tok(wd) for wd in widths],
        out_shape=[jax.ShapeDtypeStruct((b_, t_, wd), F32) for wd in widths],
        compiler_params=_cparams(("parallel", "parallel")),
        name="proj",
    )(x, mods, gain.reshape(1, D_MODEL), w)


def _prep_kernel(cur_ref, prev_ref, next_ref, sm_ref, cw_ref, ap_ref, q_ref, k_ref, v_ref, p_ref, ext_ref):
    j = pl.program_id(1)
    nj = pl.num_programs(1)
    tl = cur_ref.shape[1]
    halo = SUBLANES
    pad = CONV_W // 2
    ext_ref[0:halo, :] = jnp.where(j > 0, prev_ref[0], 0.0)
    ext_ref[halo:halo + tl, :] = cur_ref[0]
    ext_ref[halo + tl:2 * halo + tl, :] = jnp.where(j < nj - 1, next_ref[0], 0.0)
    acc = None
    for w in range(CONV_W):
        term = ext_ref[pl.ds(halo - pad + w, tl), :] * cw_ref[w:w + 1, :]
        acc = term if acc is None else acc + term
    s = _silu(acc)
    for h in range(DN_HEADS):
        lo = h * DN_HEAD_DIM
        qh = s[:, lo:lo + DN_HEAD_DIM]
        kh = s[:, DN_WIDTH + lo:DN_WIDTH + lo + DN_HEAD_DIM]
        q_ref[0, :, lo:lo + DN_HEAD_DIM] = (qh * lax.rsqrt(jnp.sum(qh * qh, axis=-1, keepdims=True) + NORM_EPS)
                                            * DN_HEAD_DIM ** -0.5)
        k_ref[0, :, lo:lo + DN_HEAD_DIM] = kh * lax.rsqrt(jnp.sum(kh * kh, axis=-1, keepdims=True) + NORM_EPS)
    v_ref[0] = s[:, 2 * DN_WIDTH:]
    sm = sm_ref[0]
    lane = lax.broadcasted_iota(jnp.int32, sm.shape, 1)
    beta = _sigmoid(sm)
    gdec = ap_ref[0:1, :] * _softplus(sm + ap_ref[1:2, :])
    p_ref[0] = jnp.where(lane < GDEC_LANE, beta, jnp.where(lane < GLR_LANE, gdec, sm))


def _prep(qkv, sm, cw, ap, tl):
    b_, t_, _ = qkv.shape
    nb = t_ // tl
    r8 = tl // SUBLANES
    last8 = t_ // SUBLANES - 1
    tok = lambda wd: pl.BlockSpec((1, tl, wd), lambda b, j: (b, j, 0))
    return pl.pallas_call(
        _prep_kernel,
        grid=(b_, nb),
        in_specs=[tok(QKV_W),
                  pl.BlockSpec((1, SUBLANES, QKV_W), lambda b, j: (b, jnp.maximum(j * r8 - 1, 0), 0)),
                  pl.BlockSpec((1, SUBLANES, QKV_W), lambda b, j: (b, jnp.minimum((j + 1) * r8, last8), 0)),
                  tok(SMALL_W),
                  pl.BlockSpec((SUBLANES, QKV_W), lambda b, j: (0, 0)),
                  pl.BlockSpec((SUBLANES, SMALL_W), lambda b, j: (0, 0))],
        out_specs=[tok(DN_WIDTH), tok(DN_WIDTH), tok(DN_WIDTH), tok(SMALL_W)],
        out_shape=[jax.ShapeDtypeStruct((b_, t_, DN_WIDTH), F32)] * 3 + [jax.ShapeDtypeStruct((b_, t_, SMALL_W), F32)],
        scratch_shapes=[pltpu.VMEM((tl + 2 * SUBLANES, QKV_W), F32)],
        compiler_params=_cparams(("parallel", "parallel")),
        name="prep",
    )(qkv, qkv, qkv, sm, cw, ap)


def _order_masks(n, rev):
    ri = lax.broadcasted_iota(jnp.int32, (n, n), 0)
    ci = lax.broadcasted_iota(jnp.int32, (n, n), 1)
    if rev:
        return ri, ci, ci >= ri, ci > ri
    return ri, ci, ci <= ri, ci < ri


def _dn_kernel(q_ref, k_ref, v_ref, p_ref, s0_ref, o_ref, st_ref, s_scr, *, rev):
    j = pl.program_id(1)
    nj = pl.num_programs(1)

    @pl.when(j == 0)
    def _():
        s_scr[...] = s0_ref[0]

    c_ = CHUNK
    tb = q_ref.shape[1]
    nchunk = tb // c_
    d = 1 if rev else 0
    ri, ci, incl, strict = _order_masks(c_, rev)
    eye = (ri == ci).astype(F32)
    rb, cb, incl_b, _ = _order_masks(tb, rev)
    cum_mat = (incl_b & ((rb // c_) == (cb // c_))).astype(F32)
    gcum_all = jnp.dot(cum_mat, p_ref[0], precision=HIGHEST, preferred_element_type=F32)
    gcum_all_t = gcum_all.T
    level_masks = []
    s = 1
    while s < c_:
        level_masks.append(strict & ((ri // (2 * s)) == (ci // (2 * s))) & ((ri // s) != (ci // s)))
        s *= 2
    last = 0 if rev else c_ - 1

    order = [nchunk - 1 - cc if rev else cc for cc in range(nchunk)]
    inst = [(c, h) for c in order for h in range(DN_HEADS)]
    pre = {}
    for c, h in inst:
        r0 = c * c_
        lo = h * DN_HEAD_DIM
        lb = BETA_LANE + d * DN_HEADS + h
        lg = GDEC_LANE + d * DN_HEADS + h
        beta = p_ref[0, r0:r0 + c_, lb:lb + 1]
        gc = gcum_all[r0:r0 + c_, lg:lg + 1]
        gr = gcum_all_t[lg:lg + 1, r0:r0 + c_]
        glast = gcum_all[r0 + last:r0 + last + 1, lg:lg + 1]
        decay = jnp.exp(jnp.where(incl, gc - gr, -jnp.inf))
        qh = q_ref[0, r0:r0 + c_, lo:lo + DN_HEAD_DIM]
        kh = k_ref[0, r0:r0 + c_, lo:lo + DN_HEAD_DIM]
        vh = v_ref[0, r0:r0 + c_, lo:lo + DN_HEAD_DIM]
        kb = kh * beta
        eg = jnp.exp(gc)
        pre[c, h] = dict(decay=decay, kq_lhs=jnp.concatenate([kb, qh], axis=0), kh=kh,
                         rhs=jnp.concatenate([vh * beta, kb * eg], axis=1), q_dec=qh * eg,
                         k_dec=kh * jnp.exp(glast - gc), e_last=jnp.exp(glast))
    kq = {i: _bdot_nt(pre[i]["kq_lhs"], pre[i]["kh"]) for i in inst}
    a = {i: jnp.where(strict, kq[i][:c_] * pre[i]["decay"], 0.0) for i in inst}
    attn = {i: kq[i][c_:] * pre[i]["decay"] for i in inst}
    t = {i: eye - jnp.where(level_masks[0], a[i], 0.0) for i in inst}
    for m in level_masks[1:]:
        ta = {i: _bdot(t[i], jnp.where(m, a[i], 0.0)) for i in inst}
        t = {i: t[i] - _bdot(ta[i], t[i]) for i in inst}
    uw = {i: _bdot(t[i], pre[i]["rhs"]) for i in inst}
    for c in order:
        r0 = c * c_
        heads = [(c, h) for h in range(DN_HEADS)]
        st = {i: s_scr[i[1]] for i in heads}
        ws = {i: _bdot(jnp.concatenate([uw[i][:, DN_HEAD_DIM:], pre[i]["q_dec"]], axis=0), st[i]) for i in heads}
        v_new = {i: uw[i][:, :DN_HEAD_DIM] - ws[i][:c_] for i in heads}
        for i in heads:
            lo = i[1] * DN_HEAD_DIM
            o_ref[0, r0:r0 + c_, lo:lo + DN_HEAD_DIM] = ws[i][c_:] + _bdot(attn[i], v_new[i])
        for i in heads:
            s_scr[i[1]] = st[i] * pre[i]["e_last"] + _bdot_tn(pre[i]["k_dec"], v_new[i])

    @pl.when(j == nj - 1)
    def _():
        st_ref[0] = s_scr[...]


def _dn_scan(q, k, v, p, s0, rev, tb):
    b_, t_, _ = q.shape
    nb = t_ // tb
    blk = (lambda b, j: (b, nb - 1 - j, 0)) if rev else (lambda b, j: (b, j, 0))
    tok = lambda wd: pl.BlockSpec((1, tb, wd), blk)
    st_spec = pl.BlockSpec((1, DN_HEADS, DN_HEAD_DIM, DN_HEAD_DIM), lambda b, j: (b, 0, 0, 0))
    return pl.pallas_call(
        functools.partial(_dn_kernel, rev=rev),
        grid=(b_, nb),
        in_specs=[tok(DN_WIDTH), tok(DN_WIDTH), tok(DN_WIDTH), tok(SMALL_W), st_spec],
        out_specs=[tok(DN_WIDTH), st_spec],
        out_shape=[jax.ShapeDtypeStruct((b_, t_, DN_WIDTH), F32),
                   jax.ShapeDtypeStruct((b_, DN_HEADS, DN_HEAD_DIM, DN_HEAD_DIM), F32)],
        scratch_shapes=[pltpu.VMEM((DN_HEADS, DN_HEAD_DIM, DN_HEAD_DIM), F32)],
        compiler_params=_cparams(("parallel", "arbitrary")),
        name="dn_rev" if rev else "dn_fwd",
    )(q, k, v, p, s0)


def _log_sigmoid(t):
    return jnp.minimum(t, 0.0) - jnp.log(1.0 + jnp.exp(-jnp.abs(t)))


def _gla_kernel(qk_ref, v_ref, p_ref, wg_ref, bg_ref, s0_ref, o_ref, st_ref, s_scr, *, rev):
    col = pl.program_id(1)
    rb = pl.program_id(2)
    ncol = pl.num_programs(1)
    nrb = pl.num_programs(2)

    @pl.when((col == 0) & (rb == 0))
    def _():
        s_scr[...] = s0_ref[0]

    c_ = CHUNK
    nchunk = qk_ref.shape[1] // c_
    nsub = c_ // SUB
    ri, ci, incl, _ = _order_masks(c_, rev)
    incl_f = incl.astype(F32)
    pos_r = (c_ - 1 - ri) if rev else ri
    pos_c = (c_ - 1 - ci) if rev else ci
    mask_diag = incl & ((pos_r // SUB) == (pos_c // SUB))
    mask_off = (pos_c // SUB) < (pos_r // SUB)
    last = 0 if rev else c_ - 1

    gate = jnp.dot(p_ref[0], wg_ref[...], precision=HIGHEST, preferred_element_type=F32) + bg_ref[...]
    la_all = _log_sigmoid(gate) * (1.0 / GLA_GATE_NORMALIZER)

    order = [nchunk - 1 - cc if rev else cc for cc in range(nchunk)]
    pre = {}
    for c in order:
        r0 = c * c_
        la = la_all[r0:r0 + c_, :]
        bc = jnp.dot(incl_f, la, precision=HIGHEST, preferred_element_type=F32)
        bex = bc - la
        b_last = bc[last:last + 1, :]
        first_rows = [(c_ - 1 - SUB * i) if rev else SUB * i for i in range(nsub)]
        pieces = [jnp.broadcast_to(bex[fr:fr + 1, :], (SUB, GLA_K_WIDTH)) for fr in first_rows]
        refd = jnp.concatenate(pieces[::-1] if rev else pieces, axis=0)
        mid = first_rows[nsub // 2]
        rmid = bex[mid:mid + 1, :]
        qa = qk_ref[0, r0:r0 + c_, 0:GLA_K_WIDTH] * GLA_DK ** -0.5
        ka = qk_ref[0, r0:r0 + c_, GLA_K_WIDTH:2 * GLA_K_WIDTH]
        q_in = qa * jnp.exp(bc)
        k_st = ka * jnp.exp(b_last - bc)
        q_d = qa * jnp.exp(bc - refd)
        k_d = ka * jnp.exp(jnp.minimum(refd - bc, EXP_CLAMP))
        q_o = qa * jnp.exp(jnp.minimum(bc - rmid, EXP_CLAMP))
        k_o = ka * jnp.exp(jnp.minimum(rmid - bc, EXP_CLAMP))
        e_last = jnp.exp(b_last)
        for h in range(GLA_HEADS):
            ks = slice(h * GLA_DK, (h + 1) * GLA_DK)
            attn = (jnp.where(mask_diag, _bdot_nt(q_d[:, ks], k_d[:, ks]), 0.0)
                    + jnp.where(mask_off, _bdot_nt(q_o[:, ks], k_o[:, ks]), 0.0))
            pre[c, h] = dict(attn=attn, q_in=q_in[:, ks], k_st=k_st[:, ks], e_last=e_last[:, ks])
    for c in order:
        r0 = c * c_
        st = [s_scr[h] for h in range(GLA_HEADS)]
        vh = [v_ref[0, r0:r0 + c_, h * GLA_DV:(h + 1) * GLA_DV] for h in range(GLA_HEADS)]
        for h in range(GLA_HEADS):
            o_ref[0, r0:r0 + c_, h * GLA_DV:(h + 1) * GLA_DV] = (_bdot_nt(pre[c, h]["q_in"], st[h])
                                                                 + _bdot(pre[c, h]["attn"], vh[h]))
        for h in range(GLA_HEADS):
            s_scr[h] = st[h] * pre[c, h]["e_last"] + _bdot_tn(vh[h], pre[c, h]["k_st"])

    @pl.when((col == ncol - 1) & (rb == nrb - 1))
    def _():
        st_ref[0] = s_scr[...]


def _gla_scan(qk, v, p, wg, bg, s0, rev, ncol, tr):
    b_, rows, _ = qk.shape
    nrb = rows // tr
    if rev:
        blk = lambda b, c, r: (b, nrb - 1 - r, ncol - 1 - c)
    else:
        blk = lambda b, c, r: (b, r, c)
    tok = lambda wd: pl.BlockSpec((1, tr, wd), blk)
    st_spec = pl.BlockSpec((1, GLA_HEADS, GLA_DV, GLA_DK), lambda b, c, r: (b, 0, 0, 0))
    return pl.pallas_call(
        functools.partial(_gla_kernel, rev=rev),
        grid=(b_, ncol, nrb),
        in_specs=[tok(2 * GLA_K_WIDTH), tok(GLA_V_WIDTH), tok(SMALL_W),
                  pl.BlockSpec((SMALL_W, GLA_K_WIDTH), lambda b, c, r: (0, 0)),
                  pl.BlockSpec((1, GLA_K_WIDTH), lambda b, c, r: (0, 0)),
                  st_spec],
        out_specs=[tok(GLA_V_WIDTH), st_spec],
        out_shape=[jax.ShapeDtypeStruct((b_, rows, ncol * GLA_V_WIDTH), F32),
                   jax.ShapeDtypeStruct((b_, GLA_HEADS, GLA_DV, GLA_DK), F32)],
        scratch_shapes=[pltpu.VMEM((GLA_HEADS, GLA_DV, GLA_DK), F32)],
        compiler_params=_cparams(("parallel", "arbitrary", "arbitrary")),
        name="gla_rev" if rev else "gla_fwd",
    )(qk, v, p, wg, bg, s0)


def _head_norm_gate(o, z, gain):
    parts = []
    for h in range(o.shape[1] // LANES):
        hs = slice(h * LANES, (h + 1) * LANES)
        parts.append(_rms(o[:, hs], gain) * _silu(z[:, hs]))
    return jnp.concatenate(parts, axis=1)


def _mixout_kernel(*refs, with_router):
    (x_ref, odf_ref, odb_ref, dnz_ref, ogf_ref, ogb_ref, glz_ref, mod_ref, gd_ref, gg_ref, w_ref, gf_ref) = refs[:12]
    if with_router:
        rw_ref, xo_ref, h_ref, cmb_ref = refs[12:]
    else:
        xo_ref, h_ref = refs[12:]
    m_dn = _head_norm_gate(odf_ref[0] + odb_ref[0], dnz_ref[0], gd_ref[...])
    m_gl = _head_norm_gate(ogf_ref[0] + ogb_ref[0], glz_ref[0], gg_ref[...])
    m = jnp.concatenate([m_dn, m_gl], axis=1).astype(BF16)
    x = x_ref[0] + mod_ref[0, 2:3, :] * jnp.dot(m, w_ref[...], preferred_element_type=F32)
    xo_ref[0] = x
    h = _rms(x, gf_ref[...]) * (1.0 + mod_ref[0, 4:5, :]) + mod_ref[0, 3:4, :]
    h_ref[0] = h.astype(BF16)
    if with_router:
        logits = jnp.dot(h, rw_ref[...], precision=HIGHEST, preferred_element_type=F32)
        lane = lax.broadcasted_iota(jnp.int32, logits.shape, 1).astype(F32)
        logits = jnp.where(lane < N_EXPERTS, logits, -jnp.inf)
        e = jnp.exp(logits - jnp.max(logits, axis=-1, keepdims=True))
        probs = e / jnp.sum(e, axis=-1, keepdims=True)
        p1 = jnp.max(probs, axis=-1, keepdims=True)
        i1 = jnp.min(jnp.where(probs == p1, lane, float(LANES)), axis=-1, keepdims=True)
        rest = jnp.where(lane == i1, -1.0, probs)
        p2 = jnp.max(rest, axis=-1, keepdims=True)
        i2 = jnp.min(jnp.where(rest == p2, lane, float(LANES)), axis=-1, keepdims=True)
        cmb_ref[0] = (jnp.where(lane == i1, p1, 0.0) + jnp.where(lane == i2, p2, 0.0)) / (p1 + p2)


def _mixout(x, odf, odb, dnz, ogf, ogb, glz, mods, mod_row, gd, gg, w_out, gf, router, tm):
    b_, t_, _ = x.shape
    with_router = router is not None
    tok = lambda wd: pl.BlockSpec((1, tm, wd), lambda b, j: (b, j, 0))
    const = lambda shp: pl.BlockSpec(shp, lambda b, j: (0,) * len(shp))
    in_specs = [tok(D_MODEL)] + [tok(DN_WIDTH)] * 6 + [
        pl.BlockSpec((1, SUBLANES, D_MODEL), lambda b, j: (mod_row(b), 0, 0)),
        const((1, LANES)), const((1, LANES)), const((D_MODEL, D_MODEL)), const((1, D_MODEL))]
    args = [x, odf, odb, dnz, ogf, ogb, glz, mods, gd.reshape(1, LANES), gg.reshape(1, LANES), w_out,
            gf.reshape(1, D_MODEL)]
    out_specs = [tok(D_MODEL), tok(D_MODEL)]
    out_shape = [jax.ShapeDtypeStruct((b_, t_, D_MODEL), F32), jax.ShapeDtypeStruct((b_, t_, D_MODEL), BF16)]
    if with_router:
        in_specs.append(const((D_MODEL, LANES)))
        args.append(router)
        out_specs.append(tok(LANES))
        out_shape.append(jax.ShapeDtypeStruct((b_, t_, LANES), F32))
    return pl.pallas_call(
        functools.partial(_mixout_kernel, with_router=with_router),
        grid=(b_, t_ // tm),
        in_specs=in_specs,
        out_specs=out_specs,
        out_shape=out_shape,
        compiler_params=_cparams(("parallel", "parallel")),
        name="mixout",
    )(*args)


def _ffn_kernel(x_ref, h_ref, mod_ref, wg_ref, wu_ref, wd_ref, o_ref, *, nsplit):
    h = h_ref[0]
    f = wg_ref.shape[1]
    fs = f // nsplit
    acc = None
    for s in range(nsplit):
        g = jnp.dot(h, wg_ref[:, s * fs:(s + 1) * fs], preferred_element_type=F32)
        u = jnp.dot(h, wu_ref[:, s * fs:(s + 1) * fs], preferred_element_type=F32)
        part = jnp.dot((_silu(g) * u).astype(BF16), wd_ref[s * fs:(s + 1) * fs, :], preferred_element_type=F32)
        acc = part if acc is None else acc + part
    o_ref[0] = x_ref[0] + mod_ref[0, 5:6, :] * acc


def _ffn(x, h, mods, mod_row, w_gu, w_down, tm):
    b_, t_, _ = x.shape
    f = w_down.shape[0]
    tok = lambda: pl.BlockSpec((1, tm, D_MODEL), lambda b, j: (b, j, 0))
    return pl.pallas_call(
        functools.partial(_ffn_kernel, nsplit=2),
        grid=(b_, t_ // tm),
        in_specs=[tok(), tok(),
                  pl.BlockSpec((1, SUBLANES, D_MODEL), lambda b, j: (mod_row(b), 0, 0)),
                  pl.BlockSpec((D_MODEL, f), lambda b, j: (0, 0)),
                  pl.BlockSpec((D_MODEL, f), lambda b, j: (0, 1)),
                  pl.BlockSpec((f, D_MODEL), lambda b, j: (0, 0))],
        out_specs=tok(),
        out_shape=jax.ShapeDtypeStruct((b_, t_, D_MODEL), F32),
        compiler_params=_cparams(("parallel", "parallel")),
        name="ffn",
    )(x, h, mods, w_gu, w_gu, w_down)


def _moe_kernel(x_ref, h_ref, cmb_ref, mod_ref, wg_ref, wu_ref, wd_ref, fg_ref, o_ref, acc_ref, act_ref, *,
                fsub, final_norm):
    e = pl.program_id(2)
    f = pl.program_id(3)
    ne = pl.num_programs(2)
    nf = pl.num_programs(3)

    @pl.when((e == 0) & (f == 0))
    def _():
        acc_ref[...] = jnp.zeros_like(acc_ref)

    h = h_ref[0]
    tf = wg_ref.shape[2]
    for s in range(tf // fsub):
        g = jnp.dot(h, wg_ref[0, :, s * fsub:(s + 1) * fsub], preferred_element_type=F32)
        u = jnp.dot(h, wu_ref[0, :, s * fsub:(s + 1) * fsub], preferred_element_type=F32)
        act_ref[:, s * fsub:(s + 1) * fsub] = (_silu(g) * u).astype(BF16)
    part = jnp.dot(act_ref[...], wd_ref[0], preferred_element_type=F32)
    cmb = cmb_ref[0]
    lane = lax.broadcasted_iota(jnp.int32, cmb.shape, 1)
    ce = jnp.sum(jnp.where(lane == e, cmb, 0.0), axis=-1, keepdims=True)
    acc_ref[...] += ce * part

    @pl.when((e == ne - 1) & (f == nf - 1))
    def _():
        y = x_ref[0] + mod_ref[0, 5:6, :] * acc_ref[...]
        if final_norm:
            y = _rms(y, fg_ref[...])
        o_ref[0] = y


def _moe(x, h, cmb, mods, mod_row, w_gu, w_down, final_gain, tm, tf, fsub):
    b_, t_, _ = x.shape
    ne, _, f2 = w_gu.shape
    fe = f2 // 2
    assert fe % tf == 0 and tf % fsub == 0
    nf = fe // tf
    tok = lambda wd: pl.BlockSpec((1, tm, wd), lambda b, j, e, f: (b, j, 0))
    final_norm = final_gain is not None
    fg = (final_gain if final_norm else jnp.ones((D_MODEL,), F32)).reshape(1, D_MODEL)
    return pl.pallas_call(
        functools.partial(_moe_kernel, fsub=fsub, final_norm=final_norm),
        grid=(b_, t_ // tm, ne, nf),
        in_specs=[tok(D_MODEL), tok(D_MODEL), tok(LANES),
                  pl.BlockSpec((1, SUBLANES, D_MODEL), lambda b, j, e, f: (mod_row(b), 0, 0)),
                  pl.BlockSpec((1, D_MODEL, tf), lambda b, j, e, f: (e, 0, f)),
                  pl.BlockSpec((1, D_MODEL, tf), lambda b, j, e, f: (e, 0, nf + f)),
                  pl.BlockSpec((1, tf, D_MODEL), lambda b, j, e, f: (e, f, 0)),
                  pl.BlockSpec((1, D_MODEL), lambda b, j, e, f: (0, 0))],
        out_specs=tok(D_MODEL),
        out_shape=jax.ShapeDtypeStruct((b_, t_, D_MODEL), F32),
        scratch_shapes=[pltpu.VMEM((tm, D_MODEL), F32), pltpu.VMEM((tm, tf), BF16)],
        compiler_params=_cparams(("parallel", "parallel", "arbitrary", "arbitrary")),
        name="moe",
    )(x, h, cmb, mods, w_gu, w_gu, w_down, fg)


def _pack_w_in(w):
    offs = np.cumsum((0, QKV_W, DN_WIDTH, 2 * DN_HEADS, 2 * DN_HEADS, GLA_K_WIDTH, GLA_K_WIDTH, GLA_V_WIDTH,
                      GLA_V_WIDTH, 2 * GLA_GATE_RANK))
    seg = lambda i: w[:, offs[i]:offs[i + 1]]
    small = jnp.concatenate([seg(2), seg(3), seg(8)], axis=1)
    small = jnp.pad(small, ((0, 0), (0, SMALL_W - small.shape[1])))
    return jnp.concatenate([seg(0), seg(1), seg(4), seg(5), seg(6), seg(7), small], axis=1).astype(BF16)


def _pack_gate(w_gate2, b_gate):
    ws = []
    for d in range(2):
        lo = GLR_LANE + d * GLA_GATE_RANK
        ws.append(jnp.zeros((SMALL_W, GLA_K_WIDTH), F32).at[lo:lo + GLA_GATE_RANK].set(w_gate2[d]))
    return ws, [b_gate[d].reshape(1, GLA_K_WIDTH) for d in range(2)]


def _pack_dn_params(a_log, dt_bias):
    neg_a = jnp.zeros((SMALL_W,), F32).at[GDEC_LANE:GLR_LANE].set(-jnp.exp(a_log.reshape(-1)))
    dtb = jnp.zeros((SMALL_W,), F32).at[GDEC_LANE:GLR_LANE].set(dt_bias.reshape(-1))
    return jnp.zeros((SUBLANES, SMALL_W), F32).at[0].set(neg_a).at[1].set(dtb)


def _pad_rows(a, rows):
    return jnp.pad(a, ((0, rows - a.shape[0]),) + ((0, 0),) * (a.ndim - 1))


def _mixers(x_seq, mods, mod_row, gain_mix, w_in_p, cw, ap, wgs, bgs, states, ncol, tm, tb):
    b_, t_, _ = x_seq.shape
    qkv, dnz, gqk, gv, gz, sm = _proj(x_seq, mods, mod_row, gain_mix, w_in_p, tm)
    q, k, v, p = _prep(qkv, sm, cw, ap, tm)
    rows = t_ // ncol
    col_view = lambda a: a.reshape(b_, rows, ncol * a.shape[2])
    outs, new_states = [], []
    for d in range(2):
        o_dn, s_dn = _dn_scan(q, k, v, p, states[0][d], bool(d), tb)
        o_gl, s_gl = _gla_scan(col_view(gqk), col_view(gv), col_view(p), wgs[d], bgs[d], states[1][d], bool(d), ncol,
                               min(rows, 2 * CHUNK))
        outs.append((o_dn, o_gl.reshape(b_, t_, GLA_V_WIDTH)))
        new_states.append((s_dn, s_gl))
    states_out = ([new_states[0][0], new_states[1][0]], [new_states[0][1], new_states[1][1]])
    return outs, dnz, gz, states_out


def kernel(x, c, ctx, c_ctx, w_ada, b_ada, norm_mix, norm_ffn, w_in, conv_qkv, dn_a_log, dn_dt_bias, dn_norm,
           gla_w_gate2, gla_b_gate, gla_norm, w_out, ffn_w_gu, ffn_w_down, moe_router, moe_w_gu, moe_w_down,
           final_norm):
    b_, seq, _ = x.shape
    depth = w_ada.shape[0]
    ctx_len = ctx.shape[1]
    mod_rows = -(-(b_ + 1) // SUBLANES) * SUBLANES
    c_rows = _pad_rows(jnp.concatenate([c, c_ctx[None, :]], axis=0), mod_rows)
    lat_row = lambda b: b
    ctx_row = lambda b: b_
    tm_lat, tm_ctx = 512, ctx_len
    tb = 4 * CHUNK

    for i in range(depth):
        last = i == depth - 1
        mods = _ada(c_rows, w_ada[i], b_ada[i]).reshape(mod_rows, N_MOD, D_MODEL)
        mods = jnp.pad(mods, ((0, 0), (0, SUBLANES - N_MOD), (0, 0)))
        w_in_p = _pack_w_in(w_in[i])
        cw = _pad_rows(conv_qkv[i], SUBLANES)
        ap = _pack_dn_params(dn_a_log[i], dn_dt_bias[i])
        wgs, bgs = _pack_gate(gla_w_gate2[i], gla_b_gate[i])
        s0 = ([jnp.zeros((b_, DN_HEADS, DN_HEAD_DIM, DN_HEAD_DIM), F32)] * 2,
              [jnp.zeros((b_, GLA_HEADS, GLA_DV, GLA_DK), F32)] * 2)
        outs_c, dnz_c, gz_c, s_ctx = _mixers(ctx, mods, ctx_row, norm_mix[i], w_in_p, cw, ap, wgs, bgs, s0, 1,
                                             tm_ctx, tb)
        outs_l, dnz_l, gz_l, _ = _mixers(x, mods, lat_row, norm_mix[i], w_in_p, cw, ap, wgs, bgs, s_ctx, GRID_W,
                                         tm_lat, tb)
        w_out_b = w_out[i].astype(BF16)
        is_moe = i % 2 == 1
        if is_moe:
            router = jnp.pad(moe_router[i // 2], ((0, 0), (0, LANES - N_EXPERTS)))
            w_gu = moe_w_gu[i // 2].astype(BF16)
            w_dn = moe_w_down[i // 2].astype(BF16)
        else:
            router = None
            w_gu = ffn_w_gu[i // 2].astype(BF16)
            w_dn = ffn_w_down[i // 2].astype(BF16)

        def channel(tokens, outs, dnz, gz, row, tm, final_gain):
            res = _mixout(tokens, outs[0][0], outs[1][0], dnz, outs[0][1], outs[1][1], gz, mods, row, dn_norm[i],
                          gla_norm[i], w_out_b, norm_ffn[i], router, tm)
            if is_moe:
                x_mid, h, cmb = res
                tm_moe = min(1024, tokens.shape[1])
                return _moe(x_mid, h, cmb, mods, row, w_gu, w_dn, final_gain, tm_moe, D_EXPERT_TILE, D_EXPERT_TILE)
            x_mid, h = res
            y = _ffn(x_mid, h, mods, row, w_gu, w_dn, tm)
            if final_gain is not None:
                raise NotImplementedError("final norm is fused into the expert mixer only")
            return y

        x = channel(x, outs_l, dnz_l, gz_l, lat_row, tm_lat, final_norm if last else None)
        if not last:
            ctx = channel(ctx, outs_c, dnz_c, gz_c, ctx_row, tm_ctx, None)
    return x
```

```python
import functools

import jax
import jax.numpy as jnp
import numpy as np
from jax import lax
from jax.experimental import pallas as pl
from jax.experimental.pallas import tpu as pltpu

F32 = jnp.float32
BF16 = jnp.bfloat16
HIGHEST = lax.Precision.HIGHEST

D_MODEL = 1024
GRID_W = 64
DN_HEADS = 4
DN_HEAD_DIM = 128
DN_WIDTH = DN_HEADS * DN_HEAD_DIM
CONV_W = 5
GLA_HEADS = 4
GLA_DK = 64
GLA_DV = 128
GLA_K_WIDTH = GLA_HEADS * GLA_DK
GLA_V_WIDTH = GLA_HEADS * GLA_DV
GLA_GATE_RANK = 16
GLA_GATE_NORMALIZER = 16.0
N_EXPERTS = 8
N_MOD = 6
NORM_EPS = 1e-6

LANES = 128
SUBLANES = 8
CHUNK = 64
SUB = 16
EXP_CLAMP = 80.0
D_EXPERT_TILE = 896
MOE_TILE = 128
GLA_COLS_PER_STEP = 8
VMEM_LIMIT = 56 * 1024 * 1024

QKV_W = 3 * DN_WIDTH
SMALL_W = LANES
PROJ_COLS = QKV_W + DN_WIDTH + 2 * GLA_K_WIDTH + 2 * GLA_V_WIDTH + SMALL_W
BETA_LANE = 0
GDEC_LANE = 2 * DN_HEADS
GLR_LANE = 4 * DN_HEADS


def _cparams(sem, vmem=VMEM_LIMIT):
    return pltpu.CompilerParams(dimension_semantics=sem, vmem_limit_bytes=vmem)


def _sigmoid(t):
    return 1.0 / (1.0 + jnp.exp(-t))


def _silu(t):
    return t * _sigmoid(t)


def _softplus(t):
    return jnp.maximum(t, 0.0) + jnp.log(1.0 + jnp.exp(-jnp.abs(t)))


def _bdot(a, b):
    return jnp.dot(a.astype(BF16), b.astype(BF16), preferred_element_type=F32)


def _bdot_nt(a, b):
    return lax.dot_general(a.astype(BF16), b.astype(BF16), (((1,), (1,)), ((), ())), preferred_element_type=F32)


def _bdot_tn(a, b):
    return lax.dot_general(a.astype(BF16), b.astype(BF16), (((0,), (0,)), ((), ())), preferred_element_type=F32)


def _rms(t, gain):
    return t * lax.rsqrt(jnp.mean(t * t, axis=-1, keepdims=True) + NORM_EPS) * gain


def _ada_kernel(c_ref, w_ref, b_ref, o_ref):
    s = _silu(c_ref[...])
    o_ref[...] = jnp.dot(s, w_ref[...], precision=HIGHEST, preferred_element_type=F32) + b_ref[...]


def _ada(c_rows, w, b):
    rows = c_rows.shape[0]
    n = w.shape[1]
    tn = D_MODEL
    return pl.pallas_call(
        _ada_kernel,
        grid=(n // tn,),
        in_specs=[pl.BlockSpec((rows, D_MODEL), lambda j: (0, 0)),
                  pl.BlockSpec((D_MODEL, tn), lambda j: (0, j)),
                  pl.BlockSpec((1, tn), lambda j: (0, j))],
        out_specs=pl.BlockSpec((rows, tn), lambda j: (0, j)),
        out_shape=jax.ShapeDtypeStruct((rows, n), F32),
        compiler_params=_cparams(("arbitrary",)),
        name="ada",
    )(c_rows, w, b.reshape(1, n))


def _proj_kernel(x_ref, mod_ref, gain_ref, w_ref, qkv_ref, dnz_ref, gqk_ref, gv_ref, gz_ref, sm_ref):
    x = x_ref[0]
    y = _rms(x, gain_ref[...])
    h = (y * (1.0 + mod_ref[0, 1:2, :]) + mod_ref[0, 0:1, :]).astype(BF16)
    r = jnp.dot(h, w_ref[...], preferred_element_type=F32)
    o = 0
    for ref in (qkv_ref, dnz_ref, gqk_ref, gv_ref, gz_ref, sm_ref):
        wd = ref.shape[2]
        ref[0] = r[:, o:o + wd]
        o += wd


def _proj(x, mods, mod_row, gain, w, tm):
    b_, t_, _ = x.shape
    widths = (QKV_W, DN_WIDTH, 2 * GLA_K_WIDTH, GLA_V_WIDTH, GLA_V_WIDTH, SMALL_W)
    tok = lambda wd: pl.BlockSpec((1, tm, wd), lambda b, j: (b, j, 0))
    return pl.pallas_call(
        _proj_kernel,
        grid=(b_, t_ // tm),
        in_specs=[tok(D_MODEL),
                  pl.BlockSpec((1, SUBLANES, D_MODEL), lambda b, j: (mod_row(b), 0, 0)),
                  pl.BlockSpec((1, D_MODEL), lambda b, j: (0, 0)),
                  pl.BlockSpec((D_MODEL, PROJ_COLS), lambda b, j: (0, 0))],
        out_specs=[tok(wd) for wd in widths],
        out_shape=[jax.ShapeDtypeStruct((b_, t_, wd), F32) for wd in widths],
        compiler_params=_cparams(("parallel", "parallel")),
        name="proj",
    )(x, mods, gain.reshape(1, D_MODEL), w)


def _prep_kernel(cur_ref, prev_ref, next_ref, sm_ref, cw_ref, ap_ref, q_ref, k_ref, v_ref, p_ref, ext_ref):
    j = pl.program_id(1)
    nj = pl.num_programs(1)
    tl = cur_ref.shape[1]
    halo = SUBLANES
    pad = CONV_W // 2
    ext_ref[0:halo, :] = jnp.where(j > 0, prev_ref[0], 0.0)
    ext_ref[halo:halo + tl, :] = cur_ref[0]
    ext_ref[halo + tl:2 * halo + tl, :] = jnp.where(j < nj - 1, next_ref[0], 0.0)
    acc = None
    for w in range(CONV_W):
        term = ext_ref[pl.ds(halo - pad + w, tl), :] * cw_ref[w:w + 1, :]
        acc = term if acc is None else acc + term
    s = _silu(acc)
    for h in range(DN_HEADS):
        lo = h * DN_HEAD_DIM
        qh = s[:, lo:lo + DN_HEAD_DIM]
        kh = s[:, DN_WIDTH + lo:DN_WIDTH + lo + DN_HEAD_DIM]
        q_ref[0, :, lo:lo + DN_HEAD_DIM] = (qh * lax.rsqrt(jnp.sum(qh * qh, axis=-1, keepdims=True) + NORM_EPS)
                                            * DN_HEAD_DIM ** -0.5)
        k_ref[0, :, lo:lo + DN_HEAD_DIM] = kh * lax.rsqrt(jnp.sum(kh * kh, axis=-1, keepdims=True) + NORM_EPS)
    v_ref[0] = s[:, 2 * DN_WIDTH:]
    sm = sm_ref[0]
    lane = lax.broadcasted_iota(jnp.int32, sm.shape, 1)
    beta = _sigmoid(sm)
    gdec = ap_ref[0:1, :] * _softplus(sm + ap_ref[1:2, :])
    p_ref[0] = jnp.where(lane < GDEC_LANE, beta, jnp.where(lane < GLR_LANE, gdec, sm))


def _prep(qkv, sm, cw, ap, tl):
    b_, t_, _ = qkv.shape
    nb = t_ // tl
    r8 = tl // SUBLANES
    last8 = t_ // SUBLANES - 1
    tok = lambda wd: pl.BlockSpec((1, tl, wd), lambda b, j: (b, j, 0))
    return pl.pallas_call(
        _prep_kernel,
        grid=(b_, nb),
        in_specs=[tok(QKV_W),
                  pl.BlockSpec((1, SUBLANES, QKV_W), lambda b, j: (b, jnp.maximum(j * r8 - 1, 0), 0)),
                  pl.BlockSpec((1, SUBLANES, QKV_W), lambda b, j: (b, jnp.minimum((j + 1) * r8, last8), 0)),
                  tok(SMALL_W),
                  pl.BlockSpec((SUBLANES, QKV_W), lambda b, j: (0, 0)),
                  pl.BlockSpec((SUBLANES, SMALL_W), lambda b, j: (0, 0))],
        out_specs=[tok(DN_WIDTH), tok(DN_WIDTH), tok(DN_WIDTH), tok(SMALL_W)],
        out_shape=[jax.ShapeDtypeStruct((b_, t_, DN_WIDTH), F32)] * 3 + [jax.ShapeDtypeStruct((b_, t_, SMALL_W), F32)],
        scratch_shapes=[pltpu.VMEM((tl + 2 * SUBLANES, QKV_W), F32)],
        compiler_params=_cparams(("parallel", "parallel")),
        name="prep",
    )(qkv, qkv, qkv, sm, cw, ap)


def _order_masks(n, rev):
    ri = lax.broadcasted_iota(jnp.int32, (n, n), 0)
    ci = lax.broadcasted_iota(jnp.int32, (n, n), 1)
    if rev:
        return ri, ci, ci >= ri, ci > ri
    return ri, ci, ci <= ri, ci < ri


def _dn_kernel(q_ref, k_ref, v_ref, p_ref, s0_ref, o_ref, st_ref, s_scr, *, rev):
    j = pl.program_id(1)
    nj = pl.num_programs(1)

    @pl.when(j == 0)
    def _():
        s_scr[...] = s0_ref[0]

    c_ = CHUNK
    tb = q_ref.shape[1]
    nchunk = tb // c_
    d = 1 if rev else 0
    ri, ci, incl, strict = _order_masks(c_, rev)
    eye = (ri == ci).astype(F32)
    rb, cb, incl_b, _ = _order_masks(tb, rev)
    cum_mat = (incl_b & ((rb // c_) == (cb // c_))).astype(F32)
    gcum_all = jnp.dot(cum_mat, p_ref[0], precision=HIGHEST, preferred_element_type=F32)
    gcum_all_t = gcum_all.T
    level_masks = []
    s = 1
    while s < c_:
        level_masks.append(strict & ((ri // (2 * s)) == (ci // (2 * s))) & ((ri // s) != (ci // s)))
        s *= 2
    last = 0 if rev else c_ - 1

    order = [nchunk - 1 - cc if rev else cc for cc in range(nchunk)]
    inst = [(c, h) for c in order for h in range(DN_HEADS)]
    pre = {}
    for c, h in inst:
        r0 = c * c_
        lo = h * DN_HEAD_DIM
        lb = BETA_LANE + d * DN_HEADS + h
        lg = GDEC_LANE + d * DN_HEADS + h
        beta = p_ref[0, r0:r0 + c_, lb:lb + 1]
        gc = gcum_all[r0:r0 + c_, lg:lg + 1]
        gr = gcum_all_t[lg:lg + 1, r0:r0 + c_]
        glast = gcum_all[r0 + last:r0 + last + 1, lg:lg + 1]
        decay = jnp.exp(jnp.where(incl, gc - gr, -jnp.inf))
        qh = q_ref[0, r0:r0 + c_, lo:lo + DN_HEAD_DIM]
        kh = k_ref[0, r0:r0 + c_, lo:lo + DN_HEAD_DIM]
        vh = v_ref[0, r0:r0 + c_, lo:lo + DN_HEAD_DIM]
        kb = kh * beta
        eg = jnp.exp(gc)
        pre[c, h] = dict(decay=decay, kq_lhs=jnp.concatenate([kb, qh], axis=0), kh=kh,
                         rhs=jnp.concatenate([vh * beta, kb * eg], axis=1), q_dec=qh * eg,
                         k_dec=kh * jnp.exp(glast - gc), e_last=jnp.exp(glast))
    kq = {i: _bdot_nt(pre[i]["kq_lhs"], pre[i]["kh"]) for i in inst}
    a = {i: jnp.where(strict, kq[i][:c_] * pre[i]["decay"], 0.0) for i in inst}
    attn = {i: kq[i][c_:] * pre[i]["decay"] for i in inst}
    t = {i: eye - jnp.where(level_masks[0], a[i], 0.0) for i in inst}
    for m in level_masks[1:]:
        ta = {i: _bdot(t[i], jnp.where(m, a[i], 0.0)) for i in inst}
        t = {i: t[i] - _bdot(ta[i], t[i]) for i in inst}
    uw = {i: _bdot(t[i], pre[i]["rhs"]) for i in inst}
    for c in order:
        r0 = c * c_
        heads = [(c, h) for h in range(DN_HEADS)]
        st = {i: s_scr[i[1]] for i in heads}
        ws = {i: _bdot(jnp.concatenate([uw[i][:, DN_HEAD_DIM:], pre[i]["q_dec"]], axis=0), st[i]) for i in heads}
        v_new = {i: uw[i][:, :DN_HEAD_DIM] - ws[i][:c_] for i in heads}
        for i in heads:
            lo = i[1] * DN_HEAD_DIM
            o_ref[0, r0:r0 + c_, lo:lo + DN_HEAD_DIM] = ws[i][c_:] + _bdot(attn[i], v_new[i])
        for i in heads:
            s_scr[i[1]] = st[i] * pre[i]["e_last"] + _bdot_tn(pre[i]["k_dec"], v_new[i])

    @pl.when(j == nj - 1)
    def _():
        st_ref[0] = s_scr[...]


def _dn_scan(q, k, v, p, s0, rev, tb):
    b_, t_, _ = q.shape
    nb = t_ // tb
    blk = (lambda b, j: (b, nb - 1 - j, 0)) if rev else (lambda b, j: (b, j, 0))
    tok = lambda wd: pl.BlockSpec((1, tb, wd), blk)
    st_spec = pl.BlockSpec((1, DN_HEADS, DN_HEAD_DIM, DN_HEAD_DIM), lambda b, j: (b, 0, 0, 0))
    return pl.pallas_call(
        functools.partial(_dn_kernel, rev=rev),
        grid=(b_, nb),
        in_specs=[tok(DN_WIDTH), tok(DN_WIDTH), tok(DN_WIDTH), tok(SMALL_W), st_spec],
        out_specs=[tok(DN_WIDTH), st_spec],
        out_shape=[jax.ShapeDtypeStruct((b_, t_, DN_WIDTH), F32),
                   jax.ShapeDtypeStruct((b_, DN_HEADS, DN_HEAD_DIM, DN_HEAD_DIM), F32)],
        scratch_shapes=[pltpu.VMEM((DN_HEADS, DN_HEAD_DIM, DN_HEAD_DIM), F32)],
        compiler_params=_cparams(("parallel", "arbitrary")),
        name="dn_rev" if rev else "dn_fwd",
    )(q, k, v, p, s0)


def _log_sigmoid(t):
    return jnp.minimum(t, 0.0) - jnp.log(1.0 + jnp.exp(-jnp.abs(t)))


def _gla_kernel(qk_ref, v_ref, p_ref, wg_ref, bg_ref, s0_ref, o_ref, st_ref, s_scr, *, rev):
    col = pl.program_id(1)
    rb = pl.program_id(2)
    ncol = pl.num_programs(1)
    nrb = pl.num_programs(2)

    @pl.when((col == 0) & (rb == 0))
    def _():
        s_scr[...] = s0_ref[0]

    c_ = CHUNK
    nchunk = qk_ref.shape[1] // c_
    ncb = p_ref.shape[2] // SMALL_W
    nsub = c_ // SUB
    ri, ci, incl, _ = _order_masks(c_, rev)
    incl_f = incl.astype(F32)
    pos_r = (c_ - 1 - ri) if rev else ri
    pos_c = (c_ - 1 - ci) if rev else ci
    mask_diag = incl & ((pos_r // SUB) == (pos_c // SUB))
    mask_off = (pos_c // SUB) < (pos_r // SUB)
    last = 0 if rev else c_ - 1

    cols = [ncb - 1 - i if rev else i for i in range(ncb)]
    chunks = [nchunk - 1 - cc if rev else cc for cc in range(nchunk)]
    order = [(cb, c) for cb in cols for c in chunks]
    la_col = {}
    for cb in cols:
        gate = jnp.dot(p_ref[0, :, cb * SMALL_W:(cb + 1) * SMALL_W], wg_ref[...], precision=HIGHEST,
                       preferred_element_type=F32) + bg_ref[...]
        la_col[cb] = _log_sigmoid(gate) * (1.0 / GLA_GATE_NORMALIZER)
    pre = {}
    for cb, c in order:
        r0 = c * c_
        la = la_col[cb][r0:r0 + c_, :]
        bc = jnp.dot(incl_f, la, precision=HIGHEST, preferred_element_type=F32)
        bex = bc - la
        b_last = bc[last:last + 1, :]
        first_rows = [(c_ - 1 - SUB * i) if rev else SUB * i for i in range(nsub)]
        pieces = [jnp.broadcast_to(bex[fr:fr + 1, :], (SUB, GLA_K_WIDTH)) for fr in first_rows]
        refd = jnp.concatenate(pieces[::-1] if rev else pieces, axis=0)
        mid = first_rows[nsub // 2]
        rmid = bex[mid:mid + 1, :]
        qk0 = cb * 2 * GLA_K_WIDTH
        qa = qk_ref[0, r0:r0 + c_, qk0:qk0 + GLA_K_WIDTH] * GLA_DK ** -0.5
        ka = qk_ref[0, r0:r0 + c_, qk0 + GLA_K_WIDTH:qk0 + 2 * GLA_K_WIDTH]
        q_in = qa * jnp.exp(bc)
        k_st = ka * jnp.exp(b_last - bc)
        q_d = qa * jnp.exp(bc - refd)
        k_d = ka * jnp.exp(jnp.minimum(refd - bc, EXP_CLAMP))
        q_o = qa * jnp.exp(jnp.minimum(bc - rmid, EXP_CLAMP))
        k_o = ka * jnp.exp(jnp.minimum(rmid - bc, EXP_CLAMP))
        e_last = jnp.exp(b_last)
        for h in range(GLA_HEADS):
            ks = slice(h * GLA_DK, (h + 1) * GLA_DK)
            attn = (jnp.where(mask_diag, _bdot_nt(q_d[:, ks], k_d[:, ks]), 0.0)
                    + jnp.where(mask_off, _bdot_nt(q_o[:, ks], k_o[:, ks]), 0.0))
            pre[cb, c, h] = dict(attn=attn, q_in=q_in[:, ks], k_st=k_st[:, ks], e_last=e_last[:, ks])
    st = [s_scr[h] for h in range(GLA_HEADS)]
    for cb, c in order:
        r0 = c * c_
        v0 = cb * GLA_V_WIDTH
        vh = [v_ref[0, r0:r0 + c_, v0 + h * GLA_DV:v0 + (h + 1) * GLA_DV] for h in range(GLA_HEADS)]
        for h in range(GLA_HEADS):
            o_ref[0, r0:r0 + c_, v0 + h * GLA_DV:v0 + (h + 1) * GLA_DV] = (
                _bdot_nt(pre[cb, c, h]["q_in"], st[h]) + _bdot(pre[cb, c, h]["attn"], vh[h]))
        st = [st[h] * pre[cb, c, h]["e_last"] + _bdot_tn(vh[h], pre[cb, c, h]["k_st"]) for h in range(GLA_HEADS)]
    for h in range(GLA_HEADS):
        s_scr[h] = st[h]

    @pl.when((col == ncol - 1) & (rb == nrb - 1))
    def _():
        st_ref[0] = s_scr[...]


def _gla_scan(qk, v, p, wg, bg, s0, rev, ncol, tr):
    b_, rows, _ = qk.shape
    nrb = rows // tr
    ncb = min(GLA_COLS_PER_STEP, ncol)
    ncblk = ncol // ncb
    assert ncb == 1 or nrb == 1
    if rev:
        blk = lambda b, c, r: (b, nrb - 1 - r, ncblk - 1 - c)
    else:
        blk = lambda b, c, r: (b, r, c)
    tok = lambda wd: pl.BlockSpec((1, tr, ncb * wd), blk)
    st_spec = pl.BlockSpec((1, GLA_HEADS, GLA_DV, GLA_DK), lambda b, c, r: (b, 0, 0, 0))
    return pl.pallas_call(
        functools.partial(_gla_kernel, rev=rev),
        grid=(b_, ncblk, nrb),
        in_specs=[tok(2 * GLA_K_WIDTH), tok(GLA_V_WIDTH), tok(SMALL_W),
                  pl.BlockSpec((SMALL_W, GLA_K_WIDTH), lambda b, c, r: (0, 0)),
                  pl.BlockSpec((1, GLA_K_WIDTH), lambda b, c, r: (0, 0)),
                  st_spec],
        out_specs=[tok(GLA_V_WIDTH), st_spec],
        out_shape=[jax.ShapeDtypeStruct((b_, rows, ncol * GLA_V_WIDTH), F32),
                   jax.ShapeDtypeStruct((b_, GLA_HEADS, GLA_DV, GLA_DK), F32)],
        scratch_shapes=[pltpu.VMEM((GLA_HEADS, GLA_DV, GLA_DK), F32)],
        compiler_params=_cparams(("parallel", "arbitrary", "arbitrary")),
        name="gla_rev" if rev else "gla_fwd",
    )(qk, v, p, wg, bg, s0)


def _head_norm_gate(o, z, gain):
    parts = []
    for h in range(o.shape[1] // LANES):
        hs = slice(h * LANES, (h + 1) * LANES)
        parts.append(_rms(o[:, hs], gain) * _silu(z[:, hs]))
    return jnp.concatenate(parts, axis=1)


def _mixout_kernel(*refs, with_router):
    (x_ref, odf_ref, odb_ref, dnz_ref, ogf_ref, ogb_ref, glz_ref, mod_ref, gd_ref, gg_ref, w_ref, gf_ref) = refs[:12]
    if with_router:
        rw_ref, xo_ref, h_ref, cmb_ref = refs[12:]
    else:
        xo_ref, h_ref = refs[12:]
    m_dn = _head_norm_gate(odf_ref[0] + odb_ref[0], dnz_ref[0], gd_ref[...])
    m_gl = _head_norm_gate(ogf_ref[0] + ogb_ref[0], glz_ref[0], gg_ref[...])
    m = jnp.concatenate([m_dn, m_gl], axis=1).astype(BF16)
    x = x_ref[0] + mod_ref[0, 2:3, :] * jnp.dot(m, w_ref[...], preferred_element_type=F32)
    xo_ref[0] = x
    h = _rms(x, gf_ref[...]) * (1.0 + mod_ref[0, 4:5, :]) + mod_ref[0, 3:4, :]
    h_ref[0] = h.astype(BF16)
    if with_router:
        logits = jnp.dot(h, rw_ref[...], precision=HIGHEST, preferred_element_type=F32)
        lane = lax.broadcasted_iota(jnp.int32, logits.shape, 1).astype(F32)
        logits = jnp.where(lane < N_EXPERTS, logits, -jnp.inf)
        e = jnp.exp(logits - jnp.max(logits, axis=-1, keepdims=True))
        probs = e / jnp.sum(e, axis=-1, keepdims=True)
        p1 = jnp.max(probs, axis=-1, keepdims=True)
        i1 = jnp.min(jnp.where(probs == p1, lane, float(LANES)), axis=-1, keepdims=True)
        rest = jnp.where(lane == i1, -1.0, probs)
        p2 = jnp.max(rest, axis=-1, keepdims=True)
        i2 = jnp.min(jnp.where(rest == p2, lane, float(LANES)), axis=-1, keepdims=True)
        cmb_ref[0] = (jnp.where(lane == i1, p1, 0.0) + jnp.where(lane == i2, p2, 0.0)) / (p1 + p2)


def _mixout(x, odf, odb, dnz, ogf, ogb, glz, mods, mod_row, gd, gg, w_out, gf, router, tm):
    b_, t_, _ = x.shape
    with_router = router is not None
    tok = lambda wd: pl.BlockSpec((1, tm, wd), lambda b, j: (b, j, 0))
    const = lambda shp: pl.BlockSpec(shp, lambda b, j: (0,) * len(shp))
    in_specs = [tok(D_MODEL)] + [tok(DN_WIDTH)] * 6 + [
        pl.BlockSpec((1, SUBLANES, D_MODEL), lambda b, j: (mod_row(b), 0, 0)),
        const((1, LANES)), const((1, LANES)), const((D_MODEL, D_MODEL)), const((1, D_MODEL))]
    args = [x, odf, odb, dnz, ogf, ogb, glz, mods, gd.reshape(1, LANES), gg.reshape(1, LANES), w_out,
            gf.reshape(1, D_MODEL)]
    out_specs = [tok(D_MODEL), tok(D_MODEL)]
    out_shape = [jax.ShapeDtypeStruct((b_, t_, D_MODEL), F32), jax.ShapeDtypeStruct((b_, t_, D_MODEL), BF16)]
    if with_router:
        in_specs.append(const((D_MODEL, LANES)))
        args.append(router)
        out_specs.append(tok(LANES))
        out_shape.append(jax.ShapeDtypeStruct((b_, t_, LANES), F32))
    return pl.pallas_call(
        functools.partial(_mixout_kernel, with_router=with_router),
        grid=(b_, t_ // tm),
        in_specs=in_specs,
        out_specs=out_specs,
        out_shape=out_shape,
        compiler_params=_cparams(("parallel", "parallel")),
        name="mixout",
    )(*args)


def _ffn_kernel(x_ref, h_ref, mod_ref, wg_ref, wu_ref, wd_ref, o_ref, *, nsplit):
    h = h_ref[0]
    f = wg_ref.shape[1]
    fs = f // nsplit
    acc = None
    for s in range(nsplit):
        g = jnp.dot(h, wg_ref[:, s * fs:(s + 1) * fs], preferred_element_type=F32)
        u = jnp.dot(h, wu_ref[:, s * fs:(s + 1) * fs], preferred_element_type=F32)
        part = jnp.dot((_silu(g) * u).astype(BF16), wd_ref[s * fs:(s + 1) * fs, :], preferred_element_type=F32)
        acc = part if acc is None else acc + part
    o_ref[0] = x_ref[0] + mod_ref[0, 5:6, :] * acc


def _ffn(x, h, mods, mod_row, w_gu, w_down, tm):
    b_, t_, _ = x.shape
    f = w_down.shape[0]
    tok = lambda: pl.BlockSpec((1, tm, D_MODEL), lambda b, j: (b, j, 0))
    return pl.pallas_call(
        functools.partial(_ffn_kernel, nsplit=2),
        grid=(b_, t_ // tm),
        in_specs=[tok(), tok(),
                  pl.BlockSpec((1, SUBLANES, D_MODEL), lambda b, j: (mod_row(b), 0, 0)),
                  pl.BlockSpec((D_MODEL, f), lambda b, j: (0, 0)),
                  pl.BlockSpec((D_MODEL, f), lambda b, j: (0, 1)),
                  pl.BlockSpec((f, D_MODEL), lambda b, j: (0, 0))],
        out_specs=tok(),
        out_shape=jax.ShapeDtypeStruct((b_, t_, D_MODEL), F32),
        compiler_params=_cparams(("parallel", "parallel")),
        name="ffn",
    )(x, h, mods, w_gu, w_gu, w_down)


def _for_row_tiles(ntiles, fn):
    pair = 2 * MOE_TILE
    npair = ntiles // 2

    def body(i, carry):
        fn(pl.multiple_of(i * pair, pair), pair)
        return carry

    lax.fori_loop(0, npair, body, 0)

    @pl.when(ntiles % 2 == 1)
    def _():
        fn(pl.multiple_of(npair * pair, pair), MOE_TILE)


def _moe_kernel(cnt_ref, x_ref, h_ref, cmb_ref, mod_ref, wg_ref, wu_ref, wd_ref, fg_ref, o_ref,
                acc_ref, xe_ref, ye_ref, rkc_ref, rkr_ref, *, final_norm):
    b = pl.program_id(0)
    j = pl.program_id(1)
    e = pl.program_id(2)
    f = pl.program_id(3)
    nj = pl.num_programs(1)
    ne = pl.num_programs(2)
    nf = pl.num_programs(3)
    tm = h_ref.shape[1]
    ntiles = (cnt_ref[(b * nj + j) * ne + e] + MOE_TILE - 1) // MOE_TILE

    @pl.when((e == 0) & (f == 0))
    def _():
        acc_ref[...] = jnp.zeros_like(acc_ref)
        sb = 2 * LANES
        ri = lax.broadcasted_iota(jnp.int32, (sb, sb), 0)
        ci = lax.broadcasted_iota(jnp.int32, (sb, sb), 1)
        before = (ci < ri).astype(BF16)
        carry = jnp.zeros((1, LANES), F32)
        for s in range(tm // sb):
            sel = (cmb_ref[0, s * sb:(s + 1) * sb, :] > 0.0).astype(F32)
            rank = jnp.dot(before, sel.astype(BF16), preferred_element_type=F32) + carry
            rkc_ref[s * sb:(s + 1) * sb, :] = jnp.where(sel > 0.0, rank, -1.0)
            carry = carry + jnp.sum(sel, axis=0, keepdims=True)
        rkr_ref[...] = rkc_ref[...].T

    @pl.when(f == 0)
    def _():
        rank_row = rkr_ref[pl.ds(e, 1), :]

        def gather(r0, n):
            slot = lax.broadcasted_iota(jnp.int32, (n, tm), 0).astype(F32) + r0.astype(F32)
            onehot = (rank_row == slot).astype(BF16)
            xe_ref[pl.ds(r0, n), :] = jnp.dot(onehot, h_ref[0], preferred_element_type=F32).astype(BF16)

        _for_row_tiles(ntiles, gather)

    def expert(r0, n):
        xr = xe_ref[pl.ds(r0, n), :]
        g = jnp.dot(xr, wg_ref[0], preferred_element_type=F32)
        u = jnp.dot(xr, wu_ref[0], preferred_element_type=F32)
        y = jnp.dot((_silu(g) * u).astype(BF16), wd_ref[0], preferred_element_type=F32)

        @pl.when(f == 0)
        def _():
            ye_ref[pl.ds(r0, n), :] = y

        @pl.when(f > 0)
        def _():
            ye_ref[pl.ds(r0, n), :] += y

    _for_row_tiles(ntiles, expert)

    @pl.when(f == nf - 1)
    def _():
        lane = lax.broadcasted_iota(jnp.int32, (tm, LANES), 1)
        rank_col = jnp.sum(jnp.where(lane == e, rkc_ref[...], 0.0), axis=-1, keepdims=True)
        ce = jnp.sum(jnp.where(lane == e, cmb_ref[0], 0.0), axis=-1, keepdims=True)

        def scatter(r0, n):
            slot = lax.broadcasted_iota(jnp.int32, (tm, n), 1).astype(F32) + r0.astype(F32)
            onehot = (rank_col == slot).astype(BF16)
            back = jnp.dot(onehot, ye_ref[pl.ds(r0, n), :].astype(BF16), preferred_element_type=F32)
            acc_ref[...] += ce * back

        _for_row_tiles(ntiles, scatter)

    @pl.when((e == ne - 1) & (f == nf - 1))
    def _():
        y = x_ref[0] + mod_ref[0, 5:6, :] * acc_ref[...]
        if final_norm:
            y = _rms(y, fg_ref[...])
        o_ref[0] = y


def _moe(x, h, cmb, mods, mod_row, w_gu, w_down, final_gain, tm, tf):
    b_, t_, _ = x.shape
    ne, _, f2 = w_gu.shape
    fe = f2 // 2
    assert fe % tf == 0 and tm % (2 * MOE_TILE) == 0 and ne <= SUBLANES
    nf = fe // tf
    nj = t_ // tm
    counts = jnp.sum((cmb[..., :ne] > 0.0).reshape(b_, nj, tm, ne), axis=2, dtype=jnp.int32).reshape(-1)
    tok = lambda wd: pl.BlockSpec((1, tm, wd), lambda b, j, e, f, cnt: (b, j, 0))
    final_norm = final_gain is not None
    fg = (final_gain if final_norm else jnp.ones((D_MODEL,), F32)).reshape(1, D_MODEL)
    grid_spec = pltpu.PrefetchScalarGridSpec(
        num_scalar_prefetch=1,
        grid=(b_, nj, ne, nf),
        in_specs=[tok(D_MODEL), tok(D_MODEL), tok(LANES),
                  pl.BlockSpec((1, SUBLANES, D_MODEL), lambda b, j, e, f, cnt: (mod_row(b), 0, 0)),
                  pl.BlockSpec((1, D_MODEL, tf), lambda b, j, e, f, cnt: (e, 0, f)),
                  pl.BlockSpec((1, D_MODEL, tf), lambda b, j, e, f, cnt: (e, 0, nf + f)),
                  pl.BlockSpec((1, tf, D_MODEL), lambda b, j, e, f, cnt: (e, f, 0)),
                  pl.BlockSpec((1, D_MODEL), lambda b, j, e, f, cnt: (0, 0))],
        out_specs=tok(D_MODEL),
        scratch_shapes=[pltpu.VMEM((tm, D_MODEL), F32),
                        pltpu.VMEM((tm, D_MODEL), BF16),
                        pltpu.VMEM((tm, D_MODEL), F32),
                        pltpu.VMEM((tm, LANES), F32),
                        pltpu.VMEM((LANES, tm), F32)])
    return pl.pallas_call(
        functools.partial(_moe_kernel, final_norm=final_norm),
        grid_spec=grid_spec,
        out_shape=jax.ShapeDtypeStruct((b_, t_, D_MODEL), F32),
        compiler_params=_cparams(("parallel", "parallel", "arbitrary", "arbitrary")),
        name="moe",
    )(counts, x, h, cmb, mods, w_gu, w_gu, w_down, fg)


def _pack_w_in(w):
    offs = np.cumsum((0, QKV_W, DN_WIDTH, 2 * DN_HEADS, 2 * DN_HEADS, GLA_K_WIDTH, GLA_K_WIDTH, GLA_V_WIDTH,
                      GLA_V_WIDTH, 2 * GLA_GATE_RANK))
    seg = lambda i: w[:, offs[i]:offs[i + 1]]
    small = jnp.concatenate([seg(2), seg(3), seg(8)], axis=1)
    small = jnp.pad(small, ((0, 0), (0, SMALL_W - small.shape[1])))
    return jnp.concatenate([seg(0), seg(1), seg(4), seg(5), seg(6), seg(7), small], axis=1).astype(BF16)


def _pack_gate(w_gate2, b_gate):
    ws = []
    for d in range(2):
        lo = GLR_LANE + d * GLA_GATE_RANK
        ws.append(jnp.zeros((SMALL_W, GLA_K_WIDTH), F32).at[lo:lo + GLA_GATE_RANK].set(w_gate2[d]))
    return ws, [b_gate[d].reshape(1, GLA_K_WIDTH) for d in range(2)]


def _pack_dn_params(a_log, dt_bias):
    neg_a = jnp.zeros((SMALL_W,), F32).at[GDEC_LANE:GLR_LANE].set(-jnp.exp(a_log.reshape(-1)))
    dtb = jnp.zeros((SMALL_W,), F32).at[GDEC_LANE:GLR_LANE].set(dt_bias.reshape(-1))
    return jnp.zeros((SUBLANES, SMALL_W), F32).at[0].set(neg_a).at[1].set(dtb)


def _pad_rows(a, rows):
    return jnp.pad(a, ((0, rows - a.shape[0]),) + ((0, 0),) * (a.ndim - 1))


def _mixers(x_seq, mods, mod_row, gain_mix, w_in_p, cw, ap, wgs, bgs, states, ncol, tm, tb):
    b_, t_, _ = x_seq.shape
    qkv, dnz, gqk, gv, gz, sm = _proj(x_seq, mods, mod_row, gain_mix, w_in_p, tm)
    q, k, v, p = _prep(qkv, sm, cw, ap, tm)
    rows = t_ // ncol
    col_view = lambda a: a.reshape(b_, rows, ncol * a.shape[2])
    outs, new_states = [], []
    for d in range(2):
        o_dn, s_dn = _dn_scan(q, k, v, p, states[0][d], bool(d), tb)
        o_gl, s_gl = _gla_scan(col_view(gqk), col_view(gv), col_view(p), wgs[d], bgs[d], states[1][d], bool(d), ncol,
                               min(rows, 2 * CHUNK))
        outs.append((o_dn, o_gl.reshape(b_, t_, GLA_V_WIDTH)))
        new_states.append((s_dn, s_gl))
    states_out = ([new_states[0][0], new_states[1][0]], [new_states[0][1], new_states[1][1]])
    return outs, dnz, gz, states_out


def kernel(x, c, ctx, c_ctx, w_ada, b_ada, norm_mix, norm_ffn, w_in, conv_qkv, dn_a_log, dn_dt_bias, dn_norm,
           gla_w_gate2, gla_b_gate, gla_norm, w_out, ffn_w_gu, ffn_w_down, moe_router, moe_w_gu, moe_w_down,
           final_norm):
    b_, seq, _ = x.shape
    depth = w_ada.shape[0]
    ctx_len = ctx.shape[1]
    mod_rows = -(-(b_ + 1) // SUBLANES) * SUBLANES
    c_rows = _pad_rows(jnp.concatenate([c, c_ctx[None, :]], axis=0), mod_rows)
    lat_row = lambda b: b
    ctx_row = lambda b: b_
    tm_lat, tm_ctx = 512, ctx_len
    tb = 4 * CHUNK

    for i in range(depth):
        last = i == depth - 1
        mods = _ada(c_rows, w_ada[i], b_ada[i]).reshape(mod_rows, N_MOD, D_MODEL)
        mods = jnp.pad(mods, ((0, 0), (0, SUBLANES - N_MOD), (0, 0)))
        w_in_p = _pack_w_in(w_in[i])
        cw = _pad_rows(conv_qkv[i], SUBLANES)
        ap = _pack_dn_params(dn_a_log[i], dn_dt_bias[i])
        wgs, bgs = _pack_gate(gla_w_gate2[i], gla_b_gate[i])
        s0 = ([jnp.zeros((b_, DN_HEADS, DN_HEAD_DIM, DN_HEAD_DIM), F32)] * 2,
              [jnp.zeros((b_, GLA_HEADS, GLA_DV, GLA_DK), F32)] * 2)
        outs_c, dnz_c, gz_c, s_ctx = _mixers(ctx, mods, ctx_row, norm_mix[i], w_in_p, cw, ap, wgs, bgs, s0, 1,
                                             tm_ctx, tb)
        outs_l, dnz_l, gz_l, _ = _mixers(x, mods, lat_row, norm_mix[i], w_in_p, cw, ap, wgs, bgs, s_ctx, GRID_W,
                                         tm_lat, tb)
        w_out_b = w_out[i].astype(BF16)
        is_moe = i % 2 == 1
        if is_moe:
            router = jnp.pad(moe_router[i // 2], ((0, 0), (0, LANES - N_EXPERTS)))
            w_gu = moe_w_gu[i // 2].astype(BF16)
            w_dn = moe_w_down[i // 2].astype(BF16)
        else:
            router = None
            w_gu = ffn_w_gu[i // 2].astype(BF16)
            w_dn = ffn_w_down[i // 2].astype(BF16)

        def channel(tokens, outs, dnz, gz, row, tm, final_gain):
            res = _mixout(tokens, outs[0][0], outs[1][0], dnz, outs[0][1], outs[1][1], gz, mods, row, dn_norm[i],
                          gla_norm[i], w_out_b, norm_ffn[i], router, tm)
            if is_moe:
                x_mid, h, cmb = res
                tm_moe = min(1024, tokens.shape[1])
                return _moe(x_mid, h, cmb, mods, row, w_gu, w_dn, final_gain, tm_moe, D_EXPERT_TILE)
            x_mid, h = res
            y = _ffn(x_mid, h, mods, row, w_gu, w_dn, tm)
            if final_gain is not None:
                raise NotImplementedError("final norm is fused into the expert mixer only")
            return y

        x = channel(x, outs_l, dnz_l, gz_l, lat_row, tm_lat, final_norm if last else None)
        if not last:
            ctx = channel(ctx, outs_c, dnz_c, gz_c, ctx_row, tm_ctx, None)
    return x
```

```python
import functools

import jax
import jax.numpy as jnp
import numpy as np
from jax import lax
from jax.experimental import pallas as pl
from jax.experimental.pallas import tpu as pltpu

F32 = jnp.float32
BF16 = jnp.bfloat16
HIGHEST = lax.Precision.HIGHEST

D_MODEL = 1024
GRID_W = 64
DN_HEADS = 4
DN_HEAD_DIM = 128
DN_WIDTH = DN_HEADS * DN_HEAD_DIM
CONV_W = 5
GLA_HEADS = 4
GLA_DK = 64
GLA_DV = 128
GLA_K_WIDTH = GLA_HEADS * GLA_DK
GLA_V_WIDTH = GLA_HEADS * GLA_DV
GLA_GATE_RANK = 16
GLA_GATE_NORMALIZER = 16.0
N_EXPERTS = 8
N_MOD = 6
NORM_EPS = 1e-6

LANES = 128
SUBLANES = 8
CHUNK = 64
SUB = 16
EXP_CLAMP = 80.0
D_EXPERT_TILE = 896
MOE_TILE = 128
GLA_COLS_PER_STEP = 8
VMEM_LIMIT = 56 * 1024 * 1024

QKV_W = 3 * DN_WIDTH
SMALL_W = LANES
PROJ_WIDTHS = (QKV_W, DN_WIDTH, 2 * GLA_K_WIDTH, GLA_V_WIDTH, GLA_V_WIDTH, SMALL_W)
PROJ_COLS = sum(PROJ_WIDTHS)
BETA_LANE = 0
GDEC_LANE = 2 * DN_HEADS
GLR_LANE = 4 * DN_HEADS


def _cparams(sem, vmem=VMEM_LIMIT):
    return pltpu.CompilerParams(dimension_semantics=sem, vmem_limit_bytes=vmem)


def _sigmoid(t):
    return 1.0 / (1.0 + jnp.exp(-t))


def _silu(t):
    return t * _sigmoid(t)


def _softplus(t):
    return jnp.maximum(t, 0.0) + jnp.log(1.0 + jnp.exp(-jnp.abs(t)))


def _bdot(a, b):
    return jnp.dot(a.astype(BF16), b.astype(BF16), preferred_element_type=F32)


def _bdot_nt(a, b):
    return lax.dot_general(a.astype(BF16), b.astype(BF16), (((1,), (1,)), ((), ())), preferred_element_type=F32)


def _bdot_tn(a, b):
    return lax.dot_general(a.astype(BF16), b.astype(BF16), (((0,), (0,)), ((), ())), preferred_element_type=F32)


def _split_bf16(t, terms):
    parts = []
    for _ in range(terms):
        part = t.astype(BF16)
        parts.append(part)
        t = t - part.astype(F32)
    return parts


def _dot_exact_lhs(a, x, terms=2):
    a = a.astype(BF16)
    out = None
    for part in _split_bf16(x, terms):
        d = jnp.dot(a, part, preferred_element_type=F32)
        out = d if out is None else out + d
    return out


def _dot_split(a, b):
    a_hi, a_lo = _split_bf16(a, 2)
    b_hi, b_lo = _split_bf16(b, 2)
    dot = lambda u, v: jnp.dot(u, v, preferred_element_type=F32)
    return dot(a_hi, b_hi) + (dot(a_lo, b_hi) + dot(a_hi, b_lo))


def _rms(t, gain):
    return t * lax.rsqrt(jnp.mean(t * t, axis=-1, keepdims=True) + NORM_EPS) * gain


def _ada_kernel(c_ref, w_ref, b_ref, o_ref):
    s = _silu(c_ref[...])
    o_ref[...] = jnp.dot(s, w_ref[...], precision=HIGHEST, preferred_element_type=F32) + b_ref[...]


def _ada(c_rows, w, b):
    rows = c_rows.shape[0]
    n = w.shape[1]
    tn = D_MODEL
    return pl.pallas_call(
        _ada_kernel,
        grid=(n // tn,),
        in_specs=[pl.BlockSpec((rows, D_MODEL), lambda j: (0, 0)),
                  pl.BlockSpec((D_MODEL, tn), lambda j: (0, j)),
                  pl.BlockSpec((1, tn), lambda j: (0, j))],
        out_specs=pl.BlockSpec((rows, tn), lambda j: (0, j)),
        out_shape=jax.ShapeDtypeStruct((rows, n), F32),
        compiler_params=_cparams(("arbitrary",)),
        name="ada",
    )(c_rows, w, b.reshape(1, n))


def _to_columns(val, out_ref, scr_ref, ncol):
    nrows, width = val.shape
    groups = width // LANES
    for g in range(groups):
        scr_ref[g] = val[:, g * LANES:(g + 1) * LANES]
    for c in range(ncol):
        for g in range(groups):
            lo = c * width + g * LANES
            out_ref[0, :, lo:lo + LANES] = scr_ref[g, pl.ds(c, nrows // ncol, stride=ncol), :]


def _from_columns(in_val, scr_ref, ncol):
    grid_rows = in_val.shape[0]
    width = in_val.shape[1] // ncol
    groups = width // LANES
    for c in range(ncol):
        for g in range(groups):
            lo = c * width + g * LANES
            scr_ref[g, pl.ds(c, grid_rows, stride=ncol), :] = in_val[:, lo:lo + LANES]
    return jnp.concatenate([scr_ref[g] for g in range(groups)], axis=1)


def _proj_kernel(x_ref, mod_ref, gain_ref, w_ref, qkv_ref, dnz_ref, gqk_ref, gv_ref, gz_ref, sm_ref, *scr, ncol):
    x = x_ref[0]
    y = _rms(x, gain_ref[...])
    h = (y * (1.0 + mod_ref[0, 1:2, :]) + mod_ref[0, 0:1, :]).astype(BF16)
    r = jnp.dot(h, w_ref[...], preferred_element_type=F32)
    o = 0
    for ref, wd in zip((qkv_ref, dnz_ref, gqk_ref, gv_ref, gz_ref, sm_ref), PROJ_WIDTHS):
        if ncol > 1 and (ref is gqk_ref or ref is gv_ref):
            _to_columns(r[:, o:o + wd], ref, scr[0], ncol)
        else:
            ref[0] = r[:, o:o + wd]
        o += wd


def _proj(x, mods, mod_row, gain, w, tm, ncol):
    b_, t_, _ = x.shape
    tok = lambda wd: pl.BlockSpec((1, tm, wd), lambda b, j: (b, j, 0))
    colm = lambda wd: pl.BlockSpec((1, tm // ncol, ncol * wd), lambda b, j: (b, j, 0))
    col_major = (False, False, ncol > 1, ncol > 1, False, False)
    scratch = [pltpu.VMEM((GLA_V_WIDTH // LANES, tm, LANES), F32)] if ncol > 1 else []
    return pl.pallas_call(
        functools.partial(_proj_kernel, ncol=ncol),
        grid=(b_, t_ // tm),
        in_specs=[tok(D_MODEL),
                  pl.BlockSpec((1, SUBLANES, D_MODEL), lambda b, j: (mod_row(b), 0, 0)),
                  pl.BlockSpec((1, D_MODEL), lambda b, j: (0, 0)),
                  pl.BlockSpec((D_MODEL, PROJ_COLS), lambda b, j: (0, 0))],
        out_specs=[colm(wd) if cm else tok(wd) for wd, cm in zip(PROJ_WIDTHS, col_major)],
        out_shape=[jax.ShapeDtypeStruct((b_, t_ // ncol, ncol * wd) if cm else (b_, t_, wd), F32)
                   for wd, cm in zip(PROJ_WIDTHS, col_major)],
        scratch_shapes=scratch,
        compiler_params=_cparams(("parallel", "parallel")),
        name="proj",
    )(x, mods, gain.reshape(1, D_MODEL), w)


def _prep_kernel(cur_ref, prev_ref, next_ref, sm_ref, cw_ref, ap_ref, q_ref, k_ref, v_ref, p_ref, *rest, ncol):
    if ncol > 1:
        pcol_ref, ext_ref, col_scr = rest
    else:
        (ext_ref,) = rest
    j = pl.program_id(1)
    nj = pl.num_programs(1)
    tl = cur_ref.shape[1]
    halo = SUBLANES
    pad = CONV_W // 2
    ext_ref[0:halo, :] = jnp.where(j > 0, prev_ref[0], 0.0)
    ext_ref[halo:halo + tl, :] = cur_ref[0]
    ext_ref[halo + tl:2 * halo + tl, :] = jnp.where(j < nj - 1, next_ref[0], 0.0)
    acc = None
    for w in range(CONV_W):
        term = ext_ref[pl.ds(halo - pad + w, tl), :] * cw_ref[w:w + 1, :]
        acc = term if acc is None else acc + term
    s = _silu(acc)
    for h in range(DN_HEADS):
        lo = h * DN_HEAD_DIM
        qh = s[:, lo:lo + DN_HEAD_DIM]
        kh = s[:, DN_WIDTH + lo:DN_WIDTH + lo + DN_HEAD_DIM]
        q_ref[0, :, lo:lo + DN_HEAD_DIM] = (qh * lax.rsqrt(jnp.sum(qh * qh, axis=-1, keepdims=True) + NORM_EPS)
                                            * DN_HEAD_DIM ** -0.5)
        k_ref[0, :, lo:lo + DN_HEAD_DIM] = kh * lax.rsqrt(jnp.sum(kh * kh, axis=-1, keepdims=True) + NORM_EPS)
    v_ref[0] = s[:, 2 * DN_WIDTH:]
    sm = sm_ref[0]
    lane = lax.broadcasted_iota(jnp.int32, sm.shape, 1)
    beta = _sigmoid(sm)
    gdec = ap_ref[0:1, :] * _softplus(sm + ap_ref[1:2, :])
    p = jnp.where(lane < GDEC_LANE, beta, jnp.where(lane < GLR_LANE, gdec, sm))
    p_ref[0] = p
    if ncol > 1:
        _to_columns(p, pcol_ref, col_scr, ncol)


def _prep(qkv, sm, cw, ap, tl, ncol):
    b_, t_, _ = qkv.shape
    nb = t_ // tl
    r8 = tl // SUBLANES
    last8 = t_ // SUBLANES - 1
    tok = lambda wd: pl.BlockSpec((1, tl, wd), lambda b, j: (b, j, 0))
    out_specs = [tok(DN_WIDTH), tok(DN_WIDTH), tok(DN_WIDTH), tok(SMALL_W)]
    out_shape = [jax.ShapeDtypeStruct((b_, t_, DN_WIDTH), F32)] * 3 + [jax.ShapeDtypeStruct((b_, t_, SMALL_W), F32)]
    scratch = [pltpu.VMEM((tl + 2 * SUBLANES, QKV_W), F32)]
    if ncol > 1:
        out_specs.append(pl.BlockSpec((1, tl // ncol, ncol * SMALL_W), lambda b, j: (b, j, 0)))
        out_shape.append(jax.ShapeDtypeStruct((b_, t_ // ncol, ncol * SMALL_W), F32))
        scratch.append(pltpu.VMEM((1, tl, LANES), F32))
    outs = pl.pallas_call(
        functools.partial(_prep_kernel, ncol=ncol),
        grid=(b_, nb),
        in_specs=[tok(QKV_W),
                  pl.BlockSpec((1, SUBLANES, QKV_W), lambda b, j: (b, jnp.maximum(j * r8 - 1, 0), 0)),
                  pl.BlockSpec((1, SUBLANES, QKV_W), lambda b, j: (b, jnp.minimum((j + 1) * r8, last8), 0)),
                  tok(SMALL_W),
                  pl.BlockSpec((SUBLANES, QKV_W), lambda b, j: (0, 0)),
                  pl.BlockSpec((SUBLANES, SMALL_W), lambda b, j: (0, 0))],
        out_specs=out_specs,
        out_shape=out_shape,
        scratch_shapes=scratch,
        compiler_params=_cparams(("parallel", "parallel")),
        name="prep",
    )(qkv, qkv, qkv, sm, cw, ap)
    return outs if ncol > 1 else list(outs) + [outs[3]]


def _order_masks(n, rev):
    ri = lax.broadcasted_iota(jnp.int32, (n, n), 0)
    ci = lax.broadcasted_iota(jnp.int32, (n, n), 1)
    if rev:
        return ri, ci, ci >= ri, ci > ri
    return ri, ci, ci <= ri, ci < ri


def _dn_kernel(q_ref, k_ref, v_ref, p_ref, s0_ref, o_ref, st_ref, s_scr, *, rev):
    j = pl.program_id(1)
    nj = pl.num_programs(1)

    @pl.when(j == 0)
    def _():
        s_scr[...] = s0_ref[0]

    c_ = CHUNK
    tb = q_ref.shape[1]
    nchunk = tb // c_
    d = 1 if rev else 0
    ri, ci, incl, strict = _order_masks(c_, rev)
    eye = (ri == ci).astype(F32)
    rb, cb, incl_b, _ = _order_masks(tb, rev)
    cum_mat = incl_b & ((rb // c_) == (cb // c_))
    gcum_all = _dot_exact_lhs(cum_mat, p_ref[0], terms=3)
    gcum_all_t = gcum_all.T
    level_masks = []
    s = 1
    while s < c_:
        level_masks.append(strict & ((ri // (2 * s)) == (ci // (2 * s))) & ((ri // s) != (ci // s)))
        s *= 2
    last = 0 if rev else c_ - 1

    order = [nchunk - 1 - cc if rev else cc for cc in range(nchunk)]
    inst = [(c, h) for c in order for h in range(DN_HEADS)]
    pre = {}
    for c, h in inst:
        r0 = c * c_
        lo = h * DN_HEAD_DIM
        lb = BETA_LANE + d * DN_HEADS + h
        lg = GDEC_LANE + d * DN_HEADS + h
        beta = p_ref[0, r0:r0 + c_, lb:lb + 1]
        gc = gcum_all[r0:r0 + c_, lg:lg + 1]
        gr = gcum_all_t[lg:lg + 1, r0:r0 + c_]
        glast = gcum_all[r0 + last:r0 + last + 1, lg:lg + 1]
        decay = jnp.exp(jnp.where(incl, gc - gr, -jnp.inf))
        qh = q_ref[0, r0:r0 + c_, lo:lo + DN_HEAD_DIM]
        kh = k_ref[0, r0:r0 + c_, lo:lo + DN_HEAD_DIM]
        vh = v_ref[0, r0:r0 + c_, lo:lo + DN_HEAD_DIM]
        kb = kh * beta
        eg = jnp.exp(gc)
        pre[c, h] = dict(decay=decay, kq_lhs=jnp.concatenate([kb, qh], axis=0), kh=kh,
                         rhs=jnp.concatenate([vh * beta, kb * eg], axis=1), q_dec=qh * eg,
                         k_dec=kh * jnp.exp(glast - gc), e_last=jnp.exp(glast))
    kq = {i: _bdot_nt(pre[i]["kq_lhs"], pre[i]["kh"]) for i in inst}
    a = {i: jnp.where(strict, kq[i][:c_] * pre[i]["decay"], 0.0) for i in inst}
    attn = {i: kq[i][c_:] * pre[i]["decay"] for i in inst}
    t = {i: eye - jnp.where(level_masks[0], a[i], 0.0) for i in inst}
    for m in level_masks[1:]:
        ta = {i: _bdot(t[i], jnp.where(m, a[i], 0.0)) for i in inst}
        t = {i: t[i] - _bdot(ta[i], t[i]) for i in inst}
    uw = {i: _bdot(t[i], pre[i]["rhs"]) for i in inst}
    for c in order:
        r0 = c * c_
        heads = [(c, h) for h in range(DN_HEADS)]
        st = {i: s_scr[i[1]] for i in heads}
        ws = {i: _bdot(jnp.concatenate([uw[i][:, DN_HEAD_DIM:], pre[i]["q_dec"]], axis=0), st[i]) for i in heads}
        v_new = {i: uw[i][:, :DN_HEAD_DIM] - ws[i][:c_] for i in heads}
        for i in heads:
            lo = i[1] * DN_HEAD_DIM
            o_ref[0, r0:r0 + c_, lo:lo + DN_HEAD_DIM] = ws[i][c_:] + _bdot(attn[i], v_new[i])
        for i in heads:
            s_scr[i[1]] = st[i] * pre[i]["e_last"] + _bdot_tn(pre[i]["k_dec"], v_new[i])

    @pl.when(j == nj - 1)
    def _():
        st_ref[0] = s_scr[...]


def _dn_scan(q, k, v, p, s0, rev, tb):
    b_, t_, _ = q.shape
    nb = t_ // tb
    blk = (lambda b, j: (b, nb - 1 - j, 0)) if rev else (lambda b, j: (b, j, 0))
    tok = lambda wd: pl.BlockSpec((1, tb, wd), blk)
    st_spec = pl.BlockSpec((1, DN_HEADS, DN_HEAD_DIM, DN_HEAD_DIM), lambda b, j: (b, 0, 0, 0))
    return pl.pallas_call(
        functools.partial(_dn_kernel, rev=rev),
        grid=(b_, nb),
        in_specs=[tok(DN_WIDTH), tok(DN_WIDTH), tok(DN_WIDTH), tok(SMALL_W), st_spec],
        out_specs=[tok(DN_WIDTH), st_spec],
        out_shape=[jax.ShapeDtypeStruct((b_, t_, DN_WIDTH), F32),
                   jax.ShapeDtypeStruct((b_, DN_HEADS, DN_HEAD_DIM, DN_HEAD_DIM), F32)],
        scratch_shapes=[pltpu.VMEM((DN_HEADS, DN_HEAD_DIM, DN_HEAD_DIM), F32)],
        compiler_params=_cparams(("parallel", "arbitrary")),
        name="dn_rev" if rev else "dn_fwd",
    )(q, k, v, p, s0)


def _log_sigmoid(t):
    return jnp.minimum(t, 0.0) - jnp.log(1.0 + jnp.exp(-jnp.abs(t)))


def _gla_kernel(qk_ref, v_ref, p_ref, wg_ref, bg_ref, s0_ref, o_ref, st_ref, s_scr, *, rev):
    col = pl.program_id(1)
    rb = pl.program_id(2)
    ncol = pl.num_programs(1)
    nrb = pl.num_programs(2)

    @pl.when((col == 0) & (rb == 0))
    def _():
        s_scr[...] = s0_ref[0]

    c_ = CHUNK
    nchunk = qk_ref.shape[1] // c_
    ncb = p_ref.shape[2] // SMALL_W
    nsub = c_ // SUB
    ri, ci, incl, _ = _order_masks(c_, rev)
    pos_r = (c_ - 1 - ri) if rev else ri
    pos_c = (c_ - 1 - ci) if rev else ci
    mask_diag = incl & ((pos_r // SUB) == (pos_c // SUB))
    mask_off = (pos_c // SUB) < (pos_r // SUB)
    last = 0 if rev else c_ - 1

    cols = [ncb - 1 - i if rev else i for i in range(ncb)]
    chunks = [nchunk - 1 - cc if rev else cc for cc in range(nchunk)]
    order = [(cb, c) for cb in cols for c in chunks]
    la_col = {}
    for cb in cols:
        gate = _dot_split(p_ref[0, :, cb * SMALL_W:(cb + 1) * SMALL_W], wg_ref[...]) + bg_ref[...]
        la_col[cb] = _log_sigmoid(gate) * (1.0 / GLA_GATE_NORMALIZER)
    pre = {}
    for cb, c in order:
        r0 = c * c_
        la = la_col[cb][r0:r0 + c_, :]
        bc = _dot_exact_lhs(incl, la, terms=3)
        bex = bc - la
        b_last = bc[last:last + 1, :]
        first_rows = [(c_ - 1 - SUB * i) if rev else SUB * i for i in range(nsub)]
        pieces = [jnp.broadcast_to(bex[fr:fr + 1, :], (SUB, GLA_K_WIDTH)) for fr in first_rows]
        refd = jnp.concatenate(pieces[::-1] if rev else pieces, axis=0)
        mid = first_rows[nsub // 2]
        rmid = bex[mid:mid + 1, :]
        qk0 = cb * 2 * GLA_K_WIDTH
        qa = qk_ref[0, r0:r0 + c_, qk0:qk0 + GLA_K_WIDTH] * GLA_DK ** -0.5
        ka = qk_ref[0, r0:r0 + c_, qk0 + GLA_K_WIDTH:qk0 + 2 * GLA_K_WIDTH]
        q_in = qa * jnp.exp(bc)
        k_st = ka * jnp.exp(b_last - bc)
        q_d = qa * jnp.exp(bc - refd)
        k_d = ka * jnp.exp(jnp.minimum(refd - bc, EXP_CLAMP))
        q_o = qa * jnp.exp(jnp.minimum(bc - rmid, EXP_CLAMP))
        k_o = ka * jnp.exp(jnp.minimum(rmid - bc, EXP_CLAMP))
        e_last = jnp.exp(b_last)
        v0 = cb * GLA_V_WIDTH
        for h in range(GLA_HEADS):
            ks = slice(h * GLA_DK, (h + 1) * GLA_DK)
            vh = v_ref[0, r0:r0 + c_, v0 + h * GLA_DV:v0 + (h + 1) * GLA_DV]
            attn = (jnp.where(mask_diag, _bdot_nt(q_d[:, ks], k_d[:, ks]), 0.0)
                    + jnp.where(mask_off, _bdot_nt(q_o[:, ks], k_o[:, ks]), 0.0))
            pre[cb, c, h] = dict(attn=attn, vh=vh, q_in=q_in[:, ks], e_last=e_last[:, ks],
                                 kv=_bdot_tn(vh, k_st[:, ks]))
    st = [s_scr[h] for h in range(GLA_HEADS)]
    for cb, c in order:
        r0 = c * c_
        v0 = cb * GLA_V_WIDTH
        for h in range(GLA_HEADS):
            o_ref[0, r0:r0 + c_, v0 + h * GLA_DV:v0 + (h + 1) * GLA_DV] = (
                _bdot_nt(pre[cb, c, h]["q_in"], st[h]) + _bdot(pre[cb, c, h]["attn"], pre[cb, c, h]["vh"]))
        st = [st[h] * pre[cb, c, h]["e_last"] + pre[cb, c, h]["kv"] for h in range(GLA_HEADS)]
    for h in range(GLA_HEADS):
        s_scr[h] = st[h]

    @pl.when((col == ncol - 1) & (rb == nrb - 1))
    def _():
        st_ref[0] = s_scr[...]


def _gla_scan(qk, v, p, wg, bg, s0, rev, ncol, tr):
    b_, rows, _ = qk.shape
    nrb = rows // tr
    ncb = min(GLA_COLS_PER_STEP, ncol)
    ncblk = ncol // ncb
    assert ncb == 1 or nrb == 1
    if rev:
        blk = lambda b, c, r: (b, nrb - 1 - r, ncblk - 1 - c)
    else:
        blk = lambda b, c, r: (b, r, c)
    tok = lambda wd: pl.BlockSpec((1, tr, ncb * wd), blk)
    st_spec = pl.BlockSpec((1, GLA_HEADS, GLA_DV, GLA_DK), lambda b, c, r: (b, 0, 0, 0))
    return pl.pallas_call(
        functools.partial(_gla_kernel, rev=rev),
        grid=(b_, ncblk, nrb),
        in_specs=[tok(2 * GLA_K_WIDTH), tok(GLA_V_WIDTH), tok(SMALL_W),
                  pl.BlockSpec((SMALL_W, GLA_K_WIDTH), lambda b, c, r: (0, 0)),
                  pl.BlockSpec((1, GLA_K_WIDTH), lambda b, c, r: (0, 0)),
                  st_spec],
        out_specs=[tok(GLA_V_WIDTH), st_spec],
        out_shape=[jax.ShapeDtypeStruct((b_, rows, ncol * GLA_V_WIDTH), F32),
                   jax.ShapeDtypeStruct((b_, GLA_HEADS, GLA_DV, GLA_DK), F32)],
        scratch_shapes=[pltpu.VMEM((GLA_HEADS, GLA_DV, GLA_DK), F32)],
        compiler_params=_cparams(("parallel", "arbitrary", "arbitrary")),
        name="gla_rev" if rev else "gla_fwd",
    )(qk, v, p, wg, bg, s0)


def _head_norm_gate(o, z, gain):
    parts = []
    for h in range(o.shape[1] // LANES):
        hs = slice(h * LANES, (h + 1) * LANES)
        parts.append(_rms(o[:, hs], gain) * _silu(z[:, hs]))
    return jnp.concatenate(parts, axis=1)


def _mixout_kernel(*refs, with_router, ncol):
    (x_ref, odf_ref, odb_ref, dnz_ref, ogf_ref, ogb_ref, glz_ref, mod_ref, gd_ref, gg_ref, w_ref, gf_ref) = refs[:12]
    refs = list(refs[12:])
    col_scr = refs.pop() if ncol > 1 else None
    if with_router:
        rw_ref, xo_ref, h_ref, cmb_ref = refs
    else:
        xo_ref, h_ref = refs
    m_dn = _head_norm_gate(odf_ref[0] + odb_ref[0], dnz_ref[0], gd_ref[...])
    o_gl = ogf_ref[0] + ogb_ref[0]
    if ncol > 1:
        o_gl = _from_columns(o_gl, col_scr, ncol)
    m_gl = _head_norm_gate(o_gl, glz_ref[0], gg_ref[...])
    m = jnp.concatenate([m_dn, m_gl], axis=1).astype(BF16)
    x = x_ref[0] + mod_ref[0, 2:3, :] * jnp.dot(m, w_ref[...], preferred_element_type=F32)
    xo_ref[0] = x
    h = _rms(x, gf_ref[...]) * (1.0 + mod_ref[0, 4:5, :]) + mod_ref[0, 3:4, :]
    h_ref[0] = h.astype(BF16)
    if with_router:
        logits = _dot_split(h, rw_ref[...])
        lane = lax.broadcasted_iota(jnp.int32, logits.shape, 1).astype(F32)
        logits = jnp.where(lane < N_EXPERTS, logits, -jnp.inf)
        e = jnp.exp(logits - jnp.max(logits, axis=-1, keepdims=True))
        probs = e / jnp.sum(e, axis=-1, keepdims=True)
        p1 = jnp.max(probs, axis=-1, keepdims=True)
        i1 = jnp.min(jnp.where(probs == p1, lane, float(LANES)), axis=-1, keepdims=True)
        rest = jnp.where(lane == i1, -1.0, probs)
        p2 = jnp.max(rest, axis=-1, keepdims=True)
        i2 = jnp.min(jnp.where(rest == p2, lane, float(LANES)), axis=-1, keepdims=True)
        cmb_ref[0] = (jnp.where(lane == i1, p1, 0.0) + jnp.where(lane == i2, p2, 0.0)) / (p1 + p2)


def _mixout(x, odf, odb, dnz, ogf, ogb, glz, mods, mod_row, gd, gg, w_out, gf, router, tm, ncol):
    b_, t_, _ = x.shape
    with_router = router is not None
    tok = lambda wd: pl.BlockSpec((1, tm, wd), lambda b, j: (b, j, 0))
    colm = lambda wd: pl.BlockSpec((1, tm // ncol, ncol * wd), lambda b, j: (b, j, 0))
    const = lambda shp: pl.BlockSpec(shp, lambda b, j: (0,) * len(shp))
    in_specs = [tok(D_MODEL)] + [tok(DN_WIDTH)] * 3 + [colm(GLA_V_WIDTH)] * 2 + [tok(GLA_V_WIDTH)] + [
        pl.BlockSpec((1, SUBLANES, D_MODEL), lambda b, j: (mod_row(b), 0, 0)),
        const((1, LANES)), const((1, LANES)), const((D_MODEL, D_MODEL)), const((1, D_MODEL))]
    args = [x, odf, odb, dnz, ogf, ogb, glz, mods, gd.reshape(1, LANES), gg.reshape(1, LANES), w_out,
            gf.reshape(1, D_MODEL)]
    out_specs = [tok(D_MODEL), tok(D_MODEL)]
    out_shape = [jax.ShapeDtypeStruct((b_, t_, D_MODEL), F32), jax.ShapeDtypeStruct((b_, t_, D_MODEL), BF16)]
    if with_router:
        in_specs.append(const((D_MODEL, LANES)))
        args.append(router)
        out_specs.append(tok(LANES))
        out_shape.append(jax.ShapeDtypeStruct((b_, t_, LANES), F32))
    scratch = [pltpu.VMEM((GLA_V_WIDTH // LANES, tm, LANES), F32)] if ncol > 1 else []
    return pl.pallas_call(
        functools.partial(_mixout_kernel, with_router=with_router, ncol=ncol),
        grid=(b_, t_ // tm),
        in_specs=in_specs,
        out_specs=out_specs,
        out_shape=out_shape,
        scratch_shapes=scratch,
        compiler_params=_cparams(("parallel", "parallel")),
        name="mixout",
    )(*args)


def _ffn_kernel(x_ref, h_ref, mod_ref, wg_ref, wu_ref, wd_ref, o_ref, *, nsplit):
    h = h_ref[0]
    f = wg_ref.shape[1]
    fs = f // nsplit
    acc = None
    for s in range(nsplit):
        g = jnp.dot(h, wg_ref[:, s * fs:(s + 1) * fs], preferred_element_type=F32)
        u = jnp.dot(h, wu_ref[:, s * fs:(s + 1) * fs], preferred_element_type=F32)
        part = jnp.dot((_silu(g) * u).astype(BF16), wd_ref[s * fs:(s + 1) * fs, :], preferred_element_type=F32)
        acc = part if acc is None else acc + part
    o_ref[0] = x_ref[0] + mod_ref[0, 5:6, :] * acc


def _ffn(x, h, mods, mod_row, w_gu, w_down, tm):
    b_, t_, _ = x.shape
    f = w_down.shape[0]
    tok = lambda: pl.BlockSpec((1, tm, D_MODEL), lambda b, j: (b, j, 0))
    return pl.pallas_call(
        functools.partial(_ffn_kernel, nsplit=2),
        grid=(b_, t_ // tm),
        in_specs=[tok(), tok(),
                  pl.BlockSpec((1, SUBLANES, D_MODEL), lambda b, j: (mod_row(b), 0, 0)),
                  pl.BlockSpec((D_MODEL, f), lambda b, j: (0, 0)),
                  pl.BlockSpec((D_MODEL, f), lambda b, j: (0, 1)),
                  pl.BlockSpec((f, D_MODEL), lambda b, j: (0, 0))],
        out_specs=tok(),
        out_shape=jax.ShapeDtypeStruct((b_, t_, D_MODEL), F32),
        compiler_params=_cparams(("parallel", "parallel")),
        name="ffn",
    )(x, h, mods, w_gu, w_gu, w_down)


def _for_row_tiles(ntiles, fn):
    pair = 2 * MOE_TILE
    npair = ntiles // 2

    def body(i, carry):
        fn(pl.multiple_of(i * pair, pair), pair)
        return carry

    lax.fori_loop(0, npair, body, 0)

    @pl.when(ntiles % 2 == 1)
    def _():
        fn(pl.multiple_of(npair * pair, pair), MOE_TILE)


def _moe_kernel(cnt_ref, x_ref, h_ref, cmb_ref, mod_ref, wg_ref, wu_ref, wd_ref, fg_ref, o_ref,
                acc_ref, xe_ref, ye_ref, rkc_ref, rkr_ref, *, final_norm):
    b = pl.program_id(0)
    j = pl.program_id(1)
    e = pl.program_id(2)
    f = pl.program_id(3)
    nj = pl.num_programs(1)
    ne = pl.num_programs(2)
    nf = pl.num_programs(3)
    tm = h_ref.shape[1]
    ntiles = (cnt_ref[(b * nj + j) * ne + e] + MOE_TILE - 1) // MOE_TILE

    @pl.when((e == 0) & (f == 0))
    def _():
        acc_ref[...] = jnp.zeros_like(acc_ref)
        sb = 2 * LANES
        ri = lax.broadcasted_iota(jnp.int32, (sb, sb), 0)
        ci = lax.broadcasted_iota(jnp.int32, (sb, sb), 1)
        before = (ci < ri).astype(BF16)
        carry = jnp.zeros((1, LANES), F32)
        for s in range(tm // sb):
            sel = (cmb_ref[0, s * sb:(s + 1) * sb, :] > 0.0).astype(F32)
            rank = jnp.dot(before, sel.astype(BF16), preferred_element_type=F32) + carry
            rkc_ref[s * sb:(s + 1) * sb, :] = jnp.where(sel > 0.0, rank, -1.0)
            carry = carry + jnp.sum(sel, axis=0, keepdims=True)
        rkr_ref[...] = rkc_ref[...].T

    @pl.when(f == 0)
    def _():
        rank_row = rkr_ref[pl.ds(e, 1), :]

        def gather(r0, n):
            slot = lax.broadcasted_iota(jnp.int32, (n, tm), 0).astype(F32) + r0.astype(F32)
            onehot = (rank_row == slot).astype(BF16)
            xe_ref[pl.ds(r0, n), :] = jnp.dot(onehot, h_ref[0], preferred_element_type=F32).astype(BF16)

        _for_row_tiles(ntiles, gather)

    def expert(r0, n):
        xr = xe_ref[pl.ds(r0, n), :]
        g = jnp.dot(xr, wg_ref[0], preferred_element_type=F32)
        u = jnp.dot(xr, wu_ref[0], preferred_element_type=F32)
        y = jnp.dot((_silu(g) * u).astype(BF16), wd_ref[0], preferred_element_type=F32)

        @pl.when(f == 0)
        def _():
            ye_ref[pl.ds(r0, n), :] = y

        @pl.when(f > 0)
        def _():
            ye_ref[pl.ds(r0, n), :] += y

    _for_row_tiles(ntiles, expert)

    @pl.when(f == nf - 1)
    def _():
        lane = lax.broadcasted_iota(jnp.int32, (tm, LANES), 1)
        rank_col = jnp.sum(jnp.where(lane == e, rkc_ref[...], 0.0), axis=-1, keepdims=True)
        ce = jnp.sum(jnp.where(lane == e, cmb_ref[0], 0.0), axis=-1, keepdims=True)

        def scatter(r0, n):
            slot = lax.broadcasted_iota(jnp.int32, (tm, n), 1).astype(F32) + r0.astype(F32)
            onehot = (rank_col == slot).astype(BF16)
            back = jnp.dot(onehot, ye_ref[pl.ds(r0, n), :].astype(BF16), preferred_element_type=F32)
            acc_ref[...] += ce * back

        _for_row_tiles(ntiles, scatter)

    @pl.when((e == ne - 1) & (f == nf - 1))
    def _():
        y = x_ref[0] + mod_ref[0, 5:6, :] * acc_ref[...]
        if final_norm:
            y = _rms(y, fg_ref[...])
        o_ref[0] = y


def _moe(x, h, cmb, mods, mod_row, w_gu, w_down, final_gain, tm, tf):
    b_, t_, _ = x.shape
    ne, _, f2 = w_gu.shape
    fe = f2 // 2
    assert fe % tf == 0 and tm % (2 * MOE_TILE) == 0 and ne <= SUBLANES
    nf = fe // tf
    nj = t_ // tm
    counts = jnp.sum((cmb[..., :ne] > 0.0).reshape(b_, nj, tm, ne), axis=2, dtype=jnp.int32).reshape(-1)
    tok = lambda wd: pl.BlockSpec((1, tm, wd), lambda b, j, e, f, cnt: (b, j, 0))
    final_norm = final_gain is not None
    fg = (final_gain if final_norm else jnp.ones((D_MODEL,), F32)).reshape(1, D_MODEL)
    grid_spec = pltpu.PrefetchScalarGridSpec(
        num_scalar_prefetch=1,
        grid=(b_, nj, ne, nf),
        in_specs=[tok(D_MODEL), tok(D_MODEL), tok(LANES),
                  pl.BlockSpec((1, SUBLANES, D_MODEL), lambda b, j, e, f, cnt: (mod_row(b), 0, 0)),
                  pl.BlockSpec((1, D_MODEL, tf), lambda b, j, e, f, cnt: (e, 0, f)),
                  pl.BlockSpec((1, D_MODEL, tf), lambda b, j, e, f, cnt: (e, 0, nf + f)),
                  pl.BlockSpec((1, tf, D_MODEL), lambda b, j, e, f, cnt: (e, f, 0)),
                  pl.BlockSpec((1, D_MODEL), lambda b, j, e, f, cnt: (0, 0))],
        out_specs=tok(D_MODEL),
        scratch_shapes=[pltpu.VMEM((tm, D_MODEL), F32),
                        pltpu.VMEM((tm, D_MODEL), BF16),
                        pltpu.VMEM((tm, D_MODEL), F32),
                        pltpu.VMEM((tm, LANES), F32),
                        pltpu.VMEM((LANES, tm), F32)])
    return pl.pallas_call(
        functools.partial(_moe_kernel, final_norm=final_norm),
        grid_spec=grid_spec,
        out_shape=jax.ShapeDtypeStruct((b_, t_, D_MODEL), F32),
        compiler_params=_cparams(("parallel", "parallel", "arbitrary", "arbitrary")),
        name="moe",
    )(counts, x, h, cmb, mods, w_gu, w_gu, w_down, fg)


def _pack_w_in(w):
    offs = np.cumsum((0, QKV_W, DN_WIDTH, 2 * DN_HEADS, 2 * DN_HEADS, GLA_K_WIDTH, GLA_K_WIDTH, GLA_V_WIDTH,
                      GLA_V_WIDTH, 2 * GLA_GATE_RANK))
    seg = lambda i: w[:, offs[i]:offs[i + 1]]
    small = jnp.concatenate([seg(2), seg(3), seg(8)], axis=1)
    small = jnp.pad(small, ((0, 0), (0, SMALL_W - small.shape[1])))
    return jnp.concatenate([seg(0), seg(1), seg(4), seg(5), seg(6), seg(7), small], axis=1).astype(BF16)


def _pack_gate(w_gate2, b_gate):
    ws = []
    for d in range(2):
        lo = GLR_LANE + d * GLA_GATE_RANK
        ws.append(jnp.zeros((SMALL_W, GLA_K_WIDTH), F32).at[lo:lo + GLA_GATE_RANK].set(w_gate2[d]))
    return ws, [b_gate[d].reshape(1, GLA_K_WIDTH) for d in range(2)]


def _pack_dn_params(a_log, dt_bias):
    neg_a = jnp.zeros((SMALL_W,), F32).at[GDEC_LANE:GLR_LANE].set(-jnp.exp(a_log.reshape(-1)))
    dtb = jnp.zeros((SMALL_W,), F32).at[GDEC_LANE:GLR_LANE].set(dt_bias.reshape(-1))
    return jnp.zeros((SUBLANES, SMALL_W), F32).at[0].set(neg_a).at[1].set(dtb)


def _pad_rows(a, rows):
    return jnp.pad(a, ((0, rows - a.shape[0]),) + ((0, 0),) * (a.ndim - 1))


def _mixers(x_seq, mods, mod_row, gain_mix, w_in_p, cw, ap, wgs, bgs, states, ncol, tm, tb):
    b_, t_, _ = x_seq.shape
    qkv, dnz, gqk, gv, gz, sm = _proj(x_seq, mods, mod_row, gain_mix, w_in_p, tm, ncol)
    q, k, v, p, p_col = _prep(qkv, sm, cw, ap, tm, ncol)
    rows = t_ // ncol
    outs, new_states = [], []
    for d in range(2):
        o_dn, s_dn = _dn_scan(q, k, v, p, states[0][d], bool(d), tb)
        o_gl, s_gl = _gla_scan(gqk, gv, p_col, wgs[d], bgs[d], states[1][d], bool(d), ncol, min(rows, 2 * CHUNK))
        outs.append((o_dn, o_gl))
        new_states.append((s_dn, s_gl))
    states_out = ([new_states[0][0], new_states[1][0]], [new_states[0][1], new_states[1][1]])
    return outs, dnz, gz, states_out


def kernel(x, c, ctx, c_ctx, w_ada, b_ada, norm_mix, norm_ffn, w_in, conv_qkv, dn_a_log, dn_dt_bias, dn_norm,
           gla_w_gate2, gla_b_gate, gla_norm, w_out, ffn_w_gu, ffn_w_down, moe_router, moe_w_gu, moe_w_down,
           final_norm):
    b_, seq, _ = x.shape
    depth = w_ada.shape[0]
    ctx_len = ctx.shape[1]
    mod_rows = -(-(b_ + 1) // SUBLANES) * SUBLANES
    c_rows = _pad_rows(jnp.concatenate([c, c_ctx[None, :]], axis=0), mod_rows)
    lat_row = lambda b: b
    ctx_row = lambda b: b_
    tm_lat, tm_ctx = 512, ctx_len
    tb = 4 * CHUNK

    for i in range(depth):
        last = i == depth - 1
        mods = _ada(c_rows, w_ada[i], b_ada[i]).reshape(mod_rows, N_MOD, D_MODEL)
        mods = jnp.pad(mods, ((0, 0), (0, SUBLANES - N_MOD), (0, 0)))
        w_in_p = _pack_w_in(w_in[i])
        cw = _pad_rows(conv_qkv[i], SUBLANES)
        ap = _pack_dn_params(dn_a_log[i], dn_dt_bias[i])
        wgs, bgs = _pack_gate(gla_w_gate2[i], gla_b_gate[i])
        s0 = ([jnp.zeros((b_, DN_HEADS, DN_HEAD_DIM, DN_HEAD_DIM), F32)] * 2,
              [jnp.zeros((b_, GLA_HEADS, GLA_DV, GLA_DK), F32)] * 2)
        outs_c, dnz_c, gz_c, s_ctx = _mixers(ctx, mods, ctx_row, norm_mix[i], w_in_p, cw, ap, wgs, bgs, s0, 1,
                                             tm_ctx, tb)
        outs_l, dnz_l, gz_l, _ = _mixers(x, mods, lat_row, norm_mix[i], w_in_p, cw, ap, wgs, bgs, s_ctx, GRID_W,
                                         tm_lat, tb)
        w_out_b = w_out[i].astype(BF16)
        is_moe = i % 2 == 1
        if is_moe:
            router = jnp.pad(moe_router[i // 2], ((0, 0), (0, LANES - N_EXPERTS)))
            w_gu = moe_w_gu[i // 2].astype(BF16)
            w_dn = moe_w_down[i // 2].astype(BF16)
        else:
            router = None
            w_gu = ffn_w_gu[i // 2].astype(BF16)
            w_dn = ffn_w_down[i // 2].astype(BF16)

        def channel(tokens, outs, dnz, gz, row, tm, ncol, final_gain):
            res = _mixout(tokens, outs[0][0], outs[1][0], dnz, outs[0][1], outs[1][1], gz, mods, row, dn_norm[i],
                          gla_norm[i], w_out_b, norm_ffn[i], router, tm, ncol)
            if is_moe:
                x_mid, h, cmb = res
                tm_moe = min(1024, tokens.shape[1])
                return _moe(x_mid, h, cmb, mods, row, w_gu, w_dn, final_gain, tm_moe, D_EXPERT_TILE)
            x_mid, h = res
            y = _ffn(x_mid, h, mods, row, w_gu, w_dn, tm)
            if final_gain is not None:
                raise NotImplementedError("final norm is fused into the expert mixer only")
            return y

        x = channel(x, outs_l, dnz_l, gz_l, lat_row, tm_lat, GRID_W, final_norm if last else None)
        if not last:
            ctx = channel(ctx, outs_c, dnz_c, gz_c, ctx_row, tm_ctx, 1, None)
    return x
```

```python
import functools

import jax
import jax.numpy as jnp
import numpy as np
from jax import lax
from jax.experimental import pallas as pl
from jax.experimental.pallas import tpu as pltpu

F32 = jnp.float32
BF16 = jnp.bfloat16
HIGHEST = lax.Precision.HIGHEST

D_MODEL = 1024
GRID_W = 64
DN_HEADS = 4
DN_HEAD_DIM = 128
DN_WIDTH = DN_HEADS * DN_HEAD_DIM
CONV_W = 5
GLA_HEADS = 4
GLA_DK = 64
GLA_DV = 128
GLA_K_WIDTH = GLA_HEADS * GLA_DK
GLA_V_WIDTH = GLA_HEADS * GLA_DV
GLA_GATE_RANK = 16
GLA_GATE_NORMALIZER = 16.0
N_EXPERTS = 8
N_MOD = 6
NORM_EPS = 1e-6

LANES = 128
SUBLANES = 8
CHUNK = 64
SUB = 16
EXP_CLAMP = 80.0
D_EXPERT_TILE = 896
MOE_TILE = 64
MOE_GROUP = 4
GLA_COLS_PER_STEP = 8
VMEM_LIMIT = 56 * 1024 * 1024

QKV_W = 3 * DN_WIDTH
SMALL_W = LANES
PROJ_WIDTHS = (QKV_W, DN_WIDTH, 2 * GLA_K_WIDTH, GLA_V_WIDTH, GLA_V_WIDTH, SMALL_W)
PROJ_COLS = sum(PROJ_WIDTHS)
BETA_LANE = 0
GDEC_LANE = 2 * DN_HEADS
GLR_LANE = 4 * DN_HEADS


def _cparams(sem, vmem=VMEM_LIMIT):
    return pltpu.CompilerParams(dimension_semantics=sem, vmem_limit_bytes=vmem)


def _sigmoid(t):
    return 1.0 / (1.0 + jnp.exp(-t))


def _silu(t):
    return t * _sigmoid(t)


def _softplus(t):
    return jnp.maximum(t, 0.0) + jnp.log(1.0 + jnp.exp(-jnp.abs(t)))


def _bdot(a, b):
    return jnp.dot(a.astype(BF16), b.astype(BF16), preferred_element_type=F32)


def _bdot_nt(a, b):
    return lax.dot_general(a.astype(BF16), b.astype(BF16), (((1,), (1,)), ((), ())), preferred_element_type=F32)


def _bdot_tn(a, b):
    return lax.dot_general(a.astype(BF16), b.astype(BF16), (((0,), (0,)), ((), ())), preferred_element_type=F32)


def _split_bf16(t, terms):
    parts = []
    for _ in range(terms):
        part = t.astype(BF16)
        parts.append(part)
        t = t - part.astype(F32)
    return parts


def _dot_exact_lhs(a, x, terms=2):
    a = a.astype(BF16)
    out = None
    for part in _split_bf16(x, terms):
        d = jnp.dot(a, part, preferred_element_type=F32)
        out = d if out is None else out + d
    return out


def _dot_split(a, b):
    a_hi, a_lo = _split_bf16(a, 2)
    b_hi, b_lo = _split_bf16(b, 2)
    dot = lambda u, v: jnp.dot(u, v, preferred_element_type=F32)
    return dot(a_hi, b_hi) + (dot(a_lo, b_hi) + dot(a_hi, b_lo))


def _rms(t, gain):
    return t * lax.rsqrt(jnp.mean(t * t, axis=-1, keepdims=True) + NORM_EPS) * gain


def _ada_kernel(c_ref, w_ref, b_ref, o_ref):
    s = _silu(c_ref[...])
    o_ref[...] = jnp.dot(s, w_ref[...], precision=HIGHEST, preferred_element_type=F32) + b_ref[...]


def _ada(c_rows, w, b):
    rows = c_rows.shape[0]
    n = w.shape[1]
    tn = D_MODEL
    return pl.pallas_call(
        _ada_kernel,
        grid=(n // tn,),
        in_specs=[pl.BlockSpec((rows, D_MODEL), lambda j: (0, 0)),
                  pl.BlockSpec((D_MODEL, tn), lambda j: (0, j)),
                  pl.BlockSpec((1, tn), lambda j: (0, j))],
        out_specs=pl.BlockSpec((rows, tn), lambda j: (0, j)),
        out_shape=jax.ShapeDtypeStruct((rows, n), F32),
        compiler_params=_cparams(("arbitrary",)),
        name="ada",
    )(c_rows, w, b.reshape(1, n))


def _to_columns(val, out_ref, scr_ref, ncol):
    nrows, width = val.shape
    groups = width // LANES
    grid_rows = nrows // ncol
    pitch = _col_pitch(ncol)
    for g in range(groups):
        for r in range(grid_rows):
            scr_ref[g, r * pitch:r * pitch + ncol, :] = val[r * ncol:(r + 1) * ncol, g * LANES:(g + 1) * LANES]
    for c in range(ncol):
        for g in range(groups):
            lo = c * width + g * LANES
            out_ref[0, :, lo:lo + LANES] = scr_ref[g, pl.ds(c, grid_rows, stride=pitch), :]


def _from_columns(in_val, scr_ref, ncol):
    grid_rows = in_val.shape[0]
    width = in_val.shape[1] // ncol
    groups = width // LANES
    pitch = _col_pitch(ncol)
    for c in range(ncol):
        for g in range(groups):
            lo = c * width + g * LANES
            scr_ref[g, pl.ds(c, grid_rows, stride=pitch), :] = in_val[:, lo:lo + LANES]
    return jnp.concatenate(
        [jnp.concatenate([scr_ref[g, r * pitch:r * pitch + ncol, :] for r in range(grid_rows)], axis=0)
         for g in range(groups)], axis=1)


def _col_pitch(ncol):
    return ncol + SUBLANES // 2


def _col_scratch(groups, grid_rows, ncol):
    rows = -(-grid_rows * _col_pitch(ncol) // SUBLANES) * SUBLANES
    return pltpu.VMEM((groups, rows, LANES), F32)


def _proj_kernel(x_ref, mod_ref, gain_ref, w_ref, qkv_ref, dnz_ref, gqk_ref, gv_ref, gz_ref, sm_ref, *scr, ncol):
    x = x_ref[0]
    y = _rms(x, gain_ref[...])
    h = (y * (1.0 + mod_ref[0, 1:2, :]) + mod_ref[0, 0:1, :]).astype(BF16)
    r = jnp.dot(h, w_ref[...], preferred_element_type=F32)
    o = 0
    for ref, wd in zip((qkv_ref, dnz_ref, gqk_ref, gv_ref, gz_ref, sm_ref), PROJ_WIDTHS):
        if ncol > 1 and (ref is gqk_ref or ref is gv_ref):
            _to_columns(r[:, o:o + wd], ref, scr[0], ncol)
        else:
            ref[0] = r[:, o:o + wd]
        o += wd


def _proj(x, mods, mod_row, gain, w, tm, ncol):
    b_, t_, _ = x.shape
    tok = lambda wd: pl.BlockSpec((1, tm, wd), lambda b, j: (b, j, 0))
    colm = lambda wd: pl.BlockSpec((1, tm // ncol, ncol * wd), lambda b, j: (b, j, 0))
    col_major = (False, False, ncol > 1, ncol > 1, False, False)
    scratch = [_col_scratch(GLA_V_WIDTH // LANES, tm // ncol, ncol)] if ncol > 1 else []
    return pl.pallas_call(
        functools.partial(_proj_kernel, ncol=ncol),
        grid=(b_, t_ // tm),
        in_specs=[tok(D_MODEL),
                  pl.BlockSpec((1, SUBLANES, D_MODEL), lambda b, j: (mod_row(b), 0, 0)),
                  pl.BlockSpec((1, D_MODEL), lambda b, j: (0, 0)),
                  pl.BlockSpec((D_MODEL, PROJ_COLS), lambda b, j: (0, 0))],
        out_specs=[colm(wd) if cm else tok(wd) for wd, cm in zip(PROJ_WIDTHS, col_major)],
        out_shape=[jax.ShapeDtypeStruct((b_, t_ // ncol, ncol * wd) if cm else (b_, t_, wd), F32)
                   for wd, cm in zip(PROJ_WIDTHS, col_major)],
        scratch_shapes=scratch,
        compiler_params=_cparams(("parallel", "parallel")),
        name="proj",
    )(x, mods, gain.reshape(1, D_MODEL), w)


def _prep_kernel(cur_ref, prev_ref, next_ref, sm_ref, cw_ref, ap_ref, q_ref, k_ref, v_ref, p_ref, *rest, ncol):
    if ncol > 1:
        pcol_ref, ext_ref, col_scr = rest
    else:
        (ext_ref,) = rest
    j = pl.program_id(1)
    nj = pl.num_programs(1)
    tl = cur_ref.shape[1]
    halo = SUBLANES
    pad = CONV_W // 2
    ext_ref[0:halo, :] = jnp.where(j > 0, prev_ref[0], 0.0)
    ext_ref[halo:halo + tl, :] = cur_ref[0]
    ext_ref[halo + tl:2 * halo + tl, :] = jnp.where(j < nj - 1, next_ref[0], 0.0)
    acc = None
    for w in range(CONV_W):
        term = ext_ref[pl.ds(halo - pad + w, tl), :] * cw_ref[w:w + 1, :]
        acc = term if acc is None else acc + term
    s = _silu(acc)
    for h in range(DN_HEADS):
        lo = h * DN_HEAD_DIM
        qh = s[:, lo:lo + DN_HEAD_DIM]
        kh = s[:, DN_WIDTH + lo:DN_WIDTH + lo + DN_HEAD_DIM]
        q_ref[0, :, lo:lo + DN_HEAD_DIM] = (qh * lax.rsqrt(jnp.sum(qh * qh, axis=-1, keepdims=True) + NORM_EPS)
                                            * DN_HEAD_DIM ** -0.5)
        k_ref[0, :, lo:lo + DN_HEAD_DIM] = kh * lax.rsqrt(jnp.sum(kh * kh, axis=-1, keepdims=True) + NORM_EPS)
    v_ref[0] = s[:, 2 * DN_WIDTH:]
    sm = sm_ref[0]
    lane = lax.broadcasted_iota(jnp.int32, sm.shape, 1)
    beta = _sigmoid(sm)
    gdec = ap_ref[0:1, :] * _softplus(sm + ap_ref[1:2, :])
    p = jnp.where(lane < GDEC_LANE, beta, jnp.where(lane < GLR_LANE, gdec, sm))
    p_ref[0] = p
    if ncol > 1:
        _to_columns(p, pcol_ref, col_scr, ncol)


def _prep(qkv, sm, cw, ap, tl, ncol):
    b_, t_, _ = qkv.shape
    nb = t_ // tl
    r8 = tl // SUBLANES
    last8 = t_ // SUBLANES - 1
    tok = lambda wd: pl.BlockSpec((1, tl, wd), lambda b, j: (b, j, 0))
    out_specs = [tok(DN_WIDTH), tok(DN_WIDTH), tok(DN_WIDTH), tok(SMALL_W)]
    out_shape = [jax.ShapeDtypeStruct((b_, t_, DN_WIDTH), F32)] * 3 + [jax.ShapeDtypeStruct((b_, t_, SMALL_W), F32)]
    scratch = [pltpu.VMEM((tl + 2 * SUBLANES, QKV_W), F32)]
    if ncol > 1:
        out_specs.append(pl.BlockSpec((1, tl // ncol, ncol * SMALL_W), lambda b, j: (b, j, 0)))
        out_shape.append(jax.ShapeDtypeStruct((b_, t_ // ncol, ncol * SMALL_W), F32))
        scratch.append(_col_scratch(1, tl // ncol, ncol))
    outs = pl.pallas_call(
        functools.partial(_prep_kernel, ncol=ncol),
        grid=(b_, nb),
        in_specs=[tok(QKV_W),
                  pl.BlockSpec((1, SUBLANES, QKV_W), lambda b, j: (b, jnp.maximum(j * r8 - 1, 0), 0)),
                  pl.BlockSpec((1, SUBLANES, QKV_W), lambda b, j: (b, jnp.minimum((j + 1) * r8, last8), 0)),
                  tok(SMALL_W),
                  pl.BlockSpec((SUBLANES, QKV_W), lambda b, j: (0, 0)),
                  pl.BlockSpec((SUBLANES, SMALL_W), lambda b, j: (0, 0))],
        out_specs=out_specs,
        out_shape=out_shape,
        scratch_shapes=scratch,
        compiler_params=_cparams(("parallel", "parallel")),
        name="prep",
    )(qkv, qkv, qkv, sm, cw, ap)
    return outs if ncol > 1 else list(outs) + [outs[3]]


def _order_masks(n, rev):
    ri = lax.broadcasted_iota(jnp.int32, (n, n), 0)
    ci = lax.broadcasted_iota(jnp.int32, (n, n), 1)
    if rev:
        return ri, ci, ci >= ri, ci > ri
    return ri, ci, ci <= ri, ci < ri


def _dn_kernel(qf_ref, kf_ref, vf_ref, pf_ref, qb_ref, kb_ref, vb_ref, pb_ref, s0_ref, of_ref, ob_ref, st_ref, s_scr):
    j = pl.program_id(1)
    nj = pl.num_programs(1)

    @pl.when(j == 0)
    def _():
        s_scr[...] = s0_ref[0]

    c_ = CHUNK
    tb = qf_ref.shape[1]
    nchunk = tb // c_
    streams = ((qf_ref, kf_ref, vf_ref, pf_ref, of_ref), (qb_ref, kb_ref, vb_ref, pb_ref, ob_ref))
    masks, gcum, gcum_t, order = [], [], [], []
    for d, rev in enumerate((False, True)):
        ri, ci, incl, strict = _order_masks(c_, rev)
        levels = []
        s = 1
        while s < c_:
            levels.append(strict & ((ri // (2 * s)) == (ci // (2 * s))) & ((ri // s) != (ci // s)))
            s *= 2
        masks.append(dict(incl=incl, strict=strict, eye=(ri == ci).astype(F32), levels=levels,
                          last=0 if rev else c_ - 1))
        rb, cb, incl_b, _ = _order_masks(tb, rev)
        cum_mat = incl_b & ((rb // c_) == (cb // c_))
        gcum.append(_dot_exact_lhs(cum_mat, streams[d][3][0], terms=3))
        gcum_t.append(gcum[d].T)
        order.append([nchunk - 1 - cc if rev else cc for cc in range(nchunk)])

    inst = [(d, c, h) for step in range(nchunk) for d in range(2) for c in (order[d][step],)
            for h in range(DN_HEADS)]
    pre = {}
    for d, c, h in inst:
        q_ref, k_ref, v_ref, p_ref, _ = streams[d]
        m = masks[d]
        r0 = c * c_
        lo = h * DN_HEAD_DIM
        lb = BETA_LANE + d * DN_HEADS + h
        lg = GDEC_LANE + d * DN_HEADS + h
        beta = p_ref[0, r0:r0 + c_, lb:lb + 1]
        gc = gcum[d][r0:r0 + c_, lg:lg + 1]
        gr = gcum_t[d][lg:lg + 1, r0:r0 + c_]
        glast = gcum[d][r0 + m["last"]:r0 + m["last"] + 1, lg:lg + 1]
        decay = jnp.exp(jnp.where(m["incl"], gc - gr, -jnp.inf))
        qh = q_ref[0, r0:r0 + c_, lo:lo + DN_HEAD_DIM]
        kh = k_ref[0, r0:r0 + c_, lo:lo + DN_HEAD_DIM]
        vh = v_ref[0, r0:r0 + c_, lo:lo + DN_HEAD_DIM]
        kb = kh * beta
        eg = jnp.exp(gc)
        pre[d, c, h] = dict(decay=decay, kq_lhs=jnp.concatenate([kb, qh], axis=0), kh=kh,
                            rhs=jnp.concatenate([vh * beta, kb * eg], axis=1), q_dec=qh * eg,
                            k_dec=kh * jnp.exp(glast - gc), e_last=jnp.exp(glast))
    kq = {i: _bdot_nt(pre[i]["kq_lhs"], pre[i]["kh"]) for i in inst}
    a = {i: jnp.where(masks[i[0]]["strict"], kq[i][:c_] * pre[i]["decay"], 0.0) for i in inst}
    attn = {i: kq[i][c_:] * pre[i]["decay"] for i in inst}
    t = {i: masks[i[0]]["eye"] - jnp.where(masks[i[0]]["levels"][0], a[i], 0.0) for i in inst}
    for lvl in range(1, len(masks[0]["levels"])):
        ta = {i: _bdot(t[i], jnp.where(masks[i[0]]["levels"][lvl], a[i], 0.0)) for i in inst}
        t = {i: t[i] - _bdot(ta[i], t[i]) for i in inst}
    uw = {i: _bdot(t[i], pre[i]["rhs"]) for i in inst}
    for step in range(nchunk):
        group = [(d, order[d][step], h) for d in range(2) for h in range(DN_HEADS)]
        st = {i: s_scr[i[0], i[2]] for i in group}
        ws = {i: _bdot(jnp.concatenate([uw[i][:, DN_HEAD_DIM:], pre[i]["q_dec"]], axis=0), st[i]) for i in group}
        v_new = {i: uw[i][:, :DN_HEAD_DIM] - ws[i][:c_] for i in group}
        for i in group:
            d, c, h = i
            streams[d][4][0, c * c_:(c + 1) * c_, h * DN_HEAD_DIM:(h + 1) * DN_HEAD_DIM] = (
                ws[i][c_:] + _bdot(attn[i], v_new[i]))
        for i in group:
            s_scr[i[0], i[2]] = st[i] * pre[i]["e_last"] + _bdot_tn(pre[i]["k_dec"], v_new[i])

    @pl.when(j == nj - 1)
    def _():
        st_ref[0] = s_scr[...]


def _dn_scan(q, k, v, p, s0, tb):
    b_, t_, _ = q.shape
    nb = t_ // tb
    fwd = lambda wd: pl.BlockSpec((1, tb, wd), lambda b, j: (b, j, 0))
    bwd = lambda wd: pl.BlockSpec((1, tb, wd), lambda b, j: (b, nb - 1 - j, 0))
    widths = (DN_WIDTH, DN_WIDTH, DN_WIDTH, SMALL_W)
    st_shape = (2, DN_HEADS, DN_HEAD_DIM, DN_HEAD_DIM)
    st_spec = pl.BlockSpec((1,) + st_shape, lambda b, j: (b, 0, 0, 0, 0))
    return pl.pallas_call(
        _dn_kernel,
        grid=(b_, nb),
        in_specs=[fwd(w) for w in widths] + [bwd(w) for w in widths] + [st_spec],
        out_specs=[fwd(DN_WIDTH), bwd(DN_WIDTH), st_spec],
        out_shape=[jax.ShapeDtypeStruct((b_, t_, DN_WIDTH), F32)] * 2 + [jax.ShapeDtypeStruct((b_,) + st_shape, F32)],
        scratch_shapes=[pltpu.VMEM(st_shape, F32)],
        compiler_params=_cparams(("parallel", "arbitrary")),
        name="dn",
    )(q, k, v, p, q, k, v, p, s0)


def _log_sigmoid(t):
    return jnp.minimum(t, 0.0) - jnp.log(1.0 + jnp.exp(-jnp.abs(t)))


def _gla_kernel(qkf_ref, vf_ref, pf_ref, qkb_ref, vb_ref, pb_ref, wg_ref, bg_ref, s0_ref, of_ref, ob_ref, st_ref,
                s_scr):
    col = pl.program_id(1)
    rb = pl.program_id(2)
    ncol = pl.num_programs(1)
    nrb = pl.num_programs(2)

    @pl.when((col == 0) & (rb == 0))
    def _():
        s_scr[...] = s0_ref[0]

    c_ = CHUNK
    nchunk = qkf_ref.shape[1] // c_
    ncb = pf_ref.shape[2] // SMALL_W
    nsub = c_ // SUB
    streams = ((qkf_ref, vf_ref, pf_ref, of_ref), (qkb_ref, vb_ref, pb_ref, ob_ref))
    masks, order = [], []
    for rev in (False, True):
        ri, ci, incl, _ = _order_masks(c_, rev)
        pos_r = (c_ - 1 - ri) if rev else ri
        pos_c = (c_ - 1 - ci) if rev else ci
        first_rows = [(c_ - 1 - SUB * i) if rev else SUB * i for i in range(nsub)]
        masks.append(dict(incl=incl, diag=incl & ((pos_r // SUB) == (pos_c // SUB)),
                          off=(pos_c // SUB) < (pos_r // SUB), last=0 if rev else c_ - 1, first_rows=first_rows,
                          rev=rev))
        cols = [ncb - 1 - i if rev else i for i in range(ncb)]
        chunks = [nchunk - 1 - cc if rev else cc for cc in range(nchunk)]
        order.append([(cb, c) for cb in cols for c in chunks])
    nstep = len(order[0])
    la_col = {}
    for d in range(2):
        for cb in range(ncb):
            gate = _dot_split(streams[d][2][0, :, cb * SMALL_W:(cb + 1) * SMALL_W], wg_ref[d]) + bg_ref[d]
            la_col[d, cb] = _log_sigmoid(gate) * (1.0 / GLA_GATE_NORMALIZER)
    pre = {}
    for step in range(nstep):
        for d in range(2):
            cb, c = order[d][step]
            qk_ref, v_ref, _, _ = streams[d]
            m = masks[d]
            r0 = c * c_
            la = la_col[d, cb][r0:r0 + c_, :]
            bc = _dot_exact_lhs(m["incl"], la, terms=3)
            bex = bc - la
            b_last = bc[m["last"]:m["last"] + 1, :]
            pieces = [jnp.broadcast_to(bex[fr:fr + 1, :], (SUB, GLA_K_WIDTH)) for fr in m["first_rows"]]
            refd = jnp.concatenate(pieces[::-1] if m["rev"] else pieces, axis=0)
            mid = m["first_rows"][nsub // 2]
            rmid = bex[mid:mid + 1, :]
            qk0 = cb * 2 * GLA_K_WIDTH
            qa = qk_ref[0, r0:r0 + c_, qk0:qk0 + GLA_K_WIDTH] * GLA_DK ** -0.5
            ka = qk_ref[0, r0:r0 + c_, qk0 + GLA_K_WIDTH:qk0 + 2 * GLA_K_WIDTH]
            q_in = qa * jnp.exp(bc)
            k_st = ka * jnp.exp(b_last - bc)
            q_d = qa * jnp.exp(bc - refd)
            k_d = ka * jnp.exp(jnp.minimum(refd - bc, EXP_CLAMP))
            q_o = qa * jnp.exp(jnp.minimum(bc - rmid, EXP_CLAMP))
            k_o = ka * jnp.exp(jnp.minimum(rmid - bc, EXP_CLAMP))
            e_last = jnp.exp(b_last)
            v0 = cb * GLA_V_WIDTH
            for h in range(GLA_HEADS):
                ks = slice(h * GLA_DK, (h + 1) * GLA_DK)
                vh = v_ref[0, r0:r0 + c_, v0 + h * GLA_DV:v0 + (h + 1) * GLA_DV]
                attn = (jnp.where(m["diag"], _bdot_nt(q_d[:, ks], k_d[:, ks]), 0.0)
                        + jnp.where(m["off"], _bdot_nt(q_o[:, ks], k_o[:, ks]), 0.0))
                pre[d, step, h] = dict(attn=attn, vh=vh, q_in=q_in[:, ks], e_last=e_last[:, ks],
                                       kv=_bdot_tn(vh, k_st[:, ks]))
    st = {(d, h): s_scr[d, h] for d in range(2) for h in range(GLA_HEADS)}
    for step in range(nstep):
        for d in range(2):
            cb, c = order[d][step]
            r0 = c * c_
            v0 = cb * GLA_V_WIDTH
            for h in range(GLA_HEADS):
                i = (d, step, h)
                streams[d][3][0, r0:r0 + c_, v0 + h * GLA_DV:v0 + (h + 1) * GLA_DV] = (
                    _bdot_nt(pre[i]["q_in"], st[d, h]) + _bdot(pre[i]["attn"], pre[i]["vh"]))
                st[d, h] = st[d, h] * pre[i]["e_last"] + pre[i]["kv"]
    for d in range(2):
        for h in range(GLA_HEADS):
            s_scr[d, h] = st[d, h]

    @pl.when((col == ncol - 1) & (rb == nrb - 1))
    def _():
        st_ref[0] = s_scr[...]


def _gla_scan(qk, v, p, wg, bg, s0, ncol, tr):
    b_, rows, _ = qk.shape
    nrb = rows // tr
    ncb = min(GLA_COLS_PER_STEP, ncol)
    ncblk = ncol // ncb
    assert ncb == 1 or nrb == 1
    fwd = lambda wd: pl.BlockSpec((1, tr, ncb * wd), lambda b, c, r: (b, r, c))
    bwd = lambda wd: pl.BlockSpec((1, tr, ncb * wd), lambda b, c, r: (b, nrb - 1 - r, ncblk - 1 - c))
    widths = (2 * GLA_K_WIDTH, GLA_V_WIDTH, SMALL_W)
    st_shape = (2, GLA_HEADS, GLA_DV, GLA_DK)
    st_spec = pl.BlockSpec((1,) + st_shape, lambda b, c, r: (b, 0, 0, 0, 0))
    return pl.pallas_call(
        _gla_kernel,
        grid=(b_, ncblk, nrb),
        in_specs=[fwd(w) for w in widths] + [bwd(w) for w in widths] + [
            pl.BlockSpec((2, SMALL_W, GLA_K_WIDTH), lambda b, c, r: (0, 0, 0)),
            pl.BlockSpec((2, 1, GLA_K_WIDTH), lambda b, c, r: (0, 0, 0)),
            st_spec],
        out_specs=[fwd(GLA_V_WIDTH), bwd(GLA_V_WIDTH), st_spec],
        out_shape=[jax.ShapeDtypeStruct((b_, rows, ncol * GLA_V_WIDTH), F32)] * 2 + [
            jax.ShapeDtypeStruct((b_,) + st_shape, F32)],
        scratch_shapes=[pltpu.VMEM(st_shape, F32)],
        compiler_params=_cparams(("parallel", "arbitrary", "arbitrary")),
        name="gla",
    )(qk, v, p, qk, v, p, wg, bg, s0)


def _head_norm_gate(o, z, gain):
    parts = []
    for h in range(o.shape[1] // LANES):
        hs = slice(h * LANES, (h + 1) * LANES)
        parts.append(_rms(o[:, hs], gain) * _silu(z[:, hs]))
    return jnp.concatenate(parts, axis=1)


def _mixout_kernel(*refs, with_router, ncol):
    (x_ref, odf_ref, odb_ref, dnz_ref, ogf_ref, ogb_ref, glz_ref, mod_ref, gd_ref, gg_ref, w_ref, gf_ref) = refs[:12]
    refs = list(refs[12:])
    col_scr = refs.pop() if ncol > 1 else None
    if with_router:
        rw_ref, xo_ref, h_ref, cmb_ref = refs
    else:
        xo_ref, h_ref = refs
    m_dn = _head_norm_gate(odf_ref[0] + odb_ref[0], dnz_ref[0], gd_ref[...])
    o_gl = ogf_ref[0] + ogb_ref[0]
    if ncol > 1:
        o_gl = _from_columns(o_gl, col_scr, ncol)
    m_gl = _head_norm_gate(o_gl, glz_ref[0], gg_ref[...])
    m = jnp.concatenate([m_dn, m_gl], axis=1).astype(BF16)
    x = x_ref[0] + mod_ref[0, 2:3, :] * jnp.dot(m, w_ref[...], preferred_element_type=F32)
    xo_ref[0] = x
    h = _rms(x, gf_ref[...]) * (1.0 + mod_ref[0, 4:5, :]) + mod_ref[0, 3:4, :]
    h_ref[0] = h.astype(BF16)
    if with_router:
        logits = _dot_split(h, rw_ref[...])
        lane = lax.broadcasted_iota(jnp.int32, logits.shape, 1).astype(F32)
        logits = jnp.where(lane < N_EXPERTS, logits, -jnp.inf)
        e = jnp.exp(logits - jnp.max(logits, axis=-1, keepdims=True))
        probs = e / jnp.sum(e, axis=-1, keepdims=True)
        p1 = jnp.max(probs, axis=-1, keepdims=True)
        i1 = jnp.min(jnp.where(probs == p1, lane, float(LANES)), axis=-1, keepdims=True)
        rest = jnp.where(lane == i1, -1.0, probs)
        p2 = jnp.max(rest, axis=-1, keepdims=True)
        i2 = jnp.min(jnp.where(rest == p2, lane, float(LANES)), axis=-1, keepdims=True)
        cmb_ref[0] = (jnp.where(lane == i1, p1, 0.0) + jnp.where(lane == i2, p2, 0.0)) / (p1 + p2)


def _mixout(x, odf, odb, dnz, ogf, ogb, glz, mods, mod_row, gd, gg, w_out, gf, router, tm, ncol):
    b_, t_, _ = x.shape
    with_router = router is not None
    tok = lambda wd: pl.BlockSpec((1, tm, wd), lambda b, j: (b, j, 0))
    colm = lambda wd: pl.BlockSpec((1, tm // ncol, ncol * wd), lambda b, j: (b, j, 0))
    const = lambda shp: pl.BlockSpec(shp, lambda b, j: (0,) * len(shp))
    in_specs = [tok(D_MODEL)] + [tok(DN_WIDTH)] * 3 + [colm(GLA_V_WIDTH)] * 2 + [tok(GLA_V_WIDTH)] + [
        pl.BlockSpec((1, SUBLANES, D_MODEL), lambda b, j: (mod_row(b), 0, 0)),
        const((1, LANES)), const((1, LANES)), const((D_MODEL, D_MODEL)), const((1, D_MODEL))]
    args = [x, odf, odb, dnz, ogf, ogb, glz, mods, gd.reshape(1, LANES), gg.reshape(1, LANES), w_out,
            gf.reshape(1, D_MODEL)]
    out_specs = [tok(D_MODEL), tok(D_MODEL)]
    out_shape = [jax.ShapeDtypeStruct((b_, t_, D_MODEL), F32), jax.ShapeDtypeStruct((b_, t_, D_MODEL), BF16)]
    if with_router:
        in_specs.append(const((D_MODEL, LANES)))
        args.append(router)
        out_specs.append(tok(LANES))
        out_shape.append(jax.ShapeDtypeStruct((b_, t_, LANES), F32))
    scratch = [_col_scratch(GLA_V_WIDTH // LANES, tm // ncol, ncol)] if ncol > 1 else []
    return pl.pallas_call(
        functools.partial(_mixout_kernel, with_router=with_router, ncol=ncol),
        grid=(b_, t_ // tm),
        in_specs=in_specs,
        out_specs=out_specs,
        out_shape=out_shape,
        scratch_shapes=scratch,
        compiler_params=_cparams(("parallel", "parallel")),
        name="mixout",
    )(*args)


def _ffn_kernel(x_ref, h_ref, mod_ref, wg_ref, wu_ref, wd_ref, o_ref, *, nsplit):
    h = h_ref[0]
    f = wg_ref.shape[1]
    fs = f // nsplit
    acc = None
    for s in range(nsplit):
        g = jnp.dot(h, wg_ref[:, s * fs:(s + 1) * fs], preferred_element_type=F32)
        u = jnp.dot(h, wu_ref[:, s * fs:(s + 1) * fs], preferred_element_type=F32)
        part = jnp.dot((_silu(g) * u).astype(BF16), wd_ref[s * fs:(s + 1) * fs, :], preferred_element_type=F32)
        acc = part if acc is None else acc + part
    o_ref[0] = x_ref[0] + mod_ref[0, 5:6, :] * acc


def _ffn(x, h, mods, mod_row, w_gu, w_down, tm):
    b_, t_, _ = x.shape
    f = w_down.shape[0]
    tok = lambda: pl.BlockSpec((1, tm, D_MODEL), lambda b, j: (b, j, 0))
    return pl.pallas_call(
        functools.partial(_ffn_kernel, nsplit=2),
        grid=(b_, t_ // tm),
        in_specs=[tok(), tok(),
                  pl.BlockSpec((1, SUBLANES, D_MODEL), lambda b, j: (mod_row(b), 0, 0)),
                  pl.BlockSpec((D_MODEL, f), lambda b, j: (0, 0)),
                  pl.BlockSpec((D_MODEL, f), lambda b, j: (0, 1)),
                  pl.BlockSpec((f, D_MODEL), lambda b, j: (0, 0))],
        out_specs=tok(),
        out_shape=jax.ShapeDtypeStruct((b_, t_, D_MODEL), F32),
        compiler_params=_cparams(("parallel", "parallel")),
        name="ffn",
    )(x, h, mods, w_gu, w_gu, w_down)


def _for_row_tiles(ntiles, fn):
    rows = MOE_GROUP * MOE_TILE
    ngroup = ntiles // MOE_GROUP

    def body(i, carry):
        fn(pl.multiple_of(i * rows, rows), rows)
        return carry

    lax.fori_loop(0, ngroup, body, 0)
    for k in range(1, MOE_GROUP):
        @pl.when(ntiles % MOE_GROUP == k)
        def _():
            fn(pl.multiple_of(ngroup * rows, rows), k * MOE_TILE)


def _moe_kernel(cnt_ref, x_ref, h_ref, cmb_ref, mod_ref, wg_ref, wu_ref, wd_ref, fg_ref, o_ref,
                acc_ref, xe_ref, ye_ref, rkc_ref, rkr_ref, *, final_norm):
    b = pl.program_id(0)
    j = pl.program_id(1)
    e = pl.program_id(2)
    f = pl.program_id(3)
    nj = pl.num_programs(1)
    ne = pl.num_programs(2)
    nf = pl.num_programs(3)
    tm = h_ref.shape[1]
    ntiles = (cnt_ref[(b * nj + j) * ne + e] + MOE_TILE - 1) // MOE_TILE

    @pl.when((e == 0) & (f == 0))
    def _():
        acc_ref[...] = jnp.zeros_like(acc_ref)
        sb = 2 * LANES
        ri = lax.broadcasted_iota(jnp.int32, (sb, sb), 0)
        ci = lax.broadcasted_iota(jnp.int32, (sb, sb), 1)
        before = (ci < ri).astype(BF16)
        carry = jnp.zeros((1, LANES), F32)
        for s in range(tm // sb):
            sel = (cmb_ref[0, s * sb:(s + 1) * sb, :] > 0.0).astype(F32)
            rank = jnp.dot(before, sel.astype(BF16), preferred_element_type=F32) + carry
            rkc_ref[s * sb:(s + 1) * sb, :] = jnp.where(sel > 0.0, rank, -1.0)
            carry = carry + jnp.sum(sel, axis=0, keepdims=True)
        rkr_ref[...] = rkc_ref[...].T

    @pl.when(f == 0)
    def _():
        rank_row = rkr_ref[pl.ds(e, 1), :]

        def gather(r0, n):
            slot = lax.broadcasted_iota(jnp.int32, (n, tm), 0).astype(F32) + r0.astype(F32)
            onehot = (rank_row == slot).astype(BF16)
            xe_ref[pl.ds(r0, n), :] = jnp.dot(onehot, h_ref[0], preferred_element_type=F32).astype(BF16)

        _for_row_tiles(ntiles, gather)

    def expert(r0, n):
        xr = xe_ref[pl.ds(r0, n), :]
        g = jnp.dot(xr, wg_ref[0], preferred_element_type=F32)
        u = jnp.dot(xr, wu_ref[0], preferred_element_type=F32)
        y = jnp.dot((_silu(g) * u).astype(BF16), wd_ref[0], preferred_element_type=F32)

        @pl.when(f == 0)
        def _():
            ye_ref[pl.ds(r0, n), :] = y

        @pl.when(f > 0)
        def _():
            ye_ref[pl.ds(r0, n), :] += y

    _for_row_tiles(ntiles, expert)

    @pl.when(f == nf - 1)
    def _():
        lane = lax.broadcasted_iota(jnp.int32, (tm, LANES), 1)
        rank_col = jnp.sum(jnp.where(lane == e, rkc_ref[...], 0.0), axis=-1, keepdims=True)
        ce = jnp.sum(jnp.where(lane == e, cmb_ref[0], 0.0), axis=-1, keepdims=True)

        def scatter(r0, n):
            slot = lax.broadcasted_iota(jnp.int32, (tm, n), 1).astype(F32) + r0.astype(F32)
            onehot = (rank_col == slot).astype(BF16)
            back = jnp.dot(onehot, ye_ref[pl.ds(r0, n), :].astype(BF16), preferred_element_type=F32)
            acc_ref[...] += ce * back

        _for_row_tiles(ntiles, scatter)

    @pl.when((e == ne - 1) & (f == nf - 1))
    def _():
        y = x_ref[0] + mod_ref[0, 5:6, :] * acc_ref[...]
        if final_norm:
            y = _rms(y, fg_ref[...])
        o_ref[0] = y


def _moe(x, h, cmb, mods, mod_row, w_gu, w_down, final_gain, tm, tf):
    b_, t_, _ = x.shape
    ne, _, f2 = w_gu.shape
    fe = f2 // 2
    assert fe % tf == 0 and tm % (MOE_GROUP * MOE_TILE) == 0 and ne <= SUBLANES
    nf = fe // tf
    nj = t_ // tm
    counts = jnp.sum((cmb[..., :ne] > 0.0).reshape(b_, nj, tm, ne), axis=2, dtype=jnp.int32).reshape(-1)
    tok = lambda wd: pl.BlockSpec((1, tm, wd), lambda b, j, e, f, cnt: (b, j, 0))
    final_norm = final_gain is not None
    fg = (final_gain if final_norm else jnp.ones((D_MODEL,), F32)).reshape(1, D_MODEL)
    grid_spec = pltpu.PrefetchScalarGridSpec(
        num_scalar_prefetch=1,
        grid=(b_, nj, ne, nf),
        in_specs=[tok(D_MODEL), tok(D_MODEL), tok(LANES),
                  pl.BlockSpec((1, SUBLANES, D_MODEL), lambda b, j, e, f, cnt: (mod_row(b), 0, 0)),
                  pl.BlockSpec((1, D_MODEL, tf), lambda b, j, e, f, cnt: (e, 0, f)),
                  pl.BlockSpec((1, D_MODEL, tf), lambda b, j, e, f, cnt: (e, 0, nf + f)),
                  pl.BlockSpec((1, tf, D_MODEL), lambda b, j, e, f, cnt: (e, f, 0)),
                  pl.BlockSpec((1, D_MODEL), lambda b, j, e, f, cnt: (0, 0))],
        out_specs=tok(D_MODEL),
        scratch_shapes=[pltpu.VMEM((tm, D_MODEL), F32),
                        pltpu.VMEM((tm, D_MODEL), BF16),
                        pltpu.VMEM((tm, D_MODEL), F32),
                        pltpu.VMEM((tm, LANES), F32),
                        pltpu.VMEM((LANES, tm), F32)])
    return pl.pallas_call(
        functools.partial(_moe_kernel, final_norm=final_norm),
        grid_spec=grid_spec,
        out_shape=jax.ShapeDtypeStruct((b_, t_, D_MODEL), F32),
        compiler_params=_cparams(("parallel", "parallel", "arbitrary", "arbitrary")),
        name="moe",
    )(counts, x, h, cmb, mods, w_gu, w_gu, w_down, fg)


def _pack_w_in(w):
    offs = np.cumsum((0, QKV_W, DN_WIDTH, 2 * DN_HEADS, 2 * DN_HEADS, GLA_K_WIDTH, GLA_K_WIDTH, GLA_V_WIDTH,
                      GLA_V_WIDTH, 2 * GLA_GATE_RANK))
    seg = lambda i: w[:, offs[i]:offs[i + 1]]
    small = jnp.concatenate([seg(2), seg(3), seg(8)], axis=1)
    small = jnp.pad(small, ((0, 0), (0, SMALL_W - small.shape[1])))
    return jnp.concatenate([seg(0), seg(1), seg(4), seg(5), seg(6), seg(7), small], axis=1).astype(BF16)


def _pack_gate(w_gate2, b_gate):
    ws = []
    for d in range(2):
        lo = GLR_LANE + d * GLA_GATE_RANK
        ws.append(jnp.zeros((SMALL_W, GLA_K_WIDTH), F32).at[lo:lo + GLA_GATE_RANK].set(w_gate2[d]))
    return jnp.stack(ws), b_gate.reshape(2, 1, GLA_K_WIDTH)


def _pack_dn_params(a_log, dt_bias):
    neg_a = jnp.zeros((SMALL_W,), F32).at[GDEC_LANE:GLR_LANE].set(-jnp.exp(a_log.reshape(-1)))
    dtb = jnp.zeros((SMALL_W,), F32).at[GDEC_LANE:GLR_LANE].set(dt_bias.reshape(-1))
    return jnp.zeros((SUBLANES, SMALL_W), F32).at[0].set(neg_a).at[1].set(dtb)


def _pad_rows(a, rows):
    return jnp.pad(a, ((0, rows - a.shape[0]),) + ((0, 0),) * (a.ndim - 1))


def _mixers(x_seq, mods, mod_row, gain_mix, w_in_p, cw, ap, wgs, bgs, states, ncol, tm, tb):
    b_, t_, _ = x_seq.shape
    qkv, dnz, gqk, gv, gz, sm = _proj(x_seq, mods, mod_row, gain_mix, w_in_p, tm, ncol)
    q, k, v, p, p_col = _prep(qkv, sm, cw, ap, tm, ncol)
    rows = t_ // ncol
    o_dn_f, o_dn_b, s_dn = _dn_scan(q, k, v, p, states[0], tb)
    o_gl_f, o_gl_b, s_gl = _gla_scan(gqk, gv, p_col, wgs, bgs, states[1], ncol, min(rows, 2 * CHUNK))
    outs = [(o_dn_f, o_gl_f), (o_dn_b, o_gl_b)]
    return outs, dnz, gz, (s_dn, s_gl)


def kernel(x, c, ctx, c_ctx, w_ada, b_ada, norm_mix, norm_ffn, w_in, conv_qkv, dn_a_log, dn_dt_bias, dn_norm,
           gla_w_gate2, gla_b_gate, gla_norm, w_out, ffn_w_gu, ffn_w_down, moe_router, moe_w_gu, moe_w_down,
           final_norm):
    b_, seq, _ = x.shape
    depth = w_ada.shape[0]
    ctx_len = ctx.shape[1]
    mod_rows = -(-(b_ + 1) // SUBLANES) * SUBLANES
    c_rows = _pad_rows(jnp.concatenate([c, c_ctx[None, :]], axis=0), mod_rows)
    lat_row = lambda b: b
    ctx_row = lambda b: b_
    tm_lat, tm_ctx = 512, ctx_len
    tb = 4 * CHUNK

    for i in range(depth):
        last = i == depth - 1
        mods = _ada(c_rows, w_ada[i], b_ada[i]).reshape(mod_rows, N_MOD, D_MODEL)
        mods = jnp.pad(mods, ((0, 0), (0, SUBLANES - N_MOD), (0, 0)))
        w_in_p = _pack_w_in(w_in[i])
        cw = _pad_rows(conv_qkv[i], SUBLANES)
        ap = _pack_dn_params(dn_a_log[i], dn_dt_bias[i])
        wgs, bgs = _pack_gate(gla_w_gate2[i], gla_b_gate[i])
        s0 = (jnp.zeros((b_, 2, DN_HEADS, DN_HEAD_DIM, DN_HEAD_DIM), F32),
              jnp.zeros((b_, 2, GLA_HEADS, GLA_DV, GLA_DK), F32))
        outs_c, dnz_c, gz_c, s_ctx = _mixers(ctx, mods, ctx_row, norm_mix[i], w_in_p, cw, ap, wgs, bgs, s0, 1,
                                             tm_ctx, tb)
        outs_l, dnz_l, gz_l, _ = _mixers(x, mods, lat_row, norm_mix[i], w_in_p, cw, ap, wgs, bgs, s_ctx, GRID_W,
                                         tm_lat, tb)
        w_out_b = w_out[i].astype(BF16)
        is_moe = i % 2 == 1
        if is_moe:
            router = jnp.pad(moe_router[i // 2], ((0, 0), (0, LANES - N_EXPERTS)))
            w_gu = moe_w_gu[i // 2].astype(BF16)
            w_dn = moe_w_down[i // 2].astype(BF16)
        else:
            router = None
            w_gu = ffn_w_gu[i // 2].astype(BF16)
            w_dn = ffn_w_down[i // 2].astype(BF16)

        def channel(tokens, outs, dnz, gz, row, tm, ncol, final_gain):
            res = _mixout(tokens, outs[0][0], outs[1][0], dnz, outs[0][1], outs[1][1], gz, mods, row, dn_norm[i],
                          gla_norm[i], w_out_b, norm_ffn[i], router, tm, ncol)
            if is_moe:
                x_mid, h, cmb = res
                tm_moe = min(1024, tokens.shape[1])
                return _moe(x_mid, h, cmb, mods, row, w_gu, w_dn, final_gain, tm_moe, D_EXPERT_TILE)
            x_mid, h = res
            y = _ffn(x_mid, h, mods, row, w_gu, w_dn, tm)
            if final_gain is not None:
                raise NotImplementedError("final norm is fused into the expert mixer only")
            return y

        x = channel(x, outs_l, dnz_l, gz_l, lat_row, tm_lat, GRID_W, final_norm if last else None)
        if not last:
            ctx = channel(ctx, outs_c, dnz_c, gz_c, ctx_row, tm_ctx, 1, None)
    return x
```

```python
import functools

import jax
import jax.numpy as jnp
import numpy as np
from jax import lax
from jax.experimental import pallas as pl
from jax.experimental.pallas import tpu as pltpu

F32 = jnp.float32
BF16 = jnp.bfloat16
HIGHEST = lax.Precision.HIGHEST

D_MODEL = 1024
GRID_W = 64
DN_HEADS = 4
DN_HEAD_DIM = 128
DN_WIDTH = DN_HEADS * DN_HEAD_DIM
CONV_W = 5
GLA_HEADS = 4
GLA_DK = 64
GLA_DV = 128
GLA_K_WIDTH = GLA_HEADS * GLA_DK
GLA_V_WIDTH = GLA_HEADS * GLA_DV
GLA_GATE_RANK = 16
GLA_GATE_NORMALIZER = 16.0
N_EXPERTS = 8
N_MOD = 6
NORM_EPS = 1e-6

LANES = 128
SUBLANES = 8
CHUNK = 64
SUB = 16
EXP_CLAMP = 80.0
D_EXPERT_TILE = 1792
MOE_TILE = 64
MOE_GROUP = 8
GLA_COLS_PER_STEP = 8
VMEM_LIMIT = 56 * 1024 * 1024

QKV_W = 3 * DN_WIDTH
SMALL_W = LANES
PROJ_WIDTHS = (QKV_W, DN_WIDTH, 2 * GLA_K_WIDTH, GLA_V_WIDTH, GLA_V_WIDTH, SMALL_W)
PROJ_COLS = sum(PROJ_WIDTHS)
BETA_LANE = 0
GDEC_LANE = 2 * DN_HEADS
GLR_LANE = 4 * DN_HEADS


def _cparams(sem, vmem=VMEM_LIMIT):
    return pltpu.CompilerParams(dimension_semantics=sem, vmem_limit_bytes=vmem)


def _sigmoid(t):
    return 1.0 / (1.0 + jnp.exp(-t))


def _silu(t):
    return t * _sigmoid(t)


def _softplus(t):
    return jnp.maximum(t, 0.0) + jnp.log(1.0 + jnp.exp(-jnp.abs(t)))


def _bdot(a, b):
    return jnp.dot(a.astype(BF16), b.astype(BF16), preferred_element_type=F32)


def _bdot_nt(a, b):
    return lax.dot_general(a.astype(BF16), b.astype(BF16), (((1,), (1,)), ((), ())), preferred_element_type=F32)


def _bdot_tn(a, b):
    return lax.dot_general(a.astype(BF16), b.astype(BF16), (((0,), (0,)), ((), ())), preferred_element_type=F32)


def _split_bf16(t, terms):
    parts = []
    for _ in range(terms):
        part = t.astype(BF16)
        parts.append(part)
        t = t - part.astype(F32)
    return parts


def _dot_exact_lhs(a, x, terms=2):
    a = a.astype(BF16)
    out = None
    for part in _split_bf16(x, terms):
        d = jnp.dot(a, part, preferred_element_type=F32)
        out = d if out is None else out + d
    return out


def _dot_split(a, b):
    a_hi, a_lo = _split_bf16(a, 2)
    b_hi, b_lo = _split_bf16(b, 2)
    dot = lambda u, v: jnp.dot(u, v, preferred_element_type=F32)
    return dot(a_hi, b_hi) + (dot(a_lo, b_hi) + dot(a_hi, b_lo))


def _rms(t, gain):
    return t * lax.rsqrt(jnp.mean(t * t, axis=-1, keepdims=True) + NORM_EPS) * gain


def _ada_kernel(c_ref, w_ref, b_ref, o_ref):
    s = _silu(c_ref[...])
    o_ref[...] = jnp.dot(s, w_ref[...], precision=HIGHEST, preferred_element_type=F32) + b_ref[...]


def _ada(c_rows, w, b):
    rows = c_rows.shape[0]
    n = w.shape[1]
    tn = D_MODEL
    return pl.pallas_call(
        _ada_kernel,
        grid=(n // tn,),
        in_specs=[pl.BlockSpec((rows, D_MODEL), lambda j: (0, 0)),
                  pl.BlockSpec((D_MODEL, tn), lambda j: (0, j)),
                  pl.BlockSpec((1, tn), lambda j: (0, j))],
        out_specs=pl.BlockSpec((rows, tn), lambda j: (0, j)),
        out_shape=jax.ShapeDtypeStruct((rows, n), F32),
        compiler_params=_cparams(("arbitrary",)),
        name="ada",
    )(c_rows, w, b.reshape(1, n))


def _to_columns(val, out_ref, scr_ref, ncol):
    nrows, width = val.shape
    groups = width // LANES
    grid_rows = nrows // ncol
    pitch = _col_pitch(ncol)
    for g in range(groups):
        for r in range(grid_rows):
            scr_ref[g, r * pitch:r * pitch + ncol, :] = val[r * ncol:(r + 1) * ncol, g * LANES:(g + 1) * LANES]
    for c in range(ncol):
        for g in range(groups):
            lo = c * width + g * LANES
            out_ref[0, :, lo:lo + LANES] = scr_ref[g, pl.ds(c, grid_rows, stride=pitch), :]


def _from_columns(in_val, scr_ref, ncol):
    grid_rows = in_val.shape[0]
    width = in_val.shape[1] // ncol
    groups = width // LANES
    pitch = _col_pitch(ncol)
    for c in range(ncol):
        for g in range(groups):
            lo = c * width + g * LANES
            scr_ref[g, pl.ds(c, grid_rows, stride=pitch), :] = in_val[:, lo:lo + LANES]
    return jnp.concatenate(
        [jnp.concatenate([scr_ref[g, r * pitch:r * pitch + ncol, :] for r in range(grid_rows)], axis=0)
         for g in range(groups)], axis=1)


def _col_pitch(ncol):
    return ncol + SUBLANES // 2


def _col_scratch(groups, grid_rows, ncol):
    rows = -(-grid_rows * _col_pitch(ncol) // SUBLANES) * SUBLANES
    return pltpu.VMEM((groups, rows, LANES), F32)


def _proj_kernel(x_ref, mod_ref, gain_ref, w_ref, qkv_ref, dnz_ref, gqk_ref, gv_ref, gz_ref, sm_ref, *scr, ncol):
    x = x_ref[0]
    y = _rms(x, gain_ref[...])
    h = (y * (1.0 + mod_ref[0, 1:2, :]) + mod_ref[0, 0:1, :]).astype(BF16)
    r = jnp.dot(h, w_ref[...], preferred_element_type=F32)
    o = 0
    for ref, wd in zip((qkv_ref, dnz_ref, gqk_ref, gv_ref, gz_ref, sm_ref), PROJ_WIDTHS):
        if ncol > 1 and (ref is gqk_ref or ref is gv_ref):
            _to_columns(r[:, o:o + wd], ref, scr[0], ncol)
        else:
            ref[0] = r[:, o:o + wd]
        o += wd


def _proj(x, mods, mod_row, gain, w, tm, ncol):
    b_, t_, _ = x.shape
    tok = lambda wd: pl.BlockSpec((1, tm, wd), lambda b, j: (b, j, 0))
    colm = lambda wd: pl.BlockSpec((1, tm // ncol, ncol * wd), lambda b, j: (b, j, 0))
    col_major = (False, False, ncol > 1, ncol > 1, False, False)
    scratch = [_col_scratch(GLA_V_WIDTH // LANES, tm // ncol, ncol)] if ncol > 1 else []
    return pl.pallas_call(
        functools.partial(_proj_kernel, ncol=ncol),
        grid=(b_, t_ // tm),
        in_specs=[tok(D_MODEL),
                  pl.BlockSpec((1, SUBLANES, D_MODEL), lambda b, j: (mod_row(b), 0, 0)),
                  pl.BlockSpec((1, D_MODEL), lambda b, j: (0, 0)),
                  pl.BlockSpec((D_MODEL, PROJ_COLS), lambda b, j: (0, 0))],
        out_specs=[colm(wd) if cm else tok(wd) for wd, cm in zip(PROJ_WIDTHS, col_major)],
        out_shape=[jax.ShapeDtypeStruct((b_, t_ // ncol, ncol * wd) if cm else (b_, t_, wd), F32)
                   for wd, cm in zip(PROJ_WIDTHS, col_major)],
        scratch_shapes=scratch,
        compiler_params=_cparams(("parallel", "parallel")),
        name="proj",
    )(x, mods, gain.reshape(1, D_MODEL), w)


def _prep_kernel(cur_ref, prev_ref, next_ref, sm_ref, cw_ref, ap_ref, q_ref, k_ref, v_ref, p_ref, *rest, ncol):
    if ncol > 1:
        pcol_ref, ext_ref, col_scr = rest
    else:
        (ext_ref,) = rest
    j = pl.program_id(1)
    nj = pl.num_programs(1)
    tl = cur_ref.shape[1]
    halo = SUBLANES
    pad = CONV_W // 2
    ext_ref[0:halo, :] = jnp.where(j > 0, prev_ref[0], 0.0)
    ext_ref[halo:halo + tl, :] = cur_ref[0]
    ext_ref[halo + tl:2 * halo + tl, :] = jnp.where(j < nj - 1, next_ref[0], 0.0)
    acc = None
    for w in range(CONV_W):
        term = ext_ref[pl.ds(halo - pad + w, tl), :] * cw_ref[w:w + 1, :]
        acc = term if acc is None else acc + term
    s = _silu(acc)
    for h in range(DN_HEADS):
        lo = h * DN_HEAD_DIM
        qh = s[:, lo:lo + DN_HEAD_DIM]
        kh = s[:, DN_WIDTH + lo:DN_WIDTH + lo + DN_HEAD_DIM]
        q_ref[0, :, lo:lo + DN_HEAD_DIM] = (qh * lax.rsqrt(jnp.sum(qh * qh, axis=-1, keepdims=True) + NORM_EPS)
                                            * DN_HEAD_DIM ** -0.5)
        k_ref[0, :, lo:lo + DN_HEAD_DIM] = kh * lax.rsqrt(jnp.sum(kh * kh, axis=-1, keepdims=True) + NORM_EPS)
    v_ref[0] = s[:, 2 * DN_WIDTH:]
    sm = sm_ref[0]
    lane = lax.broadcasted_iota(jnp.int32, sm.shape, 1)
    beta = _sigmoid(sm)
    gdec = ap_ref[0:1, :] * _softplus(sm + ap_ref[1:2, :])
    p = jnp.where(lane < GDEC_LANE, beta, jnp.where(lane < GLR_LANE, gdec, sm))
    p_ref[0] = p
    if ncol > 1:
        _to_columns(p, pcol_ref, col_scr, ncol)


def _prep(qkv, sm, cw, ap, tl, ncol):
    b_, t_, _ = qkv.shape
    nb = t_ // tl
    r8 = tl // SUBLANES
    last8 = t_ // SUBLANES - 1
    tok = lambda wd: pl.BlockSpec((1, tl, wd), lambda b, j: (b, j, 0))
    out_specs = [tok(DN_WIDTH), tok(DN_WIDTH), tok(DN_WIDTH), tok(SMALL_W)]
    out_shape = [jax.ShapeDtypeStruct((b_, t_, DN_WIDTH), F32)] * 3 + [jax.ShapeDtypeStruct((b_, t_, SMALL_W), F32)]
    scratch = [pltpu.VMEM((tl + 2 * SUBLANES, QKV_W), F32)]
    if ncol > 1:
        out_specs.append(pl.BlockSpec((1, tl // ncol, ncol * SMALL_W), lambda b, j: (b, j, 0)))
        out_shape.append(jax.ShapeDtypeStruct((b_, t_ // ncol, ncol * SMALL_W), F32))
        scratch.append(_col_scratch(1, tl // ncol, ncol))
    outs = pl.pallas_call(
        functools.partial(_prep_kernel, ncol=ncol),
        grid=(b_, nb),
        in_specs=[tok(QKV_W),
                  pl.BlockSpec((1, SUBLANES, QKV_W), lambda b, j: (b, jnp.maximum(j * r8 - 1, 0), 0)),
                  pl.BlockSpec((1, SUBLANES, QKV_W), lambda b, j: (b, jnp.minimum((j + 1) * r8, last8), 0)),
                  tok(SMALL_W),
                  pl.BlockSpec((SUBLANES, QKV_W), lambda b, j: (0, 0)),
                  pl.BlockSpec((SUBLANES, SMALL_W), lambda b, j: (0, 0))],
        out_specs=out_specs,
        out_shape=out_shape,
        scratch_shapes=scratch,
        compiler_params=_cparams(("parallel", "parallel")),
        name="prep",
    )(qkv, qkv, qkv, sm, cw, ap)
    return outs if ncol > 1 else list(outs) + [outs[3]]


def _order_masks(n, rev):
    ri = lax.broadcasted_iota(jnp.int32, (n, n), 0)
    ci = lax.broadcasted_iota(jnp.int32, (n, n), 1)
    if rev:
        return ri, ci, ci >= ri, ci > ri
    return ri, ci, ci <= ri, ci < ri


def _dn_kernel(qf_ref, kf_ref, vf_ref, pf_ref, qb_ref, kb_ref, vb_ref, pb_ref, s0_ref, of_ref, ob_ref, st_ref, s_scr):
    j = pl.program_id(1)
    nj = pl.num_programs(1)

    @pl.when(j == 0)
    def _():
        s_scr[...] = s0_ref[0]

    c_ = CHUNK
    tb = qf_ref.shape[1]
    nchunk = tb // c_
    streams = ((qf_ref, kf_ref, vf_ref, pf_ref, of_ref), (qb_ref, kb_ref, vb_ref, pb_ref, ob_ref))
    masks, gcum, gcum_t, order = [], [], [], []
    for d, rev in enumerate((False, True)):
        ri, ci, incl, strict = _order_masks(c_, rev)
        levels = []
        s = 1
        while s < c_:
            levels.append(strict & ((ri // (2 * s)) == (ci // (2 * s))) & ((ri // s) != (ci // s)))
            s *= 2
        masks.append(dict(incl=incl, strict=strict, eye=(ri == ci).astype(F32), levels=levels,
                          last=0 if rev else c_ - 1))
        rb, cb, incl_b, _ = _order_masks(tb, rev)
        cum_mat = incl_b & ((rb // c_) == (cb // c_))
        gcum.append(_dot_exact_lhs(cum_mat, streams[d][3][0], terms=3))
        gcum_t.append(gcum[d].T)
        order.append([nchunk - 1 - cc if rev else cc for cc in range(nchunk)])

    inst = [(d, c, h) for step in range(nchunk) for d in range(2) for c in (order[d][step],)
            for h in range(DN_HEADS)]
    pre = {}
    for d, c, h in inst:
        q_ref, k_ref, v_ref, p_ref, _ = streams[d]
        m = masks[d]
        r0 = c * c_
        lo = h * DN_HEAD_DIM
        lb = BETA_LANE + d * DN_HEADS + h
        lg = GDEC_LANE + d * DN_HEADS + h
        beta = p_ref[0, r0:r0 + c_, lb:lb + 1]
        gc = gcum[d][r0:r0 + c_, lg:lg + 1]
        gr = gcum_t[d][lg:lg + 1, r0:r0 + c_]
        glast = gcum[d][r0 + m["last"]:r0 + m["last"] + 1, lg:lg + 1]
        decay = jnp.exp(jnp.where(m["incl"], gc - gr, -jnp.inf))
        qh = q_ref[0, r0:r0 + c_, lo:lo + DN_HEAD_DIM]
        kh = k_ref[0, r0:r0 + c_, lo:lo + DN_HEAD_DIM]
        vh = v_ref[0, r0:r0 + c_, lo:lo + DN_HEAD_DIM]
        kb = kh * beta
        eg = jnp.exp(gc)
        pre[d, c, h] = dict(decay=decay, kq_lhs=jnp.concatenate([kb, qh], axis=0), kh=kh,
                            rhs=jnp.concatenate([vh * beta, kb * eg], axis=1), q_dec=qh * eg,
                            k_dec=kh * jnp.exp(glast - gc), e_last=jnp.exp(glast))
    kq = {i: _bdot_nt(pre[i]["kq_lhs"], pre[i]["kh"]) for i in inst}
    a = {i: jnp.where(masks[i[0]]["strict"], kq[i][:c_] * pre[i]["decay"], 0.0) for i in inst}
    attn = {i: kq[i][c_:] * pre[i]["decay"] for i in inst}
    t = {i: masks[i[0]]["eye"] - jnp.where(masks[i[0]]["levels"][0], a[i], 0.0) for i in inst}
    for lvl in range(1, len(masks[0]["levels"])):
        ta = {i: _bdot(t[i], jnp.where(masks[i[0]]["levels"][lvl], a[i], 0.0)) for i in inst}
        t = {i: t[i] - _bdot(ta[i], t[i]) for i in inst}
    uw = {i: _bdot(t[i], pre[i]["rhs"]) for i in inst}
    for step in range(nchunk):
        group = [(d, order[d][step], h) for d in range(2) for h in range(DN_HEADS)]
        st = {i: s_scr[i[0], i[2]] for i in group}
        ws = {i: _bdot(jnp.concatenate([uw[i][:, DN_HEAD_DIM:], pre[i]["q_dec"]], axis=0), st[i]) for i in group}
        v_new = {i: uw[i][:, :DN_HEAD_DIM] - ws[i][:c_] for i in group}
        for i in group:
            d, c, h = i
            streams[d][4][0, c * c_:(c + 1) * c_, h * DN_HEAD_DIM:(h + 1) * DN_HEAD_DIM] = (
                ws[i][c_:] + _bdot(attn[i], v_new[i]))
        for i in group:
            s_scr[i[0], i[2]] = st[i] * pre[i]["e_last"] + _bdot_tn(pre[i]["k_dec"], v_new[i])

    @pl.when(j == nj - 1)
    def _():
        st_ref[0] = s_scr[...]


def _dn_scan(q, k, v, p, s0, tb):
    b_, t_, _ = q.shape
    nb = t_ // tb
    fwd = lambda wd: pl.BlockSpec((1, tb, wd), lambda b, j: (b, j, 0))
    bwd = lambda wd: pl.BlockSpec((1, tb, wd), lambda b, j: (b, nb - 1 - j, 0))
    widths = (DN_WIDTH, DN_WIDTH, DN_WIDTH, SMALL_W)
    st_shape = (2, DN_HEADS, DN_HEAD_DIM, DN_HEAD_DIM)
    st_spec = pl.BlockSpec((1,) + st_shape, lambda b, j: (b, 0, 0, 0, 0))
    return pl.pallas_call(
        _dn_kernel,
        grid=(b_, nb),
        in_specs=[fwd(w) for w in widths] + [bwd(w) for w in widths] + [st_spec],
        out_specs=[fwd(DN_WIDTH), bwd(DN_WIDTH), st_spec],
        out_shape=[jax.ShapeDtypeStruct((b_, t_, DN_WIDTH), F32)] * 2 + [jax.ShapeDtypeStruct((b_,) + st_shape, F32)],
        scratch_shapes=[pltpu.VMEM(st_shape, F32)],
        compiler_params=_cparams(("parallel", "arbitrary")),
        name="dn",
    )(q, k, v, p, q, k, v, p, s0)


def _log_sigmoid(t):
    return jnp.minimum(t, 0.0) - jnp.log(1.0 + jnp.exp(-jnp.abs(t)))


def _gla_kernel(qkf_ref, vf_ref, pf_ref, qkb_ref, vb_ref, pb_ref, wg_ref, bg_ref, s0_ref, of_ref, ob_ref, st_ref,
                s_scr):
    col = pl.program_id(1)
    rb = pl.program_id(2)
    ncol = pl.num_programs(1)
    nrb = pl.num_programs(2)

    @pl.when((col == 0) & (rb == 0))
    def _():
        s_scr[...] = s0_ref[0]

    c_ = CHUNK
    nchunk = qkf_ref.shape[1] // c_
    ncb = pf_ref.shape[2] // SMALL_W
    nsub = c_ // SUB
    streams = ((qkf_ref, vf_ref, pf_ref, of_ref), (qkb_ref, vb_ref, pb_ref, ob_ref))
    masks, order = [], []
    for rev in (False, True):
        ri, ci, incl, _ = _order_masks(c_, rev)
        pos_r = (c_ - 1 - ri) if rev else ri
        pos_c = (c_ - 1 - ci) if rev else ci
        first_rows = [(c_ - 1 - SUB * i) if rev else SUB * i for i in range(nsub)]
        masks.append(dict(incl=incl, diag=incl & ((pos_r // SUB) == (pos_c // SUB)),
                          off=(pos_c // SUB) < (pos_r // SUB), last=0 if rev else c_ - 1, first_rows=first_rows,
                          rev=rev))
        cols = [ncb - 1 - i if rev else i for i in range(ncb)]
        chunks = [nchunk - 1 - cc if rev else cc for cc in range(nchunk)]
        order.append([(cb, c) for cb in cols for c in chunks])
    nstep = len(order[0])
    la_col = {}
    for d in range(2):
        for cb in range(ncb):
            gate = _dot_split(streams[d][2][0, :, cb * SMALL_W:(cb + 1) * SMALL_W], wg_ref[d]) + bg_ref[d]
            la_col[d, cb] = _log_sigmoid(gate) * (1.0 / GLA_GATE_NORMALIZER)
    pre = {}
    for step in range(nstep):
        for d in range(2):
            cb, c = order[d][step]
            qk_ref, v_ref, _, _ = streams[d]
            m = masks[d]
            r0 = c * c_
            la = la_col[d, cb][r0:r0 + c_, :]
            bc = _dot_exact_lhs(m["incl"], la, terms=3)
            bex = bc - la
            b_last = bc[m["last"]:m["last"] + 1, :]
            pieces = [jnp.broadcast_to(bex[fr:fr + 1, :], (SUB, GLA_K_WIDTH)) for fr in m["first_rows"]]
            refd = jnp.concatenate(pieces[::-1] if m["rev"] else pieces, axis=0)
            mid = m["first_rows"][nsub // 2]
            rmid = bex[mid:mid + 1, :]
            qk0 = cb * 2 * GLA_K_WIDTH
            qa = qk_ref[0, r0:r0 + c_, qk0:qk0 + GLA_K_WIDTH] * GLA_DK ** -0.5
            ka = qk_ref[0, r0:r0 + c_, qk0 + GLA_K_WIDTH:qk0 + 2 * GLA_K_WIDTH]
            q_in = qa * jnp.exp(bc)
            k_st = ka * jnp.exp(b_last - bc)
            q_d = qa * jnp.exp(bc - refd)
            k_d = ka * jnp.exp(jnp.minimum(refd - bc, EXP_CLAMP))
            q_o = qa * jnp.exp(jnp.minimum(bc - rmid, EXP_CLAMP))
            k_o = ka * jnp.exp(jnp.minimum(rmid - bc, EXP_CLAMP))
            e_last = jnp.exp(b_last)
            v0 = cb * GLA_V_WIDTH
            for h in range(GLA_HEADS):
                ks = slice(h * GLA_DK, (h + 1) * GLA_DK)
                vh = v_ref[0, r0:r0 + c_, v0 + h * GLA_DV:v0 + (h + 1) * GLA_DV]
                attn = (jnp.where(m["diag"], _bdot_nt(q_d[:, ks], k_d[:, ks]), 0.0)
                        + jnp.where(m["off"], _bdot_nt(q_o[:, ks], k_o[:, ks]), 0.0))
                pre[d, step, h] = dict(attn=attn, vh=vh, q_in=q_in[:, ks], e_last=e_last[:, ks],
                                       kv=_bdot_tn(vh, k_st[:, ks]))
    st = {(d, h): s_scr[d, h] for d in range(2) for h in range(GLA_HEADS)}
    for step in range(nstep):
        for d in range(2):
            cb, c = order[d][step]
            r0 = c * c_
            v0 = cb * GLA_V_WIDTH
            for h in range(GLA_HEADS):
                i = (d, step, h)
                streams[d][3][0, r0:r0 + c_, v0 + h * GLA_DV:v0 + (h + 1) * GLA_DV] = (
                    _bdot_nt(pre[i]["q_in"], st[d, h]) + _bdot(pre[i]["attn"], pre[i]["vh"]))
                st[d, h] = st[d, h] * pre[i]["e_last"] + pre[i]["kv"]
    for d in range(2):
        for h in range(GLA_HEADS):
            s_scr[d, h] = st[d, h]

    @pl.when((col == ncol - 1) & (rb == nrb - 1))
    def _():
        st_ref[0] = s_scr[...]


def _gla_scan(qk, v, p, wg, bg, s0, ncol, tr):
    b_, rows, _ = qk.shape
    nrb = rows // tr
    ncb = min(GLA_COLS_PER_STEP, ncol)
    ncblk = ncol // ncb
    assert ncb == 1 or nrb == 1
    fwd = lambda wd: pl.BlockSpec((1, tr, ncb * wd), lambda b, c, r: (b, r, c))
    bwd = lambda wd: pl.BlockSpec((1, tr, ncb * wd), lambda b, c, r: (b, nrb - 1 - r, ncblk - 1 - c))
    widths = (2 * GLA_K_WIDTH, GLA_V_WIDTH, SMALL_W)
    st_shape = (2, GLA_HEADS, GLA_DV, GLA_DK)
    st_spec = pl.BlockSpec((1,) + st_shape, lambda b, c, r: (b, 0, 0, 0, 0))
    return pl.pallas_call(
        _gla_kernel,
        grid=(b_, ncblk, nrb),
        in_specs=[fwd(w) for w in widths] + [bwd(w) for w in widths] + [
            pl.BlockSpec((2, SMALL_W, GLA_K_WIDTH), lambda b, c, r: (0, 0, 0)),
            pl.BlockSpec((2, 1, GLA_K_WIDTH), lambda b, c, r: (0, 0, 0)),
            st_spec],
        out_specs=[fwd(GLA_V_WIDTH), bwd(GLA_V_WIDTH), st_spec],
        out_shape=[jax.ShapeDtypeStruct((b_, rows, ncol * GLA_V_WIDTH), F32)] * 2 + [
            jax.ShapeDtypeStruct((b_,) + st_shape, F32)],
        scratch_shapes=[pltpu.VMEM(st_shape, F32)],
        compiler_params=_cparams(("parallel", "arbitrary", "arbitrary")),
        name="gla",
    )(qk, v, p, qk, v, p, wg, bg, s0)


def _head_norm_gate(o, z, gain):
    parts = []
    for h in range(o.shape[1] // LANES):
        hs = slice(h * LANES, (h + 1) * LANES)
        parts.append(_rms(o[:, hs], gain) * _silu(z[:, hs]))
    return jnp.concatenate(parts, axis=1)


def _mixout_kernel(*refs, with_router, ncol):
    (x_ref, odf_ref, odb_ref, dnz_ref, ogf_ref, ogb_ref, glz_ref, mod_ref, gd_ref, gg_ref, w_ref, gf_ref) = refs[:12]
    refs = list(refs[12:])
    col_scr = refs.pop() if ncol > 1 else None
    if with_router:
        rw_ref, xo_ref, h_ref, cmb_ref = refs
    else:
        xo_ref, h_ref = refs
    m_dn = _head_norm_gate(odf_ref[0] + odb_ref[0], dnz_ref[0], gd_ref[...])
    o_gl = ogf_ref[0] + ogb_ref[0]
    if ncol > 1:
        o_gl = _from_columns(o_gl, col_scr, ncol)
    m_gl = _head_norm_gate(o_gl, glz_ref[0], gg_ref[...])
    m = jnp.concatenate([m_dn, m_gl], axis=1).astype(BF16)
    x = x_ref[0] + mod_ref[0, 2:3, :] * jnp.dot(m, w_ref[...], preferred_element_type=F32)
    xo_ref[0] = x
    h = _rms(x, gf_ref[...]) * (1.0 + mod_ref[0, 4:5, :]) + mod_ref[0, 3:4, :]
    h_ref[0] = h.astype(BF16)
    if with_router:
        logits = _dot_split(h, rw_ref[...])
        lane = lax.broadcasted_iota(jnp.int32, logits.shape, 1).astype(F32)
        logits = jnp.where(lane < N_EXPERTS, logits, -jnp.inf)
        e = jnp.exp(logits - jnp.max(logits, axis=-1, keepdims=True))
        probs = e / jnp.sum(e, axis=-1, keepdims=True)
        p1 = jnp.max(probs, axis=-1, keepdims=True)
        i1 = jnp.min(jnp.where(probs == p1, lane, float(LANES)), axis=-1, keepdims=True)
        rest = jnp.where(lane == i1, -1.0, probs)
        p2 = jnp.max(rest, axis=-1, keepdims=True)
        i2 = jnp.min(jnp.where(rest == p2, lane, float(LANES)), axis=-1, keepdims=True)
        cmb_ref[0] = (jnp.where(lane == i1, p1, 0.0) + jnp.where(lane == i2, p2, 0.0)) / (p1 + p2)


def _mixout(x, odf, odb, dnz, ogf, ogb, glz, mods, mod_row, gd, gg, w_out, gf, router, tm, ncol):
    b_, t_, _ = x.shape
    with_router = router is not None
    tok = lambda wd: pl.BlockSpec((1, tm, wd), lambda b, j: (b, j, 0))
    colm = lambda wd: pl.BlockSpec((1, tm // ncol, ncol * wd), lambda b, j: (b, j, 0))
    const = lambda shp: pl.BlockSpec(shp, lambda b, j: (0,) * len(shp))
    in_specs = [tok(D_MODEL)] + [tok(DN_WIDTH)] * 3 + [colm(GLA_V_WIDTH)] * 2 + [tok(GLA_V_WIDTH)] + [
        pl.BlockSpec((1, SUBLANES, D_MODEL), lambda b, j: (mod_row(b), 0, 0)),
        const((1, LANES)), const((1, LANES)), const((D_MODEL, D_MODEL)), const((1, D_MODEL))]
    args = [x, odf, odb, dnz, ogf, ogb, glz, mods, gd.reshape(1, LANES), gg.reshape(1, LANES), w_out,
            gf.reshape(1, D_MODEL)]
    out_specs = [tok(D_MODEL), tok(D_MODEL)]
    out_shape = [jax.ShapeDtypeStruct((b_, t_, D_MODEL), F32), jax.ShapeDtypeStruct((b_, t_, D_MODEL), BF16)]
    if with_router:
        in_specs.append(const((D_MODEL, LANES)))
        args.append(router)
        out_specs.append(tok(LANES))
        out_shape.append(jax.ShapeDtypeStruct((b_, t_, LANES), F32))
    scratch = [_col_scratch(GLA_V_WIDTH // LANES, tm // ncol, ncol)] if ncol > 1 else []
    return pl.pallas_call(
        functools.partial(_mixout_kernel, with_router=with_router, ncol=ncol),
        grid=(b_, t_ // tm),
        in_specs=in_specs,
        out_specs=out_specs,
        out_shape=out_shape,
        scratch_shapes=scratch,
        compiler_params=_cparams(("parallel", "parallel")),
        name="mixout",
    )(*args)


def _ffn_kernel(x_ref, h_ref, mod_ref, wg_ref, wu_ref, wd_ref, o_ref, *, nsplit):
    h = h_ref[0]
    f = wg_ref.shape[1]
    fs = f // nsplit
    acc = None
    for s in range(nsplit):
        g = jnp.dot(h, wg_ref[:, s * fs:(s + 1) * fs], preferred_element_type=F32)
        u = jnp.dot(h, wu_ref[:, s * fs:(s + 1) * fs], preferred_element_type=F32)
        part = jnp.dot((_silu(g) * u).astype(BF16), wd_ref[s * fs:(s + 1) * fs, :], preferred_element_type=F32)
        acc = part if acc is None else acc + part
    o_ref[0] = x_ref[0] + mod_ref[0, 5:6, :] * acc


def _ffn(x, h, mods, mod_row, w_gu, w_down, tm):
    b_, t_, _ = x.shape
    f = w_down.shape[0]
    tok = lambda: pl.BlockSpec((1, tm, D_MODEL), lambda b, j: (b, j, 0))
    return pl.pallas_call(
        functools.partial(_ffn_kernel, nsplit=2),
        grid=(b_, t_ // tm),
        in_specs=[tok(), tok(),
                  pl.BlockSpec((1, SUBLANES, D_MODEL), lambda b, j: (mod_row(b), 0, 0)),
                  pl.BlockSpec((D_MODEL, f), lambda b, j: (0, 0)),
                  pl.BlockSpec((D_MODEL, f), lambda b, j: (0, 1)),
                  pl.BlockSpec((f, D_MODEL), lambda b, j: (0, 0))],
        out_specs=tok(),
        out_shape=jax.ShapeDtypeStruct((b_, t_, D_MODEL), F32),
        compiler_params=_cparams(("parallel", "parallel")),
        name="ffn",
    )(x, h, mods, w_gu, w_gu, w_down)


def _for_row_tiles(ntiles, fn):
    rows = MOE_GROUP * MOE_TILE
    ngroup = ntiles // MOE_GROUP

    def body(i, carry):
        fn(pl.multiple_of(i * rows, rows), rows)
        return carry

    lax.fori_loop(0, ngroup, body, 0)
    for k in range(1, MOE_GROUP):
        @pl.when(ntiles % MOE_GROUP == k)
        def _():
            fn(pl.multiple_of(ngroup * rows, rows), k * MOE_TILE)


def _moe_kernel(cnt_ref, x_ref, h_ref, cmb_ref, mod_ref, wg_ref, wu_ref, wd_ref, fg_ref, o_ref,
                acc_ref, xe_ref, ye_ref, rkc_ref, rkr_ref, *, final_norm):
    b = pl.program_id(0)
    j = pl.program_id(1)
    e = pl.program_id(2)
    f = pl.program_id(3)
    nj = pl.num_programs(1)
    ne = pl.num_programs(2)
    nf = pl.num_programs(3)
    tm = h_ref.shape[1]
    ntiles = (cnt_ref[(b * nj + j) * ne + e] + MOE_TILE - 1) // MOE_TILE

    @pl.when((e == 0) & (f == 0))
    def _():
        acc_ref[...] = jnp.zeros_like(acc_ref)
        sb = 2 * LANES
        ri = lax.broadcasted_iota(jnp.int32, (sb, sb), 0)
        ci = lax.broadcasted_iota(jnp.int32, (sb, sb), 1)
        before = (ci < ri).astype(BF16)
        carry = jnp.zeros((1, LANES), F32)
        for s in range(tm // sb):
            sel = (cmb_ref[0, s * sb:(s + 1) * sb, :] > 0.0).astype(F32)
            rank = jnp.dot(before, sel.astype(BF16), preferred_element_type=F32) + carry
            rkc_ref[s * sb:(s + 1) * sb, :] = jnp.where(sel > 0.0, rank, -1.0)
            carry = carry + jnp.sum(sel, axis=0, keepdims=True)
        rkr_ref[...] = rkc_ref[...].T

    @pl.when(f == 0)
    def _():
        rank_row = rkr_ref[pl.ds(e, 1), :]

        def gather(r0, n):
            slot = lax.broadcasted_iota(jnp.int32, (n, tm), 0).astype(F32) + r0.astype(F32)
            onehot = (rank_row == slot).astype(BF16)
            xe_ref[pl.ds(r0, n), :] = jnp.dot(onehot, h_ref[0], preferred_element_type=F32).astype(BF16)

        _for_row_tiles(ntiles, gather)

    def expert(r0, n):
        xr = xe_ref[pl.ds(r0, n), :]
        g = jnp.dot(xr, wg_ref[0], preferred_element_type=F32)
        u = jnp.dot(xr, wu_ref[0], preferred_element_type=F32)
        y = jnp.dot((_silu(g) * u).astype(BF16), wd_ref[0], preferred_element_type=F32)

        @pl.when(f == 0)
        def _():
            ye_ref[pl.ds(r0, n), :] = y

        @pl.when(f > 0)
        def _():
            ye_ref[pl.ds(r0, n), :] += y

    _for_row_tiles(ntiles, expert)

    @pl.when(f == nf - 1)
    def _():
        lane = lax.broadcasted_iota(jnp.int32, (tm, LANES), 1)
        rank_col = jnp.sum(jnp.where(lane == e, rkc_ref[...], 0.0), axis=-1, keepdims=True)
        ce = jnp.sum(jnp.where(lane == e, cmb_ref[0], 0.0), axis=-1, keepdims=True)

        def scatter(r0, n):
            slot = lax.broadcasted_iota(jnp.int32, (tm, n), 1).astype(F32) + r0.astype(F32)
            onehot = (rank_col == slot).astype(BF16)
            back = jnp.dot(onehot, ye_ref[pl.ds(r0, n), :].astype(BF16), preferred_element_type=F32)
            acc_ref[...] += ce * back

        _for_row_tiles(ntiles, scatter)

    @pl.when((e == ne - 1) & (f == nf - 1))
    def _():
        y = x_ref[0] + mod_ref[0, 5:6, :] * acc_ref[...]
        if final_norm:
            y = _rms(y, fg_ref[...])
        o_ref[0] = y


def _moe(x, h, cmb, mods, mod_row, w_gu, w_down, final_gain, tm, tf):
    b_, t_, _ = x.shape
    ne, _, f2 = w_gu.shape
    fe = f2 // 2
    assert fe % tf == 0 and tm % (MOE_GROUP * MOE_TILE) == 0 and ne <= SUBLANES
    nf = fe // tf
    nj = t_ // tm
    counts = jnp.sum((cmb[..., :ne] > 0.0).reshape(b_, nj, tm, ne), axis=2, dtype=jnp.int32).reshape(-1)
    tok = lambda wd: pl.BlockSpec((1, tm, wd), lambda b, j, e, f, cnt: (b, j, 0))
    tok1 = lambda wd: pl.BlockSpec((1, tm, wd), lambda b, j, e, f, cnt: (b, j, 0), pipeline_mode=pl.Buffered(1))
    final_norm = final_gain is not None
    fg = (final_gain if final_norm else jnp.ones((D_MODEL,), F32)).reshape(1, D_MODEL)
    grid_spec = pltpu.PrefetchScalarGridSpec(
        num_scalar_prefetch=1,
        grid=(b_, nj, ne, nf),
        in_specs=[tok1(D_MODEL), tok1(D_MODEL), tok1(LANES),
                  pl.BlockSpec((1, SUBLANES, D_MODEL), lambda b, j, e, f, cnt: (mod_row(b), 0, 0)),
                  pl.BlockSpec((1, D_MODEL, tf), lambda b, j, e, f, cnt: (e, 0, f)),
                  pl.BlockSpec((1, D_MODEL, tf), lambda b, j, e, f, cnt: (e, 0, nf + f)),
                  pl.BlockSpec((1, tf, D_MODEL), lambda b, j, e, f, cnt: (e, f, 0)),
                  pl.BlockSpec((1, D_MODEL), lambda b, j, e, f, cnt: (0, 0))],
        out_specs=tok(D_MODEL),
        scratch_shapes=[pltpu.VMEM((tm, D_MODEL), F32),
                        pltpu.VMEM((tm, D_MODEL), BF16),
                        pltpu.VMEM((tm, D_MODEL), F32),
                        pltpu.VMEM((tm, LANES), F32),
                        pltpu.VMEM((LANES, tm), F32)])
    return pl.pallas_call(
        functools.partial(_moe_kernel, final_norm=final_norm),
        grid_spec=grid_spec,
        out_shape=jax.ShapeDtypeStruct((b_, t_, D_MODEL), F32),
        compiler_params=_cparams(("parallel", "parallel", "arbitrary", "arbitrary")),
        name="moe",
    )(counts, x, h, cmb, mods, w_gu, w_gu, w_down, fg)


def _pack_w_in(w):
    offs = np.cumsum((0, QKV_W, DN_WIDTH, 2 * DN_HEADS, 2 * DN_HEADS, GLA_K_WIDTH, GLA_K_WIDTH, GLA_V_WIDTH,
                      GLA_V_WIDTH, 2 * GLA_GATE_RANK))
    seg = lambda i: w[:, offs[i]:offs[i + 1]]
    small = jnp.concatenate([seg(2), seg(3), seg(8)], axis=1)
    small = jnp.pad(small, ((0, 0), (0, SMALL_W - small.shape[1])))
    return jnp.concatenate([seg(0), seg(1), seg(4), seg(5), seg(6), seg(7), small], axis=1).astype(BF16)


def _pack_gate(w_gate2, b_gate):
    ws = []
    for d in range(2):
        lo = GLR_LANE + d * GLA_GATE_RANK
        ws.append(jnp.zeros((SMALL_W, GLA_K_WIDTH), F32).at[lo:lo + GLA_GATE_RANK].set(w_gate2[d]))
    return jnp.stack(ws), b_gate.reshape(2, 1, GLA_K_WIDTH)


def _pack_dn_params(a_log, dt_bias):
    neg_a = jnp.zeros((SMALL_W,), F32).at[GDEC_LANE:GLR_LANE].set(-jnp.exp(a_log.reshape(-1)))
    dtb = jnp.zeros((SMALL_W,), F32).at[GDEC_LANE:GLR_LANE].set(dt_bias.reshape(-1))
    return jnp.zeros((SUBLANES, SMALL_W), F32).at[0].set(neg_a).at[1].set(dtb)


def _pad_rows(a, rows):
    return jnp.pad(a, ((0, rows - a.shape[0]),) + ((0, 0),) * (a.ndim - 1))


def _mixers(x_seq, mods, mod_row, gain_mix, w_in_p, cw, ap, wgs, bgs, states, ncol, tm, tb):
    b_, t_, _ = x_seq.shape
    qkv, dnz, gqk, gv, gz, sm = _proj(x_seq, mods, mod_row, gain_mix, w_in_p, tm, ncol)
    q, k, v, p, p_col = _prep(qkv, sm, cw, ap, tm, ncol)
    rows = t_ // ncol
    o_dn_f, o_dn_b, s_dn = _dn_scan(q, k, v, p, states[0], tb)
    o_gl_f, o_gl_b, s_gl = _gla_scan(gqk, gv, p_col, wgs, bgs, states[1], ncol, min(rows, 2 * CHUNK))
    outs = [(o_dn_f, o_gl_f), (o_dn_b, o_gl_b)]
    return outs, dnz, gz, (s_dn, s_gl)


def kernel(x, c, ctx, c_ctx, w_ada, b_ada, norm_mix, norm_ffn, w_in, conv_qkv, dn_a_log, dn_dt_bias, dn_norm,
           gla_w_gate2, gla_b_gate, gla_norm, w_out, ffn_w_gu, ffn_w_down, moe_router, moe_w_gu, moe_w_down,
           final_norm):
    b_, seq, _ = x.shape
    depth = w_ada.shape[0]
    ctx_len = ctx.shape[1]
    mod_rows = -(-(b_ + 1) // SUBLANES) * SUBLANES
    c_rows = _pad_rows(jnp.concatenate([c, c_ctx[None, :]], axis=0), mod_rows)
    lat_row = lambda b: b
    ctx_row = lambda b: b_
    tm_lat, tm_ctx = 512, ctx_len
    tb = 4 * CHUNK

    for i in range(depth):
        last = i == depth - 1
        mods = _ada(c_rows, w_ada[i], b_ada[i]).reshape(mod_rows, N_MOD, D_MODEL)
        mods = jnp.pad(mods, ((0, 0), (0, SUBLANES - N_MOD), (0, 0)))
        w_in_p = _pack_w_in(w_in[i])
        cw = _pad_rows(conv_qkv[i], SUBLANES)
        ap = _pack_dn_params(dn_a_log[i], dn_dt_bias[i])
        wgs, bgs = _pack_gate(gla_w_gate2[i], gla_b_gate[i])
        s0 = (jnp.zeros((b_, 2, DN_HEADS, DN_HEAD_DIM, DN_HEAD_DIM), F32),
              jnp.zeros((b_, 2, GLA_HEADS, GLA_DV, GLA_DK), F32))
        outs_c, dnz_c, gz_c, s_ctx = _mixers(ctx, mods, ctx_row, norm_mix[i], w_in_p, cw, ap, wgs, bgs, s0, 1,
                                             tm_ctx, tb)
        outs_l, dnz_l, gz_l, _ = _mixers(x, mods, lat_row, norm_mix[i], w_in_p, cw, ap, wgs, bgs, s_ctx, GRID_W,
                                         tm_lat, tb)
        w_out_b = w_out[i].astype(BF16)
        is_moe = i % 2 == 1
        if is_moe:
            router = jnp.pad(moe_router[i // 2], ((0, 0), (0, LANES - N_EXPERTS)))
            w_gu = moe_w_gu[i // 2].astype(BF16)
            w_dn = moe_w_down[i // 2].astype(BF16)
        else:
            router = None
            w_gu = ffn_w_gu[i // 2].astype(BF16)
            w_dn = ffn_w_down[i // 2].astype(BF16)

        def channel(tokens, outs, dnz, gz, row, tm, ncol, final_gain):
            res = _mixout(tokens, outs[0][0], outs[1][0], dnz, outs[0][1], outs[1][1], gz, mods, row, dn_norm[i],
                          gla_norm[i], w_out_b, norm_ffn[i], router, tm, ncol)
            if is_moe:
                x_mid, h, cmb = res
                tm_moe = min(1024, tokens.shape[1])
                return _moe(x_mid, h, cmb, mods, row, w_gu, w_dn, final_gain, tm_moe, D_EXPERT_TILE)
            x_mid, h = res
            y = _ffn(x_mid, h, mods, row, w_gu, w_dn, tm)
            if final_gain is not None:
                raise NotImplementedError("final norm is fused into the expert mixer only")
            return y

        x = channel(x, outs_l, dnz_l, gz_l, lat_row, tm_lat, GRID_W, final_norm if last else None)
        if not last:
            ctx = channel(ctx, outs_c, dnz_c, gz_c, ctx_row, tm_ctx, 1, None)
    return x
```

```python
import functools

import jax
import jax.numpy as jnp
import numpy as np
from jax import lax
from jax.experimental import pallas as pl
from jax.experimental.pallas import tpu as pltpu

F32 = jnp.float32
BF16 = jnp.bfloat16
HIGHEST = lax.Precision.HIGHEST

D_MODEL = 1024
GRID_W = 64
DN_HEADS = 4
DN_HEAD_DIM = 128
DN_WIDTH = DN_HEADS * DN_HEAD_DIM
CONV_W = 5
GLA_HEADS = 4
GLA_DK = 64
GLA_DV = 128
GLA_K_WIDTH = GLA_HEADS * GLA_DK
GLA_V_WIDTH = GLA_HEADS * GLA_DV
GLA_GATE_RANK = 16
GLA_GATE_NORMALIZER = 16.0
N_EXPERTS = 8
N_MOD = 6
NORM_EPS = 1e-6

LANES = 128
SUBLANES = 8
CHUNK = 64
SUB = 16
EXP_CLAMP = 80.0
D_EXPERT_TILE = 896
MOE_TILE = 64
MOE_GROUP = 4
GLA_COLS_PER_STEP = 8
VMEM_LIMIT = 56 * 1024 * 1024

QKV_W = 3 * DN_WIDTH
SMALL_W = LANES
PROJ_WIDTHS = (QKV_W, DN_WIDTH, 2 * GLA_K_WIDTH, GLA_V_WIDTH, GLA_V_WIDTH, SMALL_W)
PROJ_COLS = sum(PROJ_WIDTHS)
BETA_LANE = 0
GDEC_LANE = 2 * DN_HEADS
GLR_LANE = 4 * DN_HEADS


def _cparams(sem, vmem=VMEM_LIMIT):
    return pltpu.CompilerParams(dimension_semantics=sem, vmem_limit_bytes=vmem)


def _sigmoid(t):
    return 1.0 / (1.0 + jnp.exp(-t))


def _silu(t):
    return t * _sigmoid(t)


def _softplus(t):
    return jnp.maximum(t, 0.0) + jnp.log(1.0 + jnp.exp(-jnp.abs(t)))


def _bdot(a, b):
    return jnp.dot(a.astype(BF16), b.astype(BF16), preferred_element_type=F32)


def _bdot_nt(a, b):
    return lax.dot_general(a.astype(BF16), b.astype(BF16), (((1,), (1,)), ((), ())), preferred_element_type=F32)


def _bdot_tn(a, b):
    return lax.dot_general(a.astype(BF16), b.astype(BF16), (((0,), (0,)), ((), ())), preferred_element_type=F32)


def _split_bf16(t, terms):
    parts = []
    for _ in range(terms):
        part = t.astype(BF16)
        parts.append(part)
        t = t - part.astype(F32)
    return parts


def _dot_exact_lhs(a, x, terms=2):
    a = a.astype(BF16)
    out = None
    for part in _split_bf16(x, terms):
        d = jnp.dot(a, part, preferred_element_type=F32)
        out = d if out is None else out + d
    return out


def _dot_split(a, b):
    a_hi, a_lo = _split_bf16(a, 2)
    b_hi, b_lo = _split_bf16(b, 2)
    dot = lambda u, v: jnp.dot(u, v, preferred_element_type=F32)
    return dot(a_hi, b_hi) + (dot(a_lo, b_hi) + dot(a_hi, b_lo))


def _rms(t, gain):
    return t * lax.rsqrt(jnp.mean(t * t, axis=-1, keepdims=True) + NORM_EPS) * gain


def _ada_kernel(c_ref, w_ref, b_ref, o_ref):
    s = _silu(c_ref[...])
    o_ref[...] = jnp.dot(s, w_ref[...], precision=HIGHEST, preferred_element_type=F32) + b_ref[...]


def _ada(c_rows, w, b):
    rows = c_rows.shape[0]
    n = w.shape[1]
    tn = D_MODEL
    return pl.pallas_call(
        _ada_kernel,
        grid=(n // tn,),
        in_specs=[pl.BlockSpec((rows, D_MODEL), lambda j: (0, 0)),
                  pl.BlockSpec((D_MODEL, tn), lambda j: (0, j)),
                  pl.BlockSpec((1, tn), lambda j: (0, j))],
        out_specs=pl.BlockSpec((rows, tn), lambda j: (0, j)),
        out_shape=jax.ShapeDtypeStruct((rows, n), F32),
        compiler_params=_cparams(("arbitrary",)),
        name="ada",
    )(c_rows, w, b.reshape(1, n))


def _to_columns(val, out_ref, scr_ref, ncol):
    nrows, width = val.shape
    groups = width // LANES
    grid_rows = nrows // ncol
    pitch = _col_pitch(ncol)
    for g in range(groups):
        for r in range(grid_rows):
            scr_ref[g, r * pitch:r * pitch + ncol, :] = val[r * ncol:(r + 1) * ncol, g * LANES:(g + 1) * LANES]
    for c in range(ncol):
        for g in range(groups):
            lo = c * width + g * LANES
            out_ref[0, :, lo:lo + LANES] = scr_ref[g, pl.ds(c, grid_rows, stride=pitch), :]


def _from_columns(in_val, scr_ref, ncol):
    grid_rows = in_val.shape[0]
    width = in_val.shape[1] // ncol
    groups = width // LANES
    pitch = _col_pitch(ncol)
    for c in range(ncol):
        for g in range(groups):
            lo = c * width + g * LANES
            scr_ref[g, pl.ds(c, grid_rows, stride=pitch), :] = in_val[:, lo:lo + LANES]
    return jnp.concatenate(
        [jnp.concatenate([scr_ref[g, r * pitch:r * pitch + ncol, :] for r in range(grid_rows)], axis=0)
         for g in range(groups)], axis=1)


def _col_pitch(ncol):
    return ncol + SUBLANES // 2


def _col_scratch(groups, grid_rows, ncol):
    rows = -(-grid_rows * _col_pitch(ncol) // SUBLANES) * SUBLANES
    return pltpu.VMEM((groups, rows, LANES), F32)


def _proj_kernel(x_ref, mod_ref, gain_ref, w_ref, qkv_ref, dnz_ref, gqk_ref, gv_ref, gz_ref, sm_ref, *scr, ncol):
    x = x_ref[0]
    y = _rms(x, gain_ref[...])
    h = (y * (1.0 + mod_ref[0, 1:2, :]) + mod_ref[0, 0:1, :]).astype(BF16)
    r = jnp.dot(h, w_ref[...], preferred_element_type=F32)
    o = 0
    for ref, wd in zip((qkv_ref, dnz_ref, gqk_ref, gv_ref, gz_ref, sm_ref), PROJ_WIDTHS):
        if ncol > 1 and (ref is gqk_ref or ref is gv_ref):
            _to_columns(r[:, o:o + wd], ref, scr[0], ncol)
        else:
            ref[0] = r[:, o:o + wd]
        o += wd


def _proj(x, mods, mod_row, gain, w, tm, ncol):
    b_, t_, _ = x.shape
    tok = lambda wd: pl.BlockSpec((1, tm, wd), lambda b, j: (b, j, 0))
    colm = lambda wd: pl.BlockSpec((1, tm // ncol, ncol * wd), lambda b, j: (b, j, 0))
    col_major = (False, False, ncol > 1, ncol > 1, False, False)
    scratch = [_col_scratch(GLA_V_WIDTH // LANES, tm // ncol, ncol)] if ncol > 1 else []
    return pl.pallas_call(
        functools.partial(_proj_kernel, ncol=ncol),
        grid=(b_, t_ // tm),
        in_specs=[tok(D_MODEL),
                  pl.BlockSpec((1, SUBLANES, D_MODEL), lambda b, j: (mod_row(b), 0, 0)),
                  pl.BlockSpec((1, D_MODEL), lambda b, j: (0, 0)),
                  pl.BlockSpec((D_MODEL, PROJ_COLS), lambda b, j: (0, 0))],
        out_specs=[colm(wd) if cm else tok(wd) for wd, cm in zip(PROJ_WIDTHS, col_major)],
        out_shape=[jax.ShapeDtypeStruct((b_, t_ // ncol, ncol * wd) if cm else (b_, t_, wd), F32)
                   for wd, cm in zip(PROJ_WIDTHS, col_major)],
        scratch_shapes=scratch,
        compiler_params=_cparams(("parallel", "parallel")),
        name="proj",
    )(x, mods, gain.reshape(1, D_MODEL), w)


def _prep_kernel(cur_ref, prev_ref, next_ref, sm_ref, cw_ref, ap_ref, q_ref, k_ref, v_ref, p_ref, *rest, ncol):
    if ncol > 1:
        pcol_ref, ext_ref, col_scr = rest
    else:
        (ext_ref,) = rest
    j = pl.program_id(1)
    nj = pl.num_programs(1)
    tl = cur_ref.shape[1]
    halo = SUBLANES
    pad = CONV_W // 2
    ext_ref[0:halo, :] = jnp.where(j > 0, prev_ref[0], 0.0)
    ext_ref[halo:halo + tl, :] = cur_ref[0]
    ext_ref[halo + tl:2 * halo + tl, :] = jnp.where(j < nj - 1, next_ref[0], 0.0)
    acc = None
    for w in range(CONV_W):
        term = ext_ref[pl.ds(halo - pad + w, tl), :] * cw_ref[w:w + 1, :]
        acc = term if acc is None else acc + term
    s = _silu(acc)
    for h in range(DN_HEADS):
        lo = h * DN_HEAD_DIM
        qh = s[:, lo:lo + DN_HEAD_DIM]
        kh = s[:, DN_WIDTH + lo:DN_WIDTH + lo + DN_HEAD_DIM]
        q_ref[0, :, lo:lo + DN_HEAD_DIM] = (qh * lax.rsqrt(jnp.sum(qh * qh, axis=-1, keepdims=True) + NORM_EPS)
                                            * DN_HEAD_DIM ** -0.5)
        k_ref[0, :, lo:lo + DN_HEAD_DIM] = kh * lax.rsqrt(jnp.sum(kh * kh, axis=-1, keepdims=True) + NORM_EPS)
    v_ref[0] = s[:, 2 * DN_WIDTH:]
    sm = sm_ref[0]
    lane = lax.broadcasted_iota(jnp.int32, sm.shape, 1)
    beta = _sigmoid(sm)
    gdec = ap_ref[0:1, :] * _softplus(sm + ap_ref[1:2, :])
    p = jnp.where(lane < GDEC_LANE, beta, jnp.where(lane < GLR_LANE, gdec, sm))
    p_ref[0] = p
    if ncol > 1:
        _to_columns(p, pcol_ref, col_scr, ncol)


def _prep(qkv, sm, cw, ap, tl, ncol):
    b_, t_, _ = qkv.shape
    nb = t_ // tl
    r8 = tl // SUBLANES
    last8 = t_ // SUBLANES - 1
    tok = lambda wd: pl.BlockSpec((1, tl, wd), lambda b, j: (b, j, 0))
    out_specs = [tok(DN_WIDTH), tok(DN_WIDTH), tok(DN_WIDTH), tok(SMALL_W)]
    out_shape = [jax.ShapeDtypeStruct((b_, t_, DN_WIDTH), F32)] * 3 + [jax.ShapeDtypeStruct((b_, t_, SMALL_W), F32)]
    scratch = [pltpu.VMEM((tl + 2 * SUBLANES, QKV_W), F32)]
    if ncol > 1:
        out_specs.append(pl.BlockSpec((1, tl // ncol, ncol * SMALL_W), lambda b, j: (b, j, 0)))
        out_shape.append(jax.ShapeDtypeStruct((b_, t_ // ncol, ncol * SMALL_W), F32))
        scratch.append(_col_scratch(1, tl // ncol, ncol))
    outs = pl.pallas_call(
        functools.partial(_prep_kernel, ncol=ncol),
        grid=(b_, nb),
        in_specs=[tok(QKV_W),
                  pl.BlockSpec((1, SUBLANES, QKV_W), lambda b, j: (b, jnp.maximum(j * r8 - 1, 0), 0)),
                  pl.BlockSpec((1, SUBLANES, QKV_W), lambda b, j: (b, jnp.minimum((j + 1) * r8, last8), 0)),
                  tok(SMALL_W),
                  pl.BlockSpec((SUBLANES, QKV_W), lambda b, j: (0, 0)),
                  pl.BlockSpec((SUBLANES, SMALL_W), lambda b, j: (0, 0))],
        out_specs=out_specs,
        out_shape=out_shape,
        scratch_shapes=scratch,
        compiler_params=_cparams(("parallel", "parallel")),
        name="prep",
    )(qkv, qkv, qkv, sm, cw, ap)
    return outs if ncol > 1 else list(outs) + [outs[3]]


def _order_masks(n, rev):
    ri = lax.broadcasted_iota(jnp.int32, (n, n), 0)
    ci = lax.broadcasted_iota(jnp.int32, (n, n), 1)
    if rev:
        return ri, ci, ci >= ri, ci > ri
    return ri, ci, ci <= ri, ci < ri


def _dn_kernel(qf_ref, kf_ref, vf_ref, pf_ref, qb_ref, kb_ref, vb_ref, pb_ref, s0_ref, of_ref, ob_ref, st_ref, s_scr):
    j = pl.program_id(1)
    nj = pl.num_programs(1)

    @pl.when(j == 0)
    def _():
        s_scr[...] = s0_ref[0]

    c_ = CHUNK
    tb = qf_ref.shape[1]
    nchunk = tb // c_
    streams = ((qf_ref, kf_ref, vf_ref, pf_ref, of_ref), (qb_ref, kb_ref, vb_ref, pb_ref, ob_ref))
    masks, gcum, gcum_t, order = [], [], [], []
    for d, rev in enumerate((False, True)):
        ri, ci, incl, strict = _order_masks(c_, rev)
        levels = []
        s = 1
        while s < c_:
            levels.append(strict & ((ri // (2 * s)) == (ci // (2 * s))) & ((ri // s) != (ci // s)))
            s *= 2
        masks.append(dict(incl=incl, strict=strict, eye=(ri == ci).astype(F32), levels=levels,
                          last=0 if rev else c_ - 1))
        rb, cb, incl_b, _ = _order_masks(tb, rev)
        cum_mat = incl_b & ((rb // c_) == (cb // c_))
        gcum.append(_dot_exact_lhs(cum_mat, streams[d][3][0], terms=3))
        gcum_t.append(gcum[d].T)
        order.append([nchunk - 1 - cc if rev else cc for cc in range(nchunk)])

    inst = [(d, c, h) for step in range(nchunk) for d in range(2) for c in (order[d][step],)
            for h in range(DN_HEADS)]
    pre = {}
    for d, c, h in inst:
        q_ref, k_ref, v_ref, p_ref, _ = streams[d]
        m = masks[d]
        r0 = c * c_
        lo = h * DN_HEAD_DIM
        lb = BETA_LANE + d * DN_HEADS + h
        lg = GDEC_LANE + d * DN_HEADS + h
        beta = p_ref[0, r0:r0 + c_, lb:lb + 1]
        gc = gcum[d][r0:r0 + c_, lg:lg + 1]
        gr = gcum_t[d][lg:lg + 1, r0:r0 + c_]
        glast = gcum[d][r0 + m["last"]:r0 + m["last"] + 1, lg:lg + 1]
        decay = jnp.exp(jnp.where(m["incl"], gc - gr, -jnp.inf))
        qh = q_ref[0, r0:r0 + c_, lo:lo + DN_HEAD_DIM]
        kh = k_ref[0, r0:r0 + c_, lo:lo + DN_HEAD_DIM]
        vh = v_ref[0, r0:r0 + c_, lo:lo + DN_HEAD_DIM]
        kb = kh * beta
        eg = jnp.exp(gc)
        pre[d, c, h] = dict(decay=decay, kq_lhs=jnp.concatenate([kb, qh], axis=0), kh=kh,
                            rhs=jnp.concatenate([vh * beta, kb * eg], axis=1), q_dec=qh * eg,
                            k_dec=kh * jnp.exp(glast - gc), e_last=jnp.exp(glast))
    kq = {i: _bdot_nt(pre[i]["kq_lhs"], pre[i]["kh"]) for i in inst}
    a = {i: jnp.where(masks[i[0]]["strict"], kq[i][:c_] * pre[i]["decay"], 0.0) for i in inst}
    attn = {i: kq[i][c_:] * pre[i]["decay"] for i in inst}
    t = {i: masks[i[0]]["eye"] - jnp.where(masks[i[0]]["levels"][0], a[i], 0.0) for i in inst}
    for lvl in range(1, len(masks[0]["levels"])):
        ta = {i: _bdot(t[i], jnp.where(masks[i[0]]["levels"][lvl], a[i], 0.0)) for i in inst}
        t = {i: t[i] - _bdot(ta[i], t[i]) for i in inst}
    uw = {i: _bdot(t[i], pre[i]["rhs"]) for i in inst}
    for step in range(nchunk):
        group = [(d, order[d][step], h) for d in range(2) for h in range(DN_HEADS)]
        st = {i: s_scr[i[0], i[2]] for i in group}
        ws = {i: _bdot(jnp.concatenate([uw[i][:, DN_HEAD_DIM:], pre[i]["q_dec"]], axis=0), st[i]) for i in group}
        v_new = {i: uw[i][:, :DN_HEAD_DIM] - ws[i][:c_] for i in group}
        for i in group:
            d, c, h = i
            streams[d][4][0, c * c_:(c + 1) * c_, h * DN_HEAD_DIM:(h + 1) * DN_HEAD_DIM] = (
                ws[i][c_:] + _bdot(attn[i], v_new[i]))
        for i in group:
            s_scr[i[0], i[2]] = st[i] * pre[i]["e_last"] + _bdot_tn(pre[i]["k_dec"], v_new[i])

    @pl.when(j == nj - 1)
    def _():
        st_ref[0] = s_scr[...]


def _dn_scan(q, k, v, p, s0, tb):
    b_, t_, _ = q.shape
    nb = t_ // tb
    fwd = lambda wd: pl.BlockSpec((1, tb, wd), lambda b, j: (b, j, 0))
    bwd = lambda wd: pl.BlockSpec((1, tb, wd), lambda b, j: (b, nb - 1 - j, 0))
    widths = (DN_WIDTH, DN_WIDTH, DN_WIDTH, SMALL_W)
    st_shape = (2, DN_HEADS, DN_HEAD_DIM, DN_HEAD_DIM)
    st_spec = pl.BlockSpec((1,) + st_shape, lambda b, j: (b, 0, 0, 0, 0))
    return pl.pallas_call(
        _dn_kernel,
        grid=(b_, nb),
        in_specs=[fwd(w) for w in widths] + [bwd(w) for w in widths] + [st_spec],
        out_specs=[fwd(DN_WIDTH), bwd(DN_WIDTH), st_spec],
        out_shape=[jax.ShapeDtypeStruct((b_, t_, DN_WIDTH), F32)] * 2 + [jax.ShapeDtypeStruct((b_,) + st_shape, F32)],
        scratch_shapes=[pltpu.VMEM(st_shape, F32)],
        compiler_params=_cparams(("parallel", "arbitrary")),
        name="dn",
    )(q, k, v, p, q, k, v, p, s0)


def _log_sigmoid(t):
    return jnp.minimum(t, 0.0) - jnp.log(1.0 + jnp.exp(-jnp.abs(t)))


def _gla_kernel(qkf_ref, vf_ref, pf_ref, qkb_ref, vb_ref, pb_ref, wg_ref, bg_ref, s0_ref, of_ref, ob_ref, st_ref,
                s_scr):
    col = pl.program_id(1)
    rb = pl.program_id(2)
    ncol = pl.num_programs(1)
    nrb = pl.num_programs(2)

    @pl.when((col == 0) & (rb == 0))
    def _():
        s_scr[...] = s0_ref[0]

    c_ = CHUNK
    nchunk = qkf_ref.shape[1] // c_
    ncb = pf_ref.shape[2] // SMALL_W
    nsub = c_ // SUB
    streams = ((qkf_ref, vf_ref, pf_ref, of_ref), (qkb_ref, vb_ref, pb_ref, ob_ref))
    masks, order = [], []
    for rev in (False, True):
        ri, ci, incl, _ = _order_masks(c_, rev)
        pos_r = (c_ - 1 - ri) if rev else ri
        pos_c = (c_ - 1 - ci) if rev else ci
        first_rows = [(c_ - 1 - SUB * i) if rev else SUB * i for i in range(nsub)]
        masks.append(dict(incl=incl, diag=incl & ((pos_r // SUB) == (pos_c // SUB)),
                          off=(pos_c // SUB) < (pos_r // SUB), last=0 if rev else c_ - 1, first_rows=first_rows,
                          rev=rev))
        cols = [ncb - 1 - i if rev else i for i in range(ncb)]
        chunks = [nchunk - 1 - cc if rev else cc for cc in range(nchunk)]
        order.append([(cb, c) for cb in cols for c in chunks])
    nstep = len(order[0])
    la_col = {}
    for d in range(2):
        for cb in range(ncb):
            gate = _dot_split(streams[d][2][0, :, cb * SMALL_W:(cb + 1) * SMALL_W], wg_ref[d]) + bg_ref[d]
            la_col[d, cb] = _log_sigmoid(gate) * (1.0 / GLA_GATE_NORMALIZER)
    pre = {}
    for step in range(nstep):
        for d in range(2):
            cb, c = order[d][step]
            qk_ref, v_ref, _, _ = streams[d]
            m = masks[d]
            r0 = c * c_
            la = la_col[d, cb][r0:r0 + c_, :]
            bc = _dot_exact_lhs(m["incl"], la, terms=3)
            bex = bc - la
            b_last = bc[m["last"]:m["last"] + 1, :]
            pieces = [jnp.broadcast_to(bex[fr:fr + 1, :], (SUB, GLA_K_WIDTH)) for fr in m["first_rows"]]
            refd = jnp.concatenate(pieces[::-1] if m["rev"] else pieces, axis=0)
            mid = m["first_rows"][nsub // 2]
            rmid = bex[mid:mid + 1, :]
            qk0 = cb * 2 * GLA_K_WIDTH
            qa = qk_ref[0, r0:r0 + c_, qk0:qk0 + GLA_K_WIDTH] * GLA_DK ** -0.5
            ka = qk_ref[0, r0:r0 + c_, qk0 + GLA_K_WIDTH:qk0 + 2 * GLA_K_WIDTH]
            q_in = qa * jnp.exp(bc)
            k_st = ka * jnp.exp(b_last - bc)
            q_d = qa * jnp.exp(bc - refd)
            k_d = ka * jnp.exp(jnp.minimum(refd - bc, EXP_CLAMP))
            q_o = qa * jnp.exp(jnp.minimum(bc - rmid, EXP_CLAMP))
            k_o = ka * jnp.exp(jnp.minimum(rmid - bc, EXP_CLAMP))
            e_last = jnp.exp(b_last)
            v0 = cb * GLA_V_WIDTH
            for h in range(GLA_HEADS):
                ks = slice(h * GLA_DK, (h + 1) * GLA_DK)
                vh = v_ref[0, r0:r0 + c_, v0 + h * GLA_DV:v0 + (h + 1) * GLA_DV]
                attn = (jnp.where(m["diag"], _bdot_nt(q_d[:, ks], k_d[:, ks]), 0.0)
                        + jnp.where(m["off"], _bdot_nt(q_o[:, ks], k_o[:, ks]), 0.0))
                pre[d, step, h] = dict(attn=attn, vh=vh, q_in=q_in[:, ks], e_last=e_last[:, ks],
                                       kv=_bdot_tn(vh, k_st[:, ks]))
    st = {(d, h): s_scr[d, h] for d in range(2) for h in range(GLA_HEADS)}
    for step in range(nstep):
        for d in range(2):
            cb, c = order[d][step]
            r0 = c * c_
            v0 = cb * GLA_V_WIDTH
            for h in range(GLA_HEADS):
                i = (d, step, h)
                streams[d][3][0, r0:r0 + c_, v0 + h * GLA_DV:v0 + (h + 1) * GLA_DV] = (
                    _bdot_nt(pre[i]["q_in"], st[d, h]) + _bdot(pre[i]["attn"], pre[i]["vh"]))
                st[d, h] = st[d, h] * pre[i]["e_last"] + pre[i]["kv"]
    for d in range(2):
        for h in range(GLA_HEADS):
            s_scr[d, h] = st[d, h]

    @pl.when((col == ncol - 1) & (rb == nrb - 1))
    def _():
        st_ref[0] = s_scr[...]


def _gla_scan(qk, v, p, wg, bg, s0, ncol, tr):
    b_, rows, _ = qk.shape
    nrb = rows // tr
    ncb = min(GLA_COLS_PER_STEP, ncol)
    ncblk = ncol // ncb
    assert ncb == 1 or nrb == 1
    fwd = lambda wd: pl.BlockSpec((1, tr, ncb * wd), lambda b, c, r: (b, r, c))
    bwd = lambda wd: pl.BlockSpec((1, tr, ncb * wd), lambda b, c, r: (b, nrb - 1 - r, ncblk - 1 - c))
    widths = (2 * GLA_K_WIDTH, GLA_V_WIDTH, SMALL_W)
    st_shape = (2, GLA_HEADS, GLA_DV, GLA_DK)
    st_spec = pl.BlockSpec((1,) + st_shape, lambda b, c, r: (b, 0, 0, 0, 0))
    return pl.pallas_call(
        _gla_kernel,
        grid=(b_, ncblk, nrb),
        in_specs=[fwd(w) for w in widths] + [bwd(w) for w in widths] + [
            pl.BlockSpec((2, SMALL_W, GLA_K_WIDTH), lambda b, c, r: (0, 0, 0)),
            pl.BlockSpec((2, 1, GLA_K_WIDTH), lambda b, c, r: (0, 0, 0)),
            st_spec],
        out_specs=[fwd(GLA_V_WIDTH), bwd(GLA_V_WIDTH), st_spec],
        out_shape=[jax.ShapeDtypeStruct((b_, rows, ncol * GLA_V_WIDTH), F32)] * 2 + [
            jax.ShapeDtypeStruct((b_,) + st_shape, F32)],
        scratch_shapes=[pltpu.VMEM(st_shape, F32)],
        compiler_params=_cparams(("parallel", "arbitrary", "arbitrary")),
        name="gla",
    )(qk, v, p, qk, v, p, wg, bg, s0)


def _head_norm_gate(o, z, gain):
    parts = []
    for h in range(o.shape[1] // LANES):
        hs = slice(h * LANES, (h + 1) * LANES)
        parts.append(_rms(o[:, hs], gain) * _silu(z[:, hs]))
    return jnp.concatenate(parts, axis=1)


def _mixout_kernel(*refs, with_router, ncol):
    (x_ref, odf_ref, odb_ref, dnz_ref, ogf_ref, ogb_ref, glz_ref, mod_ref, gd_ref, gg_ref, w_ref, gf_ref) = refs[:12]
    refs = list(refs[12:])
    col_scr = refs.pop() if ncol > 1 else None
    if with_router:
        rw_ref, xo_ref, h_ref, cmb_ref = refs
    else:
        xo_ref, h_ref = refs
    m_dn = _head_norm_gate(odf_ref[0] + odb_ref[0], dnz_ref[0], gd_ref[...])
    o_gl = ogf_ref[0] + ogb_ref[0]
    if ncol > 1:
        o_gl = _from_columns(o_gl, col_scr, ncol)
    m_gl = _head_norm_gate(o_gl, glz_ref[0], gg_ref[...])
    m = jnp.concatenate([m_dn, m_gl], axis=1).astype(BF16)
    x = x_ref[0] + mod_ref[0, 2:3, :] * jnp.dot(m, w_ref[...], preferred_element_type=F32)
    xo_ref[0] = x
    h = _rms(x, gf_ref[...]) * (1.0 + mod_ref[0, 4:5, :]) + mod_ref[0, 3:4, :]
    h_ref[0] = h.astype(BF16)
    if with_router:
        logits = _dot_split(h, rw_ref[...])
        lane = lax.broadcasted_iota(jnp.int32, logits.shape, 1).astype(F32)
        logits = jnp.where(lane < N_EXPERTS, logits, -jnp.inf)
        e = jnp.exp(logits - jnp.max(logits, axis=-1, keepdims=True))
        probs = e / jnp.sum(e, axis=-1, keepdims=True)
        p1 = jnp.max(probs, axis=-1, keepdims=True)
        i1 = jnp.min(jnp.where(probs == p1, lane, float(LANES)), axis=-1, keepdims=True)
        rest = jnp.where(lane == i1, -1.0, probs)
        p2 = jnp.max(rest, axis=-1, keepdims=True)
        i2 = jnp.min(jnp.where(rest == p2, lane, float(LANES)), axis=-1, keepdims=True)
        cmb_ref[0] = (jnp.where(lane == i1, p1, 0.0) + jnp.where(lane == i2, p2, 0.0)) / (p1 + p2)


def _mixout(x, odf, odb, dnz, ogf, ogb, glz, mods, mod_row, gd, gg, w_out, gf, router, tm, ncol):
    b_, t_, _ = x.shape
    with_router = router is not None
    tok = lambda wd: pl.BlockSpec((1, tm, wd), lambda b, j: (b, j, 0))
    colm = lambda wd: pl.BlockSpec((1, tm // ncol, ncol * wd), lambda b, j: (b, j, 0))
    const = lambda shp: pl.BlockSpec(shp, lambda b, j: (0,) * len(shp))
    in_specs = [tok(D_MODEL)] + [tok(DN_WIDTH)] * 3 + [colm(GLA_V_WIDTH)] * 2 + [tok(GLA_V_WIDTH)] + [
        pl.BlockSpec((1, SUBLANES, D_MODEL), lambda b, j: (mod_row(b), 0, 0)),
        const((1, LANES)), const((1, LANES)), const((D_MODEL, D_MODEL)), const((1, D_MODEL))]
    args = [x, odf, odb, dnz, ogf, ogb, glz, mods, gd.reshape(1, LANES), gg.reshape(1, LANES), w_out,
            gf.reshape(1, D_MODEL)]
    out_specs = [tok(D_MODEL), tok(D_MODEL)]
    out_shape = [jax.ShapeDtypeStruct((b_, t_, D_MODEL), F32), jax.ShapeDtypeStruct((b_, t_, D_MODEL), BF16)]
    if with_router:
        in_specs.append(const((D_MODEL, LANES)))
        args.append(router)
        out_specs.append(tok(LANES))
        out_shape.append(jax.ShapeDtypeStruct((b_, t_, LANES), F32))
    scratch = [_col_scratch(GLA_V_WIDTH // LANES, tm // ncol, ncol)] if ncol > 1 else []
    return pl.pallas_call(
        functools.partial(_mixout_kernel, with_router=with_router, ncol=ncol),
        grid=(b_, t_ // tm),
        in_specs=in_specs,
        out_specs=out_specs,
        out_shape=out_shape,
        scratch_shapes=scratch,
        compiler_params=_cparams(("parallel", "parallel")),
        name="mixout",
    )(*args)


def _ffn_kernel(x_ref, h_ref, mod_ref, wg_ref, wu_ref, wd_ref, o_ref, *, nsplit):
    h = h_ref[0]
    f = wg_ref.shape[1]
    fs = f // nsplit
    acc = None
    for s in range(nsplit):
        g = jnp.dot(h, wg_ref[:, s * fs:(s + 1) * fs], preferred_element_type=F32)
        u = jnp.dot(h, wu_ref[:, s * fs:(s + 1) * fs], preferred_element_type=F32)
        part = jnp.dot((_silu(g) * u).astype(BF16), wd_ref[s * fs:(s + 1) * fs, :], preferred_element_type=F32)
        acc = part if acc is None else acc + part
    o_ref[0] = x_ref[0] + mod_ref[0, 5:6, :] * acc


def _ffn(x, h, mods, mod_row, w_gu, w_down, tm):
    b_, t_, _ = x.shape
    f = w_down.shape[0]
    tok = lambda: pl.BlockSpec((1, tm, D_MODEL), lambda b, j: (b, j, 0))
    return pl.pallas_call(
        functools.partial(_ffn_kernel, nsplit=2),
        grid=(b_, t_ // tm),
        in_specs=[tok(), tok(),
                  pl.BlockSpec((1, SUBLANES, D_MODEL), lambda b, j: (mod_row(b), 0, 0)),
                  pl.BlockSpec((D_MODEL, f), lambda b, j: (0, 0)),
                  pl.BlockSpec((D_MODEL, f), lambda b, j: (0, 1)),
                  pl.BlockSpec((f, D_MODEL), lambda b, j: (0, 0))],
        out_specs=tok(),
        out_shape=jax.ShapeDtypeStruct((b_, t_, D_MODEL), F32),
        compiler_params=_cparams(("parallel", "parallel")),
        name="ffn",
    )(x, h, mods, w_gu, w_gu, w_down)


def _for_row_tiles(ntiles, fn):
    rows = MOE_GROUP * MOE_TILE
    single = ntiles == MOE_GROUP + 1
    keep = 1 - single.astype(jnp.int32)
    ngroup = (ntiles // MOE_GROUP) * keep
    rest = (ntiles % MOE_GROUP) * keep

    def body(i, carry):
        fn(pl.multiple_of(i * rows, rows), rows)
        return carry

    lax.fori_loop(0, ngroup, body, 0)
    for k in range(1, MOE_GROUP):
        @pl.when(rest == k)
        def _():
            fn(pl.multiple_of(ngroup * rows, rows), k * MOE_TILE)

    @pl.when(single)
    def _():
        fn(0, rows + MOE_TILE)


def _moe_kernel(cnt_ref, x_ref, h_ref, cmb_ref, mod_ref, wg_ref, wu_ref, wd_ref, fg_ref, o_ref,
                acc_ref, xe_ref, ye_ref, rkc_ref, rkr_ref, *, final_norm):
    b = pl.program_id(0)
    j = pl.program_id(1)
    e = pl.program_id(2)
    f = pl.program_id(3)
    nj = pl.num_programs(1)
    ne = pl.num_programs(2)
    nf = pl.num_programs(3)
    tm = h_ref.shape[1]
    ntiles = (cnt_ref[(b * nj + j) * ne + e] + MOE_TILE - 1) // MOE_TILE

    @pl.when((e == 0) & (f == 0))
    def _():
        acc_ref[...] = jnp.zeros_like(acc_ref)
        sb = 2 * LANES
        ri = lax.broadcasted_iota(jnp.int32, (sb, sb), 0)
        ci = lax.broadcasted_iota(jnp.int32, (sb, sb), 1)
        before = (ci < ri).astype(BF16)
        carry = jnp.zeros((1, LANES), F32)
        for s in range(tm // sb):
            sel = (cmb_ref[0, s * sb:(s + 1) * sb, :] > 0.0).astype(F32)
            rank = jnp.dot(before, sel.astype(BF16), preferred_element_type=F32) + carry
            rkc_ref[s * sb:(s + 1) * sb, :] = jnp.where(sel > 0.0, rank, -1.0)
            carry = carry + jnp.sum(sel, axis=0, keepdims=True)
        rkr_ref[...] = rkc_ref[...].T

    @pl.when(f == 0)
    def _():
        rank_row = rkr_ref[pl.ds(e, 1), :]

        def gather(r0, n):
            slot = (lax.broadcasted_iota(jnp.int32, (n, tm), 0) + r0).astype(F32)
            onehot = (rank_row == slot).astype(BF16)
            xe_ref[pl.ds(r0, n), :] = jnp.dot(onehot, h_ref[0], preferred_element_type=F32).astype(BF16)

        _for_row_tiles(ntiles, gather)

    def expert(r0, n):
        xr = xe_ref[pl.ds(r0, n), :]
        g = jnp.dot(xr, wg_ref[0], preferred_element_type=F32)
        u = jnp.dot(xr, wu_ref[0], preferred_element_type=F32)
        y = jnp.dot((_silu(g) * u).astype(BF16), wd_ref[0], preferred_element_type=F32)

        @pl.when(f == 0)
        def _():
            ye_ref[pl.ds(r0, n), :] = y

        @pl.when(f > 0)
        def _():
            ye_ref[pl.ds(r0, n), :] += y

    _for_row_tiles(ntiles, expert)

    @pl.when(f == nf - 1)
    def _():
        lane = lax.broadcasted_iota(jnp.int32, (tm, LANES), 1)
        rank_col = jnp.sum(jnp.where(lane == e, rkc_ref[...], 0.0), axis=-1, keepdims=True)
        ce = jnp.sum(jnp.where(lane == e, cmb_ref[0], 0.0), axis=-1, keepdims=True)

        def scatter(r0, n):
            slot = (lax.broadcasted_iota(jnp.int32, (tm, n), 1) + r0).astype(F32)
            onehot = (rank_col == slot).astype(BF16)
            back = jnp.dot(onehot, ye_ref[pl.ds(r0, n), :].astype(BF16), preferred_element_type=F32)
            acc_ref[...] += ce * back

        _for_row_tiles(ntiles, scatter)

    @pl.when((e == ne - 1) & (f == nf - 1))
    def _():
        y = x_ref[0] + mod_ref[0, 5:6, :] * acc_ref[...]
        if final_norm:
            y = _rms(y, fg_ref[...])
        o_ref[0] = y


def _moe(x, h, cmb, mods, mod_row, w_gu, w_down, final_gain, tm, tf):
    b_, t_, _ = x.shape
    ne, _, f2 = w_gu.shape
    fe = f2 // 2
    assert fe % tf == 0 and tm % (MOE_GROUP * MOE_TILE) == 0 and ne <= SUBLANES
    nf = fe // tf
    nj = t_ // tm
    counts = jnp.sum((cmb[..., :ne] > 0.0).reshape(b_, nj, tm, ne), axis=2, dtype=jnp.int32).reshape(-1)
    tok = lambda wd: pl.BlockSpec((1, tm, wd), lambda b, j, e, f, cnt: (b, j, 0))
    final_norm = final_gain is not None
    fg = (final_gain if final_norm else jnp.ones((D_MODEL,), F32)).reshape(1, D_MODEL)
    grid_spec = pltpu.PrefetchScalarGridSpec(
        num_scalar_prefetch=1,
        grid=(b_, nj, ne, nf),
        in_specs=[tok(D_MODEL), tok(D_MODEL), tok(LANES),
                  pl.BlockSpec((1, SUBLANES, D_MODEL), lambda b, j, e, f, cnt: (mod_row(b), 0, 0)),
                  pl.BlockSpec((1, D_MODEL, tf), lambda b, j, e, f, cnt: (e, 0, f)),
                  pl.BlockSpec((1, D_MODEL, tf), lambda b, j, e, f, cnt: (e, 0, nf + f)),
                  pl.BlockSpec((1, tf, D_MODEL), lambda b, j, e, f, cnt: (e, f, 0)),
                  pl.BlockSpec((1, D_MODEL), lambda b, j, e, f, cnt: (0, 0))],
        out_specs=tok(D_MODEL),
        scratch_shapes=[pltpu.VMEM((tm, D_MODEL), F32),
                        pltpu.VMEM((tm, D_MODEL), BF16),
                        pltpu.VMEM((tm, D_MODEL), F32),
                        pltpu.VMEM((tm, LANES), F32),
                        pltpu.VMEM((LANES, tm), F32)])
    return pl.pallas_call(
        functools.partial(_moe_kernel, final_norm=final_norm),
        grid_spec=grid_spec,
        out_shape=jax.ShapeDtypeStruct((b_, t_, D_MODEL), F32),
        compiler_params=_cparams(("parallel", "parallel", "arbitrary", "arbitrary")),
        name="moe",
    )(counts, x, h, cmb, mods, w_gu, w_gu, w_down, fg)


def _pack_w_in(w):
    offs = np.cumsum((0, QKV_W, DN_WIDTH, 2 * DN_HEADS, 2 * DN_HEADS, GLA_K_WIDTH, GLA_K_WIDTH, GLA_V_WIDTH,
                      GLA_V_WIDTH, 2 * GLA_GATE_RANK))
    seg = lambda i: w[:, offs[i]:offs[i + 1]]
    small = jnp.concatenate([seg(2), seg(3), seg(8)], axis=1)
    small = jnp.pad(small, ((0, 0), (0, SMALL_W - small.shape[1])))
    return jnp.concatenate([seg(0), seg(1), seg(4), seg(5), seg(6), seg(7), small], axis=1).astype(BF16)


def _pack_gate(w_gate2, b_gate):
    ws = []
    for d in range(2):
        lo = GLR_LANE + d * GLA_GATE_RANK
        ws.append(jnp.zeros((SMALL_W, GLA_K_WIDTH), F32).at[lo:lo + GLA_GATE_RANK].set(w_gate2[d]))
    return jnp.stack(ws), b_gate.reshape(2, 1, GLA_K_WIDTH)


def _pack_dn_params(a_log, dt_bias):
    neg_a = jnp.zeros((SMALL_W,), F32).at[GDEC_LANE:GLR_LANE].set(-jnp.exp(a_log.reshape(-1)))
    dtb = jnp.zeros((SMALL_W,), F32).at[GDEC_LANE:GLR_LANE].set(dt_bias.reshape(-1))
    return jnp.zeros((SUBLANES, SMALL_W), F32).at[0].set(neg_a).at[1].set(dtb)


def _pad_rows(a, rows):
    return jnp.pad(a, ((0, rows - a.shape[0]),) + ((0, 0),) * (a.ndim - 1))


def _mixers(x_seq, mods, mod_row, gain_mix, w_in_p, cw, ap, wgs, bgs, states, ncol, tm, tb):
    b_, t_, _ = x_seq.shape
    qkv, dnz, gqk, gv, gz, sm = _proj(x_seq, mods, mod_row, gain_mix, w_in_p, tm, ncol)
    q, k, v, p, p_col = _prep(qkv, sm, cw, ap, tm, ncol)
    rows = t_ // ncol
    o_dn_f, o_dn_b, s_dn = _dn_scan(q, k, v, p, states[0], tb)
    o_gl_f, o_gl_b, s_gl = _gla_scan(gqk, gv, p_col, wgs, bgs, states[1], ncol, min(rows, 2 * CHUNK))
    outs = [(o_dn_f, o_gl_f), (o_dn_b, o_gl_b)]
    return outs, dnz, gz, (s_dn, s_gl)


def kernel(x, c, ctx, c_ctx, w_ada, b_ada, norm_mix, norm_ffn, w_in, conv_qkv, dn_a_log, dn_dt_bias, dn_norm,
           gla_w_gate2, gla_b_gate, gla_norm, w_out, ffn_w_gu, ffn_w_down, moe_router, moe_w_gu, moe_w_down,
           final_norm):
    b_, seq, _ = x.shape
    depth = w_ada.shape[0]
    ctx_len = ctx.shape[1]
    mod_rows = -(-(b_ + 1) // SUBLANES) * SUBLANES
    c_rows = _pad_rows(jnp.concatenate([c, c_ctx[None, :]], axis=0), mod_rows)
    lat_row = lambda b: b
    ctx_row = lambda b: b_
    tm_lat, tm_ctx = 512, ctx_len
    tb = 4 * CHUNK

    for i in range(depth):
        last = i == depth - 1
        mods = _ada(c_rows, w_ada[i], b_ada[i]).reshape(mod_rows, N_MOD, D_MODEL)
        mods = jnp.pad(mods, ((0, 0), (0, SUBLANES - N_MOD), (0, 0)))
        w_in_p = _pack_w_in(w_in[i])
        cw = _pad_rows(conv_qkv[i], SUBLANES)
        ap = _pack_dn_params(dn_a_log[i], dn_dt_bias[i])
        wgs, bgs = _pack_gate(gla_w_gate2[i], gla_b_gate[i])
        s0 = (jnp.zeros((b_, 2, DN_HEADS, DN_HEAD_DIM, DN_HEAD_DIM), F32),
              jnp.zeros((b_, 2, GLA_HEADS, GLA_DV, GLA_DK), F32))
        outs_c, dnz_c, gz_c, s_ctx = _mixers(ctx, mods, ctx_row, norm_mix[i], w_in_p, cw, ap, wgs, bgs, s0, 1,
                                             tm_ctx, tb)
        outs_l, dnz_l, gz_l, _ = _mixers(x, mods, lat_row, norm_mix[i], w_in_p, cw, ap, wgs, bgs, s_ctx, GRID_W,
                                         tm_lat, tb)
        w_out_b = w_out[i].astype(BF16)
        is_moe = i % 2 == 1
        if is_moe:
            router = jnp.pad(moe_router[i // 2], ((0, 0), (0, LANES - N_EXPERTS)))
            w_gu = moe_w_gu[i // 2].astype(BF16)
            w_dn = moe_w_down[i // 2].astype(BF16)
        else:
            router = None
            w_gu = ffn_w_gu[i // 2].astype(BF16)
            w_dn = ffn_w_down[i // 2].astype(BF16)

        def channel(tokens, outs, dnz, gz, row, tm, ncol, final_gain):
            res = _mixout(tokens, outs[0][0], outs[1][0], dnz, outs[0][1], outs[1][1], gz, mods, row, dn_norm[i],
                          gla_norm[i], w_out_b, norm_ffn[i], router, tm, ncol)
            if is_moe:
                x_mid, h, cmb = res
                tm_moe = min(1024, tokens.shape[1])
                return _moe(x_mid, h, cmb, mods, row, w_gu, w_dn, final_gain, tm_moe, D_EXPERT_TILE)
            x_mid, h = res
            y = _ffn(x_mid, h, mods, row, w_gu, w_dn, tm)
            if final_gain is not None:
                raise NotImplementedError("final norm is fused into the expert mixer only")
            return y

        x = channel(x, outs_l, dnz_l, gz_l, lat_row, tm_lat, GRID_W, final_norm if last else None)
        if not last:
            ctx = channel(ctx, outs_c, dnz_c, gz_c, ctx_row, tm_ctx, 1, None)
    return x
```

```python
import functools

import jax
import jax.numpy as jnp
import numpy as np
from jax import lax
from jax.experimental import pallas as pl
from jax.experimental.pallas import tpu as pltpu

F32 = jnp.float32
BF16 = jnp.bfloat16
HIGHEST = lax.Precision.HIGHEST

D_MODEL = 1024
GRID_W = 64
DN_HEADS = 4
DN_HEAD_DIM = 128
DN_WIDTH = DN_HEADS * DN_HEAD_DIM
CONV_W = 5
GLA_HEADS = 4
GLA_DK = 64
GLA_DV = 128
GLA_K_WIDTH = GLA_HEADS * GLA_DK
GLA_V_WIDTH = GLA_HEADS * GLA_DV
GLA_GATE_RANK = 16
GLA_GATE_NORMALIZER = 16.0
N_EXPERTS = 8
N_MOD = 6
NORM_EPS = 1e-6

LANES = 128
SUBLANES = 8
CHUNK = 64
SUB = 16
EXP_CLAMP = 80.0
D_EXPERT_TILE = 896
MOE_TILE = 64
MOE_GROUP = 4
GLA_COLS_PER_STEP = 8
VMEM_LIMIT = 56 * 1024 * 1024

QKV_W = 3 * DN_WIDTH
SMALL_W = LANES
PROJ_WIDTHS = (QKV_W, DN_WIDTH, 2 * GLA_K_WIDTH, GLA_V_WIDTH, GLA_V_WIDTH, SMALL_W)
PROJ_COLS = sum(PROJ_WIDTHS)
BETA_LANE = 0
GDEC_LANE = 2 * DN_HEADS
GLR_LANE = 4 * DN_HEADS


def _cparams(sem, vmem=VMEM_LIMIT):
    return pltpu.CompilerParams(dimension_semantics=sem, vmem_limit_bytes=vmem)


def _sigmoid(t):
    return 1.0 / (1.0 + jnp.exp(-t))


def _silu(t):
    return t * _sigmoid(t)


def _softplus(t):
    return jnp.maximum(t, 0.0) + jnp.log(1.0 + jnp.exp(-jnp.abs(t)))


def _bdot(a, b):
    return jnp.dot(a.astype(BF16), b.astype(BF16), preferred_element_type=F32)


def _bdot_nt(a, b):
    return lax.dot_general(a.astype(BF16), b.astype(BF16), (((1,), (1,)), ((), ())), preferred_element_type=F32)


def _bdot_tn(a, b):
    return lax.dot_general(a.astype(BF16), b.astype(BF16), (((0,), (0,)), ((), ())), preferred_element_type=F32)


def _split_bf16(t, terms):
    parts = []
    for _ in range(terms):
        part = t.astype(BF16)
        parts.append(part)
        t = t - part.astype(F32)
    return parts


def _dot_exact_lhs(a, x, terms=2):
    a = a.astype(BF16)
    out = None
    for part in _split_bf16(x, terms):
        d = jnp.dot(a, part, preferred_element_type=F32)
        out = d if out is None else out + d
    return out


def _dot_split(a, b):
    a_hi, a_lo = _split_bf16(a, 2)
    b_hi, b_lo = _split_bf16(b, 2)
    dot = lambda u, v: jnp.dot(u, v, preferred_element_type=F32)
    return dot(a_hi, b_hi) + (dot(a_lo, b_hi) + dot(a_hi, b_lo))


def _rms(t, gain):
    return t * lax.rsqrt(jnp.mean(t * t, axis=-1, keepdims=True) + NORM_EPS) * gain


def _ada_kernel(c_ref, w_ref, b_ref, o_ref):
    s = _silu(c_ref[...])
    o_ref[...] = jnp.dot(s, w_ref[...], precision=HIGHEST, preferred_element_type=F32) + b_ref[...]


def _ada(c_rows, w, b):
    rows = c_rows.shape[0]
    n = w.shape[1]
    tn = D_MODEL
    return pl.pallas_call(
        _ada_kernel,
        grid=(n // tn,),
        in_specs=[pl.BlockSpec((rows, D_MODEL), lambda j: (0, 0)),
                  pl.BlockSpec((D_MODEL, tn), lambda j: (0, j)),
                  pl.BlockSpec((1, tn), lambda j: (0, j))],
        out_specs=pl.BlockSpec((rows, tn), lambda j: (0, j)),
        out_shape=jax.ShapeDtypeStruct((rows, n), F32),
        compiler_params=_cparams(("arbitrary",)),
        name="ada",
    )(c_rows, w, b.reshape(1, n))


def _to_columns(val, out_ref, scr_ref, ncol):
    nrows, width = val.shape
    groups = width // LANES
    grid_rows = nrows // ncol
    pitch = _col_pitch(ncol)
    for g in range(groups):
        for r in range(grid_rows):
            scr_ref[g, r * pitch:r * pitch + ncol, :] = val[r * ncol:(r + 1) * ncol, g * LANES:(g + 1) * LANES]
    for c in range(ncol):
        for g in range(groups):
            lo = c * width + g * LANES
            out_ref[0, :, lo:lo + LANES] = scr_ref[g, pl.ds(c, grid_rows, stride=pitch), :]


def _from_columns(in_val, scr_ref, ncol):
    grid_rows = in_val.shape[0]
    width = in_val.shape[1] // ncol
    groups = width // LANES
    pitch = _col_pitch(ncol)
    for c in range(ncol):
        for g in range(groups):
            lo = c * width + g * LANES
            scr_ref[g, pl.ds(c, grid_rows, stride=pitch), :] = in_val[:, lo:lo + LANES]
    return jnp.concatenate(
        [jnp.concatenate([scr_ref[g, r * pitch:r * pitch + ncol, :] for r in range(grid_rows)], axis=0)
         for g in range(groups)], axis=1)


def _col_pitch(ncol):
    return ncol + SUBLANES // 2


def _col_scratch(groups, grid_rows, ncol):
    rows = -(-grid_rows * _col_pitch(ncol) // SUBLANES) * SUBLANES
    return pltpu.VMEM((groups, rows, LANES), F32)


def _proj_kernel(x_ref, xp_ref, xn_ref, mod_ref, gain_ref, w_ref, cw_ref, ap_ref,
                 q_ref, k_ref, v_ref, p_ref, dnz_ref, gqk_ref, gv_ref, gz_ref, *rest, ncol):
    if ncol > 1:
        pcol_ref, ext_ref, col_scr = rest
    else:
        (ext_ref,) = rest
    j = pl.program_id(1)
    nj = pl.num_programs(1)
    tm = x_ref.shape[1]
    halo = SUBLANES
    pad = CONV_W // 2
    norm_mod = lambda t: _rms(t, gain_ref[...]) * (1.0 + mod_ref[0, 1:2, :]) + mod_ref[0, 0:1, :]
    h = norm_mod(x_ref[0])
    h_all = jnp.concatenate([h, norm_mod(xp_ref[0]), norm_mod(xn_ref[0])], axis=0).astype(BF16)
    r_qkv = jnp.dot(h_all, w_ref[:, 0:QKV_W], preferred_element_type=F32)
    r = jnp.dot(h.astype(BF16), w_ref[:, QKV_W:], preferred_element_type=F32)
    o = 0
    for ref, wd in zip((dnz_ref, gqk_ref, gv_ref, gz_ref), PROJ_WIDTHS[1:5]):
        if ncol > 1 and (ref is gqk_ref or ref is gv_ref):
            _to_columns(r[:, o:o + wd], ref, col_scr, ncol)
        else:
            ref[0] = r[:, o:o + wd]
        o += wd
    sm = r[:, o:o + SMALL_W]

    ext_ref[0:halo, :] = jnp.where(j > 0, r_qkv[tm:tm + halo], 0.0)
    ext_ref[halo:halo + tm, :] = r_qkv[0:tm]
    ext_ref[halo + tm:2 * halo + tm, :] = jnp.where(j < nj - 1, r_qkv[tm + halo:], 0.0)
    acc = None
    for w in range(CONV_W):
        term = ext_ref[pl.ds(halo - pad + w, tm), :] * cw_ref[w:w + 1, :]
        acc = term if acc is None else acc + term
    s = _silu(acc)
    for hd in range(DN_HEADS):
        lo = hd * DN_HEAD_DIM
        qh = s[:, lo:lo + DN_HEAD_DIM]
        kh = s[:, DN_WIDTH + lo:DN_WIDTH + lo + DN_HEAD_DIM]
        q_ref[0, :, lo:lo + DN_HEAD_DIM] = (qh * lax.rsqrt(jnp.sum(qh * qh, axis=-1, keepdims=True) + NORM_EPS)
                                            * DN_HEAD_DIM ** -0.5)
        k_ref[0, :, lo:lo + DN_HEAD_DIM] = kh * lax.rsqrt(jnp.sum(kh * kh, axis=-1, keepdims=True) + NORM_EPS)
    v_ref[0] = s[:, 2 * DN_WIDTH:]
    lane = lax.broadcasted_iota(jnp.int32, sm.shape, 1)
    beta = _sigmoid(sm)
    gdec = ap_ref[0:1, :] * _softplus(sm + ap_ref[1:2, :])
    p = jnp.where(lane < GDEC_LANE, beta, jnp.where(lane < GLR_LANE, gdec, sm))
    p_ref[0] = p
    if ncol > 1:
        _to_columns(p, pcol_ref, col_scr, ncol)


def _proj(x, mods, mod_row, gain, w, cw, ap, tm, ncol):
    b_, t_, _ = x.shape
    r8 = tm // SUBLANES
    last8 = t_ // SUBLANES - 1
    tok = lambda wd: pl.BlockSpec((1, tm, wd), lambda b, j: (b, j, 0))
    colm = lambda wd: pl.BlockSpec((1, tm // ncol, ncol * wd), lambda b, j: (b, j, 0))
    const = lambda shp: pl.BlockSpec(shp, lambda b, j: (0,) * len(shp))
    out_specs = [tok(DN_WIDTH), tok(DN_WIDTH), tok(DN_WIDTH), tok(SMALL_W), tok(DN_WIDTH),
                 colm(2 * GLA_K_WIDTH), colm(GLA_V_WIDTH), tok(GLA_V_WIDTH)]
    tshape = lambda wd: jax.ShapeDtypeStruct((b_, t_, wd), F32)
    cshape = lambda wd: jax.ShapeDtypeStruct((b_, t_ // ncol, ncol * wd), F32)
    out_shape = [tshape(DN_WIDTH)] * 3 + [tshape(SMALL_W), tshape(DN_WIDTH), cshape(2 * GLA_K_WIDTH),
                                          cshape(GLA_V_WIDTH), tshape(GLA_V_WIDTH)]
    scratch = [pltpu.VMEM((tm + 2 * SUBLANES, QKV_W), F32)]
    if ncol > 1:
        out_specs.append(colm(SMALL_W))
        out_shape.append(cshape(SMALL_W))
        scratch.append(_col_scratch(GLA_V_WIDTH // LANES, tm // ncol, ncol))
    outs = pl.pallas_call(
        functools.partial(_proj_kernel, ncol=ncol),
        grid=(b_, t_ // tm),
        in_specs=[tok(D_MODEL),
                  pl.BlockSpec((1, SUBLANES, D_MODEL), lambda b, j: (b, jnp.maximum(j * r8 - 1, 0), 0)),
                  pl.BlockSpec((1, SUBLANES, D_MODEL), lambda b, j: (b, jnp.minimum((j + 1) * r8, last8), 0)),
                  pl.BlockSpec((1, SUBLANES, D_MODEL), lambda b, j: (mod_row(b), 0, 0)),
                  const((1, D_MODEL)), const((D_MODEL, PROJ_COLS)), const((SUBLANES, QKV_W)),
                  const((SUBLANES, SMALL_W))],
        out_specs=out_specs,
        out_shape=out_shape,
        scratch_shapes=scratch,
        compiler_params=_cparams(("parallel", "parallel")),
        name="proj",
    )(x, x, x, mods, gain.reshape(1, D_MODEL), w, cw, ap)
    q, k, v, p, dnz, gqk, gv, gz = outs[:8]
    p_col = outs[8] if ncol > 1 else p
    return q, k, v, p, p_col, dnz, gqk, gv, gz


def _order_masks(n, rev):
    ri = lax.broadcasted_iota(jnp.int32, (n, n), 0)
    ci = lax.broadcasted_iota(jnp.int32, (n, n), 1)
    if rev:
        return ri, ci, ci >= ri, ci > ri
    return ri, ci, ci <= ri, ci < ri


def _dn_kernel(qf_ref, kf_ref, vf_ref, pf_ref, qb_ref, kb_ref, vb_ref, pb_ref, s0_ref, of_ref, ob_ref, st_ref, s_scr):
    j = pl.program_id(1)
    nj = pl.num_programs(1)

    @pl.when(j == 0)
    def _():
        s_scr[...] = s0_ref[0]

    c_ = CHUNK
    tb = qf_ref.shape[1]
    nchunk = tb // c_
    streams = ((qf_ref, kf_ref, vf_ref, pf_ref, of_ref), (qb_ref, kb_ref, vb_ref, pb_ref, ob_ref))
    masks, gcum, gcum_t, order = [], [], [], []
    for d, rev in enumerate((False, True)):
        ri, ci, incl, strict = _order_masks(c_, rev)
        levels = []
        s = 1
        while s < c_:
            levels.append(strict & ((ri // (2 * s)) == (ci // (2 * s))) & ((ri // s) != (ci // s)))
            s *= 2
        masks.append(dict(incl=incl, strict=strict, eye=(ri == ci).astype(F32), levels=levels,
                          last=0 if rev else c_ - 1))
        rb, cb, incl_b, _ = _order_masks(tb, rev)
        cum_mat = incl_b & ((rb // c_) == (cb // c_))
        gcum.append(_dot_exact_lhs(cum_mat, streams[d][3][0], terms=3))
        gcum_t.append(gcum[d].T)
        order.append([nchunk - 1 - cc if rev else cc for cc in range(nchunk)])

    inst = [(d, c, h) for step in range(nchunk) for d in range(2) for c in (order[d][step],)
            for h in range(DN_HEADS)]
    pre = {}
    for d, c, h in inst:
        q_ref, k_ref, v_ref, p_ref, _ = streams[d]
        m = masks[d]
        r0 = c * c_
        lo = h * DN_HEAD_DIM
        lb = BETA_LANE + d * DN_HEADS + h
        lg = GDEC_LANE + d * DN_HEADS + h
        beta = p_ref[0, r0:r0 + c_, lb:lb + 1]
        gc = gcum[d][r0:r0 + c_, lg:lg + 1]
        gr = gcum_t[d][lg:lg + 1, r0:r0 + c_]
        glast = gcum[d][r0 + m["last"]:r0 + m["last"] + 1, lg:lg + 1]
        decay = jnp.exp(jnp.where(m["incl"], gc - gr, -jnp.inf))
        qh = q_ref[0, r0:r0 + c_, lo:lo + DN_HEAD_DIM]
        kh = k_ref[0, r0:r0 + c_, lo:lo + DN_HEAD_DIM]
        vh = v_ref[0, r0:r0 + c_, lo:lo + DN_HEAD_DIM]
        kb = kh * beta
        eg = jnp.exp(gc)
        pre[d, c, h] = dict(decay=decay, kq_lhs=jnp.concatenate([kb, qh], axis=0), kh=kh,
                            rhs=jnp.concatenate([vh * beta, kb * eg], axis=1), q_dec=qh * eg,
                            k_dec=kh * jnp.exp(glast - gc), e_last=jnp.exp(glast))
    kq = {i: _bdot_nt(pre[i]["kq_lhs"], pre[i]["kh"]) for i in inst}
    a = {i: jnp.where(masks[i[0]]["strict"], kq[i][:c_] * pre[i]["decay"], 0.0) for i in inst}
    attn = {i: kq[i][c_:] * pre[i]["decay"] for i in inst}
    t = {i: masks[i[0]]["eye"] - jnp.where(masks[i[0]]["levels"][0], a[i], 0.0) for i in inst}
    for lvl in range(1, len(masks[0]["levels"])):
        ta = {i: _bdot(t[i], jnp.where(masks[i[0]]["levels"][lvl], a[i], 0.0)) for i in inst}
        t = {i: t[i] - _bdot(ta[i], t[i]) for i in inst}
    uw = {i: _bdot(t[i], pre[i]["rhs"]) for i in inst}
    for step in range(nchunk):
        group = [(d, order[d][step], h) for d in range(2) for h in range(DN_HEADS)]
        st = {i: s_scr[i[0], i[2]] for i in group}
        ws = {i: _bdot(jnp.concatenate([uw[i][:, DN_HEAD_DIM:], pre[i]["q_dec"]], axis=0), st[i]) for i in group}
        v_new = {i: uw[i][:, :DN_HEAD_DIM] - ws[i][:c_] for i in group}
        for i in group:
            d, c, h = i
            streams[d][4][0, c * c_:(c + 1) * c_, h * DN_HEAD_DIM:(h + 1) * DN_HEAD_DIM] = (
                ws[i][c_:] + _bdot(attn[i], v_new[i]))
        for i in group:
            s_scr[i[0], i[2]] = st[i] * pre[i]["e_last"] + _bdot_tn(pre[i]["k_dec"], v_new[i])

    @pl.when(j == nj - 1)
    def _():
        st_ref[0] = s_scr[...]


def _dn_scan(q, k, v, p, s0, tb):
    b_, t_, _ = q.shape
    nb = t_ // tb
    fwd = lambda wd: pl.BlockSpec((1, tb, wd), lambda b, j: (b, j, 0))
    bwd = lambda wd: pl.BlockSpec((1, tb, wd), lambda b, j: (b, nb - 1 - j, 0))
    widths = (DN_WIDTH, DN_WIDTH, DN_WIDTH, SMALL_W)
    st_shape = (2, DN_HEADS, DN_HEAD_DIM, DN_HEAD_DIM)
    st_spec = pl.BlockSpec((1,) + st_shape, lambda b, j: (b, 0, 0, 0, 0))
    return pl.pallas_call(
        _dn_kernel,
        grid=(b_, nb),
        in_specs=[fwd(w) for w in widths] + [bwd(w) for w in widths] + [st_spec],
        out_specs=[fwd(DN_WIDTH), bwd(DN_WIDTH), st_spec],
        out_shape=[jax.ShapeDtypeStruct((b_, t_, DN_WIDTH), F32)] * 2 + [jax.ShapeDtypeStruct((b_,) + st_shape, F32)],
        scratch_shapes=[pltpu.VMEM(st_shape, F32)],
        compiler_params=_cparams(("parallel", "arbitrary")),
        name="dn",
    )(q, k, v, p, q, k, v, p, s0)


def _log_sigmoid(t):
    return jnp.minimum(t, 0.0) - jnp.log(1.0 + jnp.exp(-jnp.abs(t)))


def _gla_kernel(qkf_ref, vf_ref, pf_ref, qkb_ref, vb_ref, pb_ref, wg_ref, bg_ref, s0_ref, of_ref, ob_ref, st_ref,
                s_scr):
    col = pl.program_id(1)
    rb = pl.program_id(2)
    ncol = pl.num_programs(1)
    nrb = pl.num_programs(2)

    @pl.when((col == 0) & (rb == 0))
    def _():
        s_scr[...] = s0_ref[0]

    c_ = CHUNK
    nchunk = qkf_ref.shape[1] // c_
    ncb = pf_ref.shape[2] // SMALL_W
    nsub = c_ // SUB
    streams = ((qkf_ref, vf_ref, pf_ref, of_ref), (qkb_ref, vb_ref, pb_ref, ob_ref))
    masks, order = [], []
    for rev in (False, True):
        ri, ci, incl, _ = _order_masks(c_, rev)
        pos_r = (c_ - 1 - ri) if rev else ri
        pos_c = (c_ - 1 - ci) if rev else ci
        first_rows = [(c_ - 1 - SUB * i) if rev else SUB * i for i in range(nsub)]
        masks.append(dict(incl=incl, diag=incl & ((pos_r // SUB) == (pos_c // SUB)),
                          off=(pos_c // SUB) < (pos_r // SUB), last=0 if rev else c_ - 1, first_rows=first_rows,
                          rev=rev))
        cols = [ncb - 1 - i if rev else i for i in range(ncb)]
        chunks = [nchunk - 1 - cc if rev else cc for cc in range(nchunk)]
        order.append([(cb, c) for cb in cols for c in chunks])
    nstep = len(order[0])
    la_col = {}
    for d in range(2):
        for cb in range(ncb):
            gate = _dot_split(streams[d][2][0, :, cb * SMALL_W:(cb + 1) * SMALL_W], wg_ref[d]) + bg_ref[d]
            la_col[d, cb] = _log_sigmoid(gate) * (1.0 / GLA_GATE_NORMALIZER)
    pre = {}
    for step in range(nstep):
        for d in range(2):
            cb, c = order[d][step]
            qk_ref, v_ref, _, _ = streams[d]
            m = masks[d]
            r0 = c * c_
            la = la_col[d, cb][r0:r0 + c_, :]
            bc = _dot_exact_lhs(m["incl"], la, terms=3)
            bex = bc - la
            b_last = bc[m["last"]:m["last"] + 1, :]
            pieces = [jnp.broadcast_to(bex[fr:fr + 1, :], (SUB, GLA_K_WIDTH)) for fr in m["first_rows"]]
            refd = jnp.concatenate(pieces[::-1] if m["rev"] else pieces, axis=0)
            mid = m["first_rows"][nsub // 2]
            rmid = bex[mid:mid + 1, :]
            qk0 = cb * 2 * GLA_K_WIDTH
            qa = qk_ref[0, r0:r0 + c_, qk0:qk0 + GLA_K_WIDTH] * GLA_DK ** -0.5
            ka = qk_ref[0, r0:r0 + c_, qk0 + GLA_K_WIDTH:qk0 + 2 * GLA_K_WIDTH]
            q_in = qa * jnp.exp(bc)
            k_st = ka * jnp.exp(b_last - bc)
            q_d = qa * jnp.exp(bc - refd)
            k_d = ka * jnp.exp(jnp.minimum(refd - bc, EXP_CLAMP))
            q_o = qa * jnp.exp(jnp.minimum(bc - rmid, EXP_CLAMP))
            k_o = ka * jnp.exp(jnp.minimum(rmid - bc, EXP_CLAMP))
            e_last = jnp.exp(b_last)
            v0 = cb * GLA_V_WIDTH
            for h in range(GLA_HEADS):
                ks = slice(h * GLA_DK, (h + 1) * GLA_DK)
                vh = v_ref[0, r0:r0 + c_, v0 + h * GLA_DV:v0 + (h + 1) * GLA_DV]
                attn = (jnp.where(m["diag"], _bdot_nt(q_d[:, ks], k_d[:, ks]), 0.0)
                        + jnp.where(m["off"], _bdot_nt(q_o[:, ks], k_o[:, ks]), 0.0))
                pre[d, step, h] = dict(attn=attn, vh=vh, q_in=q_in[:, ks], e_last=e_last[:, ks],
                                       kv=_bdot_tn(vh, k_st[:, ks]))
    st = {(d, h): s_scr[d, h] for d in range(2) for h in range(GLA_HEADS)}
    for step in range(nstep):
        for d in range(2):
            cb, c = order[d][step]
            r0 = c * c_
            v0 = cb * GLA_V_WIDTH
            for h in range(GLA_HEADS):
                i = (d, step, h)
                streams[d][3][0, r0:r0 + c_, v0 + h * GLA_DV:v0 + (h + 1) * GLA_DV] = (
                    _bdot_nt(pre[i]["q_in"], st[d, h]) + _bdot(pre[i]["attn"], pre[i]["vh"]))
                st[d, h] = st[d, h] * pre[i]["e_last"] + pre[i]["kv"]
    for d in range(2):
        for h in range(GLA_HEADS):
            s_scr[d, h] = st[d, h]

    @pl.when((col == ncol - 1) & (rb == nrb - 1))
    def _():
        st_ref[0] = s_scr[...]


def _gla_scan(qk, v, p, wg, bg, s0, ncol, tr):
    b_, rows, _ = qk.shape
    nrb = rows // tr
    ncb = min(GLA_COLS_PER_STEP, ncol)
    ncblk = ncol // ncb
    assert ncb == 1 or nrb == 1
    fwd = lambda wd: pl.BlockSpec((1, tr, ncb * wd), lambda b, c, r: (b, r, c))
    bwd = lambda wd: pl.BlockSpec((1, tr, ncb * wd), lambda b, c, r: (b, nrb - 1 - r, ncblk - 1 - c))
    widths = (2 * GLA_K_WIDTH, GLA_V_WIDTH, SMALL_W)
    st_shape = (2, GLA_HEADS, GLA_DV, GLA_DK)
    st_spec = pl.BlockSpec((1,) + st_shape, lambda b, c, r: (b, 0, 0, 0, 0))
    return pl.pallas_call(
        _gla_kernel,
        grid=(b_, ncblk, nrb),
        in_specs=[fwd(w) for w in widths] + [bwd(w) for w in widths] + [
            pl.BlockSpec((2, SMALL_W, GLA_K_WIDTH), lambda b, c, r: (0, 0, 0)),
            pl.BlockSpec((2, 1, GLA_K_WIDTH), lambda b, c, r: (0, 0, 0)),
            st_spec],
        out_specs=[fwd(GLA_V_WIDTH), bwd(GLA_V_WIDTH), st_spec],
        out_shape=[jax.ShapeDtypeStruct((b_, rows, ncol * GLA_V_WIDTH), F32)] * 2 + [
            jax.ShapeDtypeStruct((b_,) + st_shape, F32)],
        scratch_shapes=[pltpu.VMEM(st_shape, F32)],
        compiler_params=_cparams(("parallel", "arbitrary", "arbitrary")),
        name="gla",
    )(qk, v, p, qk, v, p, wg, bg, s0)


def _head_norm_gate(o, z, gain):
    parts = []
    for h in range(o.shape[1] // LANES):
        hs = slice(h * LANES, (h + 1) * LANES)
        parts.append(_rms(o[:, hs], gain) * _silu(z[:, hs]))
    return jnp.concatenate(parts, axis=1)


def _mixout_kernel(*refs, with_router, ncol):
    (x_ref, odf_ref, odb_ref, dnz_ref, ogf_ref, ogb_ref, glz_ref, mod_ref, gd_ref, gg_ref, w_ref, gf_ref) = refs[:12]
    refs = list(refs[12:])
    col_scr = refs.pop() if ncol > 1 else None
    if with_router:
        rw_ref, xo_ref, h_ref, cmb_ref = refs
    else:
        xo_ref, h_ref = refs
    m_dn = _head_norm_gate(odf_ref[0] + odb_ref[0], dnz_ref[0], gd_ref[...])
    o_gl = ogf_ref[0] + ogb_ref[0]
    if ncol > 1:
        o_gl = _from_columns(o_gl, col_scr, ncol)
    m_gl = _head_norm_gate(o_gl, glz_ref[0], gg_ref[...])
    m = jnp.concatenate([m_dn, m_gl], axis=1).astype(BF16)
    x = x_ref[0] + mod_ref[0, 2:3, :] * jnp.dot(m, w_ref[...], preferred_element_type=F32)
    xo_ref[0] = x
    h = _rms(x, gf_ref[...]) * (1.0 + mod_ref[0, 4:5, :]) + mod_ref[0, 3:4, :]
    h_ref[0] = h.astype(BF16)
    if with_router:
        logits = _dot_split(h, rw_ref[...])
        lane = lax.broadcasted_iota(jnp.int32, logits.shape, 1).astype(F32)
        logits = jnp.where(lane < N_EXPERTS, logits, -jnp.inf)
        e = jnp.exp(logits - jnp.max(logits, axis=-1, keepdims=True))
        probs = e / jnp.sum(e, axis=-1, keepdims=True)
        p1 = jnp.max(probs, axis=-1, keepdims=True)
        i1 = jnp.min(jnp.where(probs == p1, lane, float(LANES)), axis=-1, keepdims=True)
        rest = jnp.where(lane == i1, -1.0, probs)
        p2 = jnp.max(rest, axis=-1, keepdims=True)
        i2 = jnp.min(jnp.where(rest == p2, lane, float(LANES)), axis=-1, keepdims=True)
        cmb_ref[0] = (jnp.where(lane == i1, p1, 0.0) + jnp.where(lane == i2, p2, 0.0)) / (p1 + p2)


def _mixout(x, odf, odb, dnz, ogf, ogb, glz, mods, mod_row, gd, gg, w_out, gf, router, tm, ncol):
    b_, t_, _ = x.shape
    with_router = router is not None
    tok = lambda wd: pl.BlockSpec((1, tm, wd), lambda b, j: (b, j, 0))
    colm = lambda wd: pl.BlockSpec((1, tm // ncol, ncol * wd), lambda b, j: (b, j, 0))
    const = lambda shp: pl.BlockSpec(shp, lambda b, j: (0,) * len(shp))
    in_specs = [tok(D_MODEL)] + [tok(DN_WIDTH)] * 3 + [colm(GLA_V_WIDTH)] * 2 + [tok(GLA_V_WIDTH)] + [
        pl.BlockSpec((1, SUBLANES, D_MODEL), lambda b, j: (mod_row(b), 0, 0)),
        const((1, LANES)), const((1, LANES)), const((D_MODEL, D_MODEL)), const((1, D_MODEL))]
    args = [x, odf, odb, dnz, ogf, ogb, glz, mods, gd.reshape(1, LANES), gg.reshape(1, LANES), w_out,
            gf.reshape(1, D_MODEL)]
    out_specs = [tok(D_MODEL), tok(D_MODEL)]
    out_shape = [jax.ShapeDtypeStruct((b_, t_, D_MODEL), F32), jax.ShapeDtypeStruct((b_, t_, D_MODEL), BF16)]
    if with_router:
        in_specs.append(const((D_MODEL, LANES)))
        args.append(router)
        out_specs.append(tok(LANES))
        out_shape.append(jax.ShapeDtypeStruct((b_, t_, LANES), F32))
    scratch = [_col_scratch(GLA_V_WIDTH // LANES, tm // ncol, ncol)] if ncol > 1 else []
    return pl.pallas_call(
        functools.partial(_mixout_kernel, with_router=with_router, ncol=ncol),
        grid=(b_, t_ // tm),
        in_specs=in_specs,
        out_specs=out_specs,
        out_shape=out_shape,
        scratch_shapes=scratch,
        compiler_params=_cparams(("parallel", "parallel")),
        name="mixout",
    )(*args)


def _ffn_kernel(x_ref, h_ref, mod_ref, wg_ref, wu_ref, wd_ref, o_ref, *, nsplit):
    h = h_ref[0]
    f = wg_ref.shape[1]
    fs = f // nsplit
    acc = None
    for s in range(nsplit):
        g = jnp.dot(h, wg_ref[:, s * fs:(s + 1) * fs], preferred_element_type=F32)
        u = jnp.dot(h, wu_ref[:, s * fs:(s + 1) * fs], preferred_element_type=F32)
        part = jnp.dot((_silu(g) * u).astype(BF16), wd_ref[s * fs:(s + 1) * fs, :], preferred_element_type=F32)
        acc = part if acc is None else acc + part
    o_ref[0] = x_ref[0] + mod_ref[0, 5:6, :] * acc


def _ffn(x, h, mods, mod_row, w_gu, w_down, tm):
    b_, t_, _ = x.shape
    f = w_down.shape[0]
    tok = lambda: pl.BlockSpec((1, tm, D_MODEL), lambda b, j: (b, j, 0))
    return pl.pallas_call(
        functools.partial(_ffn_kernel, nsplit=2),
        grid=(b_, t_ // tm),
        in_specs=[tok(), tok(),
                  pl.BlockSpec((1, SUBLANES, D_MODEL), lambda b, j: (mod_row(b), 0, 0)),
                  pl.BlockSpec((D_MODEL, f), lambda b, j: (0, 0)),
                  pl.BlockSpec((D_MODEL, f), lambda b, j: (0, 1)),
                  pl.BlockSpec((f, D_MODEL), lambda b, j: (0, 0))],
        out_specs=tok(),
        out_shape=jax.ShapeDtypeStruct((b_, t_, D_MODEL), F32),
        compiler_params=_cparams(("parallel", "parallel")),
        name="ffn",
    )(x, h, mods, w_gu, w_gu, w_down)


def _for_row_tiles(ntiles, fn):
    rows = MOE_GROUP * MOE_TILE
    single = ntiles == MOE_GROUP + 1
    keep = 1 - single.astype(jnp.int32)
    ngroup = (ntiles // MOE_GROUP) * keep
    rest = (ntiles % MOE_GROUP) * keep

    def body(i, carry):
        fn(pl.multiple_of(i * rows, rows), rows)
        return carry

    lax.fori_loop(0, ngroup, body, 0)
    for k in range(1, MOE_GROUP):
        @pl.when(rest == k)
        def _():
            fn(pl.multiple_of(ngroup * rows, rows), k * MOE_TILE)

    @pl.when(single)
    def _():
        fn(0, rows + MOE_TILE)


def _moe_kernel(cnt_ref, x_ref, h_ref, cmb_ref, mod_ref, wg_ref, wu_ref, wd_ref, fg_ref, o_ref,
                acc_ref, xe_ref, ye_ref, rkc_ref, rkr_ref, *, final_norm):
    b = pl.program_id(0)
    j = pl.program_id(1)
    e = pl.program_id(2)
    f = pl.program_id(3)
    nj = pl.num_programs(1)
    ne = pl.num_programs(2)
    nf = pl.num_programs(3)
    tm = h_ref.shape[1]
    ntiles = (cnt_ref[(b * nj + j) * ne + e] + MOE_TILE - 1) // MOE_TILE

    @pl.when((e == 0) & (f == 0))
    def _():
        acc_ref[...] = jnp.zeros_like(acc_ref)
        sb = 2 * LANES
        ri = lax.broadcasted_iota(jnp.int32, (sb, sb), 0)
        ci = lax.broadcasted_iota(jnp.int32, (sb, sb), 1)
        before = (ci < ri).astype(BF16)
        carry = jnp.zeros((1, LANES), F32)
        for s in range(tm // sb):
            sel = (cmb_ref[0, s * sb:(s + 1) * sb, :] > 0.0).astype(F32)
            rank = jnp.dot(before, sel.astype(BF16), preferred_element_type=F32) + carry
            rkc_ref[s * sb:(s + 1) * sb, :] = jnp.where(sel > 0.0, rank, -1.0)
            carry = carry + jnp.sum(sel, axis=0, keepdims=True)
        rkr_ref[...] = rkc_ref[...].T

    @pl.when(f == 0)
    def _():
        rank_row = rkr_ref[pl.ds(e, 1), :]

        def gather(r0, n):
            slot = (lax.broadcasted_iota(jnp.int32, (n, tm), 0) + r0).astype(F32)
            onehot = (rank_row == slot).astype(BF16)
            xe_ref[pl.ds(r0, n), :] = jnp.dot(onehot, h_ref[0], preferred_element_type=F32).astype(BF16)

        _for_row_tiles(ntiles, gather)

    def expert(r0, n):
        xr = xe_ref[pl.ds(r0, n), :]
        g = jnp.dot(xr, wg_ref[0], preferred_element_type=F32)
        u = jnp.dot(xr, wu_ref[0], preferred_element_type=F32)
        y = jnp.dot((_silu(g) * u).astype(BF16), wd_ref[0], preferred_element_type=F32)

        @pl.when(f == 0)
        def _():
            ye_ref[pl.ds(r0, n), :] = y

        @pl.when(f > 0)
        def _():
            ye_ref[pl.ds(r0, n), :] += y

    _for_row_tiles(ntiles, expert)

    @pl.when(f == nf - 1)
    def _():
        lane = lax.broadcasted_iota(jnp.int32, (tm, LANES), 1)
        rank_col = jnp.sum(jnp.where(lane == e, rkc_ref[...], 0.0), axis=-1, keepdims=True)
        ce = jnp.sum(jnp.where(lane == e, cmb_ref[0], 0.0), axis=-1, keepdims=True)

        def scatter(r0, n):
            slot = (lax.broadcasted_iota(jnp.int32, (tm, n), 1) + r0).astype(F32)
            onehot = (rank_col == slot).astype(BF16)
            back = jnp.dot(onehot, ye_ref[pl.ds(r0, n), :].astype(BF16), preferred_element_type=F32)
            acc_ref[...] += ce * back

        _for_row_tiles(ntiles, scatter)

    @pl.when((e == ne - 1) & (f == nf - 1))
    def _():
        y = x_ref[0] + mod_ref[0, 5:6, :] * acc_ref[...]
        if final_norm:
            y = _rms(y, fg_ref[...])
        o_ref[0] = y


def _moe(x, h, cmb, mods, mod_row, w_gu, w_down, final_gain, tm, tf):
    b_, t_, _ = x.shape
    ne, _, f2 = w_gu.shape
    fe = f2 // 2
    assert fe % tf == 0 and tm % (MOE_GROUP * MOE_TILE) == 0 and ne <= SUBLANES
    nf = fe // tf
    nj = t_ // tm
    counts = jnp.sum((cmb[..., :ne] > 0.0).reshape(b_, nj, tm, ne), axis=2, dtype=jnp.int32).reshape(-1)
    tok = lambda wd: pl.BlockSpec((1, tm, wd), lambda b, j, e, f, cnt: (b, j, 0))
    final_norm = final_gain is not None
    fg = (final_gain if final_norm else jnp.ones((D_MODEL,), F32)).reshape(1, D_MODEL)
    grid_spec = pltpu.PrefetchScalarGridSpec(
        num_scalar_prefetch=1,
        grid=(b_, nj, ne, nf),
        in_specs=[tok(D_MODEL), tok(D_MODEL), tok(LANES),
                  pl.BlockSpec((1, SUBLANES, D_MODEL), lambda b, j, e, f, cnt: (mod_row(b), 0, 0)),
                  pl.BlockSpec((1, D_MODEL, tf), lambda b, j, e, f, cnt: (e, 0, f)),
                  pl.BlockSpec((1, D_MODEL, tf), lambda b, j, e, f, cnt: (e, 0, nf + f)),
                  pl.BlockSpec((1, tf, D_MODEL), lambda b, j, e, f, cnt: (e, f, 0)),
                  pl.BlockSpec((1, D_MODEL), lambda b, j, e, f, cnt: (0, 0))],
        out_specs=tok(D_MODEL),
        scratch_shapes=[pltpu.VMEM((tm, D_MODEL), F32),
                        pltpu.VMEM((tm, D_MODEL), BF16),
                        pltpu.VMEM((tm, D_MODEL), F32),
                        pltpu.VMEM((tm, LANES), F32),
                        pltpu.VMEM((LANES, tm), F32)])
    return pl.pallas_call(
        functools.partial(_moe_kernel, final_norm=final_norm),
        grid_spec=grid_spec,
        out_shape=jax.ShapeDtypeStruct((b_, t_, D_MODEL), F32),
        compiler_params=_cparams(("parallel", "parallel", "arbitrary", "arbitrary")),
        name="moe",
    )(counts, x, h, cmb, mods, w_gu, w_gu, w_down, fg)


def _pack_w_in(w):
    offs = np.cumsum((0, QKV_W, DN_WIDTH, 2 * DN_HEADS, 2 * DN_HEADS, GLA_K_WIDTH, GLA_K_WIDTH, GLA_V_WIDTH,
                      GLA_V_WIDTH, 2 * GLA_GATE_RANK))
    seg = lambda i: w[:, offs[i]:offs[i + 1]]
    small = jnp.concatenate([seg(2), seg(3), seg(8)], axis=1)
    small = jnp.pad(small, ((0, 0), (0, SMALL_W - small.shape[1])))
    return jnp.concatenate([seg(0), seg(1), seg(4), seg(5), seg(6), seg(7), small], axis=1).astype(BF16)


def _pack_gate(w_gate2, b_gate):
    ws = []
    for d in range(2):
        lo = GLR_LANE + d * GLA_GATE_RANK
        ws.append(jnp.zeros((SMALL_W, GLA_K_WIDTH), F32).at[lo:lo + GLA_GATE_RANK].set(w_gate2[d]))
    return jnp.stack(ws), b_gate.reshape(2, 1, GLA_K_WIDTH)


def _pack_dn_params(a_log, dt_bias):
    neg_a = jnp.zeros((SMALL_W,), F32).at[GDEC_LANE:GLR_LANE].set(-jnp.exp(a_log.reshape(-1)))
    dtb = jnp.zeros((SMALL_W,), F32).at[GDEC_LANE:GLR_LANE].set(dt_bias.reshape(-1))
    return jnp.zeros((SUBLANES, SMALL_W), F32).at[0].set(neg_a).at[1].set(dtb)


def _pad_rows(a, rows):
    return jnp.pad(a, ((0, rows - a.shape[0]),) + ((0, 0),) * (a.ndim - 1))


def _mixers(x_seq, mods, mod_row, gain_mix, w_in_p, cw, ap, wgs, bgs, states, ncol, tm, tb):
    b_, t_, _ = x_seq.shape
    q, k, v, p, p_col, dnz, gqk, gv, gz = _proj(x_seq, mods, mod_row, gain_mix, w_in_p, cw, ap, tm, ncol)
    rows = t_ // ncol
    o_dn_f, o_dn_b, s_dn = _dn_scan(q, k, v, p, states[0], tb)
    o_gl_f, o_gl_b, s_gl = _gla_scan(gqk, gv, p_col, wgs, bgs, states[1], ncol, min(rows, 2 * CHUNK))
    outs = [(o_dn_f, o_gl_f), (o_dn_b, o_gl_b)]
    return outs, dnz, gz, (s_dn, s_gl)


def kernel(x, c, ctx, c_ctx, w_ada, b_ada, norm_mix, norm_ffn, w_in, conv_qkv, dn_a_log, dn_dt_bias, dn_norm,
           gla_w_gate2, gla_b_gate, gla_norm, w_out, ffn_w_gu, ffn_w_down, moe_router, moe_w_gu, moe_w_down,
           final_norm):
    b_, seq, _ = x.shape
    depth = w_ada.shape[0]
    ctx_len = ctx.shape[1]
    mod_rows = -(-(b_ + 1) // SUBLANES) * SUBLANES
    c_rows = _pad_rows(jnp.concatenate([c, c_ctx[None, :]], axis=0), mod_rows)
    lat_row = lambda b: b
    ctx_row = lambda b: b_
    tm_lat, tm_ctx = 512, ctx_len
    tb = 4 * CHUNK

    for i in range(depth):
        last = i == depth - 1
        mods = _ada(c_rows, w_ada[i], b_ada[i]).reshape(mod_rows, N_MOD, D_MODEL)
        mods = jnp.pad(mods, ((0, 0), (0, SUBLANES - N_MOD), (0, 0)))
        w_in_p = _pack_w_in(w_in[i])
        cw = _pad_rows(conv_qkv[i], SUBLANES)
        ap = _pack_dn_params(dn_a_log[i], dn_dt_bias[i])
        wgs, bgs = _pack_gate(gla_w_gate2[i], gla_b_gate[i])
        s0 = (jnp.zeros((b_, 2, DN_HEADS, DN_HEAD_DIM, DN_HEAD_DIM), F32),
              jnp.zeros((b_, 2, GLA_HEADS, GLA_DV, GLA_DK), F32))
        outs_c, dnz_c, gz_c, s_ctx = _mixers(ctx, mods, ctx_row, norm_mix[i], w_in_p, cw, ap, wgs, bgs, s0, 1,
                                             tm_ctx, tb)
        outs_l, dnz_l, gz_l, _ = _mixers(x, mods, lat_row, norm_mix[i], w_in_p, cw, ap, wgs, bgs, s_ctx, GRID_W,
                                         tm_lat, tb)
        w_out_b = w_out[i].astype(BF16)
        is_moe = i % 2 == 1
        if is_moe:
            router = jnp.pad(moe_router[i // 2], ((0, 0), (0, LANES - N_EXPERTS)))
            w_gu = moe_w_gu[i // 2].astype(BF16)
            w_dn = moe_w_down[i // 2].astype(BF16)
        else:
            router = None
            w_gu = ffn_w_gu[i // 2].astype(BF16)
            w_dn = ffn_w_down[i // 2].astype(BF16)

        def channel(tokens, outs, dnz, gz, row, tm, ncol, final_gain):
            res = _mixout(tokens, outs[0][0], outs[1][0], dnz, outs[0][1], outs[1][1], gz, mods, row, dn_norm[i],
                          gla_norm[i], w_out_b, norm_ffn[i], router, tm, ncol)
            if is_moe:
                x_mid, h, cmb = res
                tm_moe = min(1024, tokens.shape[1])
                return _moe(x_mid, h, cmb, mods, row, w_gu, w_dn, final_gain, tm_moe, D_EXPERT_TILE)
            x_mid, h = res
            y = _ffn(x_mid, h, mods, row, w_gu, w_dn, tm)
            if final_gain is not None:
                raise NotImplementedError("final norm is fused into the expert mixer only")
            return y

        x = channel(x, outs_l, dnz_l, gz_l, lat_row, tm_lat, GRID_W, final_norm if last else None)
        if not last:
            ctx = channel(ctx, outs_c, dnz_c, gz_c, ctx_row, tm_ctx, 1, None)
    return x
```

```python
import functools

import jax
import jax.numpy as jnp
import numpy as np
from jax import lax
from jax.experimental import pallas as pl
from jax.experimental.pallas import tpu as pltpu

F32 = jnp.float32
BF16 = jnp.bfloat16
HIGHEST = lax.Precision.HIGHEST

D_MODEL = 1024
GRID_W = 64
DN_HEADS = 4
DN_HEAD_DIM = 128
DN_WIDTH = DN_HEADS * DN_HEAD_DIM
CONV_W = 5
GLA_HEADS = 4
GLA_DK = 64
GLA_DV = 128
GLA_K_WIDTH = GLA_HEADS * GLA_DK
GLA_V_WIDTH = GLA_HEADS * GLA_DV
GLA_GATE_RANK = 16
GLA_GATE_NORMALIZER = 16.0
N_EXPERTS = 8
N_MOD = 6
NORM_EPS = 1e-6

LANES = 128
SUBLANES = 8
CHUNK = 64
SUB = 16
EXP_CLAMP = 80.0
D_EXPERT_TILE = 896
MOE_TILE = 64
MOE_GROUP = 4
GLA_COLS_PER_STEP = 8
VMEM_LIMIT = 56 * 1024 * 1024

QKV_W = 3 * DN_WIDTH
SMALL_W = LANES
PROJ_WIDTHS = (QKV_W, DN_WIDTH, 2 * GLA_K_WIDTH, GLA_V_WIDTH, GLA_V_WIDTH, SMALL_W)
PROJ_COLS = sum(PROJ_WIDTHS)
BETA_LANE = 0
GDEC_LANE = 2 * DN_HEADS
GLR_LANE = 4 * DN_HEADS


def _cparams(sem, vmem=VMEM_LIMIT):
    return pltpu.CompilerParams(dimension_semantics=sem, vmem_limit_bytes=vmem)


def _sigmoid(t):
    return 1.0 / (1.0 + jnp.exp(-t))


def _silu(t):
    return t * _sigmoid(t)


def _softplus(t):
    return jnp.maximum(t, 0.0) + jnp.log(1.0 + jnp.exp(-jnp.abs(t)))


def _bdot(a, b):
    return jnp.dot(a.astype(BF16), b.astype(BF16), preferred_element_type=F32)


def _bdot_nt(a, b):
    return lax.dot_general(a.astype(BF16), b.astype(BF16), (((1,), (1,)), ((), ())), preferred_element_type=F32)


def _bdot_tn(a, b):
    return lax.dot_general(a.astype(BF16), b.astype(BF16), (((0,), (0,)), ((), ())), preferred_element_type=F32)


def _split_bf16(t, terms):
    parts = []
    for _ in range(terms):
        part = t.astype(BF16)
        parts.append(part)
        t = t - part.astype(F32)
    return parts


def _dot_exact_lhs(a, x, terms=2):
    a = a.astype(BF16)
    out = None
    for part in _split_bf16(x, terms):
        d = jnp.dot(a, part, preferred_element_type=F32)
        out = d if out is None else out + d
    return out


def _dot_split(a, b):
    a_hi, a_lo = _split_bf16(a, 2)
    b_hi, b_lo = _split_bf16(b, 2)
    dot = lambda u, v: jnp.dot(u, v, preferred_element_type=F32)
    return dot(a_hi, b_hi) + (dot(a_lo, b_hi) + dot(a_hi, b_lo))


def _rms(t, gain):
    return t * lax.rsqrt(jnp.mean(t * t, axis=-1, keepdims=True) + NORM_EPS) * gain


def _ada_kernel(c_ref, w_ref, b_ref, o_ref):
    s = _silu(c_ref[...])
    o_ref[...] = jnp.dot(s, w_ref[...], precision=HIGHEST, preferred_element_type=F32) + b_ref[...]


def _ada(c_rows, w, b):
    rows = c_rows.shape[0]
    n = w.shape[1]
    tn = D_MODEL
    return pl.pallas_call(
        _ada_kernel,
        grid=(n // tn,),
        in_specs=[pl.BlockSpec((rows, D_MODEL), lambda j: (0, 0)),
                  pl.BlockSpec((D_MODEL, tn), lambda j: (0, j)),
                  pl.BlockSpec((1, tn), lambda j: (0, j))],
        out_specs=pl.BlockSpec((rows, tn), lambda j: (0, j)),
        out_shape=jax.ShapeDtypeStruct((rows, n), F32),
        compiler_params=_cparams(("arbitrary",)),
        name="ada",
    )(c_rows, w, b.reshape(1, n))


def _to_columns(val, out_ref, scr_ref, ncol):
    nrows, width = val.shape
    groups = width // LANES
    grid_rows = nrows // ncol
    pitch = _col_pitch(ncol)
    for g in range(groups):
        for r in range(grid_rows):
            scr_ref[g, r * pitch:r * pitch + ncol, :] = val[r * ncol:(r + 1) * ncol, g * LANES:(g + 1) * LANES]
    for c in range(ncol):
        for g in range(groups):
            lo = c * width + g * LANES
            out_ref[0, :, lo:lo + LANES] = scr_ref[g, pl.ds(c, grid_rows, stride=pitch), :]


def _from_columns(in_val, scr_ref, ncol):
    grid_rows = in_val.shape[0]
    width = in_val.shape[1] // ncol
    groups = width // LANES
    pitch = _col_pitch(ncol)
    for c in range(ncol):
        for g in range(groups):
            lo = c * width + g * LANES
            scr_ref[g, pl.ds(c, grid_rows, stride=pitch), :] = in_val[:, lo:lo + LANES]
    return jnp.concatenate(
        [jnp.concatenate([scr_ref[g, r * pitch:r * pitch + ncol, :] for r in range(grid_rows)], axis=0)
         for g in range(groups)], axis=1)


def _col_pitch(ncol):
    return ncol + SUBLANES // 2


def _col_scratch(groups, grid_rows, ncol):
    rows = -(-grid_rows * _col_pitch(ncol) // SUBLANES) * SUBLANES
    return pltpu.VMEM((groups, rows, LANES), F32)


def _proj_kernel(x_ref, xp_ref, xn_ref, mod_ref, gain_ref, w_ref, cw_ref, ap_ref,
                 q_ref, k_ref, v_ref, p_ref, dnz_ref, gqk_ref, gv_ref, gz_ref, *rest, ncol):
    if ncol > 1:
        pcol_ref, ext_ref, col_scr = rest
    else:
        (ext_ref,) = rest
    j = pl.program_id(1)
    nj = pl.num_programs(1)
    tm = x_ref.shape[1]
    halo = SUBLANES
    pad = CONV_W // 2
    norm_mod = lambda t: _rms(t, gain_ref[...]) * (1.0 + mod_ref[0, 1:2, :]) + mod_ref[0, 0:1, :]
    h = norm_mod(x_ref[0])
    h_all = jnp.concatenate([h, norm_mod(xp_ref[0]), norm_mod(xn_ref[0])], axis=0).astype(BF16)
    r_qkv = jnp.dot(h_all, w_ref[:, 0:QKV_W], preferred_element_type=F32)
    r = jnp.dot(h.astype(BF16), w_ref[:, QKV_W:], preferred_element_type=F32)
    o = 0
    for ref, wd in zip((dnz_ref, gqk_ref, gv_ref, gz_ref), PROJ_WIDTHS[1:5]):
        if ncol > 1 and (ref is gqk_ref or ref is gv_ref):
            _to_columns(r[:, o:o + wd], ref, col_scr, ncol)
        else:
            ref[0] = r[:, o:o + wd].astype(ref.dtype)
        o += wd
    sm = r[:, o:o + SMALL_W]

    ext_ref[0:halo, :] = jnp.where(j > 0, r_qkv[tm:tm + halo], 0.0)
    ext_ref[halo:halo + tm, :] = r_qkv[0:tm]
    ext_ref[halo + tm:2 * halo + tm, :] = jnp.where(j < nj - 1, r_qkv[tm + halo:], 0.0)
    acc = None
    for w in range(CONV_W):
        term = ext_ref[pl.ds(halo - pad + w, tm), :] * cw_ref[w:w + 1, :]
        acc = term if acc is None else acc + term
    s = _silu(acc)
    for hd in range(DN_HEADS):
        lo = hd * DN_HEAD_DIM
        qh = s[:, lo:lo + DN_HEAD_DIM]
        kh = s[:, DN_WIDTH + lo:DN_WIDTH + lo + DN_HEAD_DIM]
        q_ref[0, :, lo:lo + DN_HEAD_DIM] = (qh * lax.rsqrt(jnp.sum(qh * qh, axis=-1, keepdims=True) + NORM_EPS)
                                            * DN_HEAD_DIM ** -0.5)
        k_ref[0, :, lo:lo + DN_HEAD_DIM] = kh * lax.rsqrt(jnp.sum(kh * kh, axis=-1, keepdims=True) + NORM_EPS)
    v_ref[0] = s[:, 2 * DN_WIDTH:]
    lane = lax.broadcasted_iota(jnp.int32, sm.shape, 1)
    beta = _sigmoid(sm)
    gdec = ap_ref[0:1, :] * _softplus(sm + ap_ref[1:2, :])
    p = jnp.where(lane < GDEC_LANE, beta, jnp.where(lane < GLR_LANE, gdec, sm))
    p_ref[0] = p
    if ncol > 1:
        _to_columns(p, pcol_ref, col_scr, ncol)


def _proj(x, mods, mod_row, gain, w, cw, ap, tm, ncol):
    b_, t_, _ = x.shape
    r8 = tm // SUBLANES
    last8 = t_ // SUBLANES - 1
    tok = lambda wd: pl.BlockSpec((1, tm, wd), lambda b, j: (b, j, 0))
    colm = lambda wd: pl.BlockSpec((1, tm // ncol, ncol * wd), lambda b, j: (b, j, 0))
    const = lambda shp: pl.BlockSpec(shp, lambda b, j: (0,) * len(shp))
    out_specs = [tok(DN_WIDTH), tok(DN_WIDTH), tok(DN_WIDTH), tok(SMALL_W), tok(DN_WIDTH),
                 colm(2 * GLA_K_WIDTH), colm(GLA_V_WIDTH), tok(GLA_V_WIDTH)]
    tshape = lambda wd, dt=F32: jax.ShapeDtypeStruct((b_, t_, wd), dt)
    cshape = lambda wd: jax.ShapeDtypeStruct((b_, t_ // ncol, ncol * wd), F32)
    out_shape = [tshape(DN_WIDTH)] * 3 + [tshape(SMALL_W), tshape(DN_WIDTH, BF16), cshape(2 * GLA_K_WIDTH),
                                          cshape(GLA_V_WIDTH), tshape(GLA_V_WIDTH, BF16)]
    scratch = [pltpu.VMEM((tm + 2 * SUBLANES, QKV_W), F32)]
    if ncol > 1:
        out_specs.append(colm(SMALL_W))
        out_shape.append(cshape(SMALL_W))
        scratch.append(_col_scratch(GLA_V_WIDTH // LANES, tm // ncol, ncol))
    outs = pl.pallas_call(
        functools.partial(_proj_kernel, ncol=ncol),
        grid=(b_, t_ // tm),
        in_specs=[tok(D_MODEL),
                  pl.BlockSpec((1, SUBLANES, D_MODEL), lambda b, j: (b, jnp.maximum(j * r8 - 1, 0), 0)),
                  pl.BlockSpec((1, SUBLANES, D_MODEL), lambda b, j: (b, jnp.minimum((j + 1) * r8, last8), 0)),
                  pl.BlockSpec((1, SUBLANES, D_MODEL), lambda b, j: (mod_row(b), 0, 0)),
                  const((1, D_MODEL)), const((D_MODEL, PROJ_COLS)), const((SUBLANES, QKV_W)),
                  const((SUBLANES, SMALL_W))],
        out_specs=out_specs,
        out_shape=out_shape,
        scratch_shapes=scratch,
        compiler_params=_cparams(("parallel", "parallel")),
        name="proj",
    )(x, x, x, mods, gain.reshape(1, D_MODEL), w, cw, ap)
    q, k, v, p, dnz, gqk, gv, gz = outs[:8]
    p_col = outs[8] if ncol > 1 else p
    return q, k, v, p, p_col, dnz, gqk, gv, gz


def _order_masks(n, rev):
    ri = lax.broadcasted_iota(jnp.int32, (n, n), 0)
    ci = lax.broadcasted_iota(jnp.int32, (n, n), 1)
    if rev:
        return ri, ci, ci >= ri, ci > ri
    return ri, ci, ci <= ri, ci < ri


def _dn_kernel(qf_ref, kf_ref, vf_ref, pf_ref, qb_ref, kb_ref, vb_ref, pb_ref, s0_ref, of_ref, ob_ref, st_ref, s_scr):
    j = pl.program_id(1)
    nj = pl.num_programs(1)

    @pl.when(j == 0)
    def _():
        s_scr[...] = s0_ref[0]

    c_ = CHUNK
    tb = qf_ref.shape[1]
    nchunk = tb // c_
    streams = ((qf_ref, kf_ref, vf_ref, pf_ref, of_ref), (qb_ref, kb_ref, vb_ref, pb_ref, ob_ref))
    masks, gcum, gcum_t, order = [], [], [], []
    for d, rev in enumerate((False, True)):
        ri, ci, incl, strict = _order_masks(c_, rev)
        levels = []
        s = 1
        while s < c_:
            levels.append(strict & ((ri // (2 * s)) == (ci // (2 * s))) & ((ri // s) != (ci // s)))
            s *= 2
        masks.append(dict(incl=incl, strict=strict, eye=(ri == ci).astype(F32), levels=levels,
                          last=0 if rev else c_ - 1))
        rb, cb, incl_b, _ = _order_masks(tb, rev)
        cum_mat = incl_b & ((rb // c_) == (cb // c_))
        gcum.append(_dot_exact_lhs(cum_mat, streams[d][3][0], terms=3))
        gcum_t.append(gcum[d].T)
        order.append([nchunk - 1 - cc if rev else cc for cc in range(nchunk)])

    inst = [(d, c, h) for step in range(nchunk) for d in range(2) for c in (order[d][step],)
            for h in range(DN_HEADS)]
    pre = {}
    for d, c, h in inst:
        q_ref, k_ref, v_ref, p_ref, _ = streams[d]
        m = masks[d]
        r0 = c * c_
        lo = h * DN_HEAD_DIM
        lb = BETA_LANE + d * DN_HEADS + h
        lg = GDEC_LANE + d * DN_HEADS + h
        beta = p_ref[0, r0:r0 + c_, lb:lb + 1]
        gc = gcum[d][r0:r0 + c_, lg:lg + 1]
        gr = gcum_t[d][lg:lg + 1, r0:r0 + c_]
        glast = gcum[d][r0 + m["last"]:r0 + m["last"] + 1, lg:lg + 1]
        decay = jnp.exp(jnp.where(m["incl"], gc - gr, -jnp.inf))
        qh = q_ref[0, r0:r0 + c_, lo:lo + DN_HEAD_DIM]
        kh = k_ref[0, r0:r0 + c_, lo:lo + DN_HEAD_DIM]
        vh = v_ref[0, r0:r0 + c_, lo:lo + DN_HEAD_DIM]
        kb = kh * beta
        eg = jnp.exp(gc)
        pre[d, c, h] = dict(decay=decay, kq_lhs=jnp.concatenate([kb, qh], axis=0), kh=kh,
                            rhs=jnp.concatenate([vh * beta, kb * eg], axis=1), q_dec=qh * eg,
                            k_dec=kh * jnp.exp(glast - gc), e_last=jnp.exp(glast))
    kq = {i: _bdot_nt(pre[i]["kq_lhs"], pre[i]["kh"]) for i in inst}
    a = {i: jnp.where(masks[i[0]]["strict"], kq[i][:c_] * pre[i]["decay"], 0.0) for i in inst}
    attn = {i: kq[i][c_:] * pre[i]["decay"] for i in inst}
    t = {i: masks[i[0]]["eye"] - jnp.where(masks[i[0]]["levels"][0], a[i], 0.0) for i in inst}
    for lvl in range(1, len(masks[0]["levels"])):
        ta = {i: _bdot(t[i], jnp.where(masks[i[0]]["levels"][lvl], a[i], 0.0)) for i in inst}
        t = {i: t[i] - _bdot(ta[i], t[i]) for i in inst}
    uw = {i: _bdot(t[i], pre[i]["rhs"]) for i in inst}
    for step in range(nchunk):
        group = [(d, order[d][step], h) for d in range(2) for h in range(DN_HEADS)]
        st = {i: s_scr[i[0], i[2]] for i in group}
        ws = {i: _bdot(jnp.concatenate([uw[i][:, DN_HEAD_DIM:], pre[i]["q_dec"]], axis=0), st[i]) for i in group}
        v_new = {i: uw[i][:, :DN_HEAD_DIM] - ws[i][:c_] for i in group}
        for i in group:
            d, c, h = i
            streams[d][4][0, c * c_:(c + 1) * c_, h * DN_HEAD_DIM:(h + 1) * DN_HEAD_DIM] = (
                ws[i][c_:] + _bdot(attn[i], v_new[i]))
        for i in group:
            s_scr[i[0], i[2]] = st[i] * pre[i]["e_last"] + _bdot_tn(pre[i]["k_dec"], v_new[i])

    @pl.when(j == nj - 1)
    def _():
        st_ref[0] = s_scr[...]


def _dn_scan(q, k, v, p, s0, tb):
    b_, t_, _ = q.shape
    nb = t_ // tb
    fwd = lambda wd: pl.BlockSpec((1, tb, wd), lambda b, j: (b, j, 0))
    bwd = lambda wd: pl.BlockSpec((1, tb, wd), lambda b, j: (b, nb - 1 - j, 0))
    widths = (DN_WIDTH, DN_WIDTH, DN_WIDTH, SMALL_W)
    st_shape = (2, DN_HEADS, DN_HEAD_DIM, DN_HEAD_DIM)
    st_spec = pl.BlockSpec((1,) + st_shape, lambda b, j: (b, 0, 0, 0, 0))
    return pl.pallas_call(
        _dn_kernel,
        grid=(b_, nb),
        in_specs=[fwd(w) for w in widths] + [bwd(w) for w in widths] + [st_spec],
        out_specs=[fwd(DN_WIDTH), bwd(DN_WIDTH), st_spec],
        out_shape=[jax.ShapeDtypeStruct((b_, t_, DN_WIDTH), F32)] * 2 + [jax.ShapeDtypeStruct((b_,) + st_shape, F32)],
        scratch_shapes=[pltpu.VMEM(st_shape, F32)],
        compiler_params=_cparams(("parallel", "arbitrary")),
        name="dn",
    )(q, k, v, p, q, k, v, p, s0)


def _log_sigmoid(t):
    return jnp.minimum(t, 0.0) - jnp.log(1.0 + jnp.exp(-jnp.abs(t)))


def _gla_kernel(qkf_ref, vf_ref, pf_ref, qkb_ref, vb_ref, pb_ref, wg_ref, bg_ref, s0_ref, of_ref, ob_ref, st_ref,
                s_scr):
    col = pl.program_id(1)
    rb = pl.program_id(2)
    ncol = pl.num_programs(1)
    nrb = pl.num_programs(2)

    @pl.when((col == 0) & (rb == 0))
    def _():
        s_scr[...] = s0_ref[0]

    c_ = CHUNK
    nchunk = qkf_ref.shape[1] // c_
    ncb = pf_ref.shape[2] // SMALL_W
    nsub = c_ // SUB
    streams = ((qkf_ref, vf_ref, pf_ref, of_ref), (qkb_ref, vb_ref, pb_ref, ob_ref))
    masks, order = [], []
    for rev in (False, True):
        ri, ci, incl, _ = _order_masks(c_, rev)
        pos_r = (c_ - 1 - ri) if rev else ri
        pos_c = (c_ - 1 - ci) if rev else ci
        first_rows = [(c_ - 1 - SUB * i) if rev else SUB * i for i in range(nsub)]
        masks.append(dict(incl=incl, diag=incl & ((pos_r // SUB) == (pos_c // SUB)),
                          off=(pos_c // SUB) < (pos_r // SUB), last=0 if rev else c_ - 1, first_rows=first_rows,
                          rev=rev))
        cols = [ncb - 1 - i if rev else i for i in range(ncb)]
        chunks = [nchunk - 1 - cc if rev else cc for cc in range(nchunk)]
        order.append([(cb, c) for cb in cols for c in chunks])
    nstep = len(order[0])
    la_col = {}
    for d in range(2):
        for cb in range(ncb):
            gate = _dot_split(streams[d][2][0, :, cb * SMALL_W:(cb + 1) * SMALL_W], wg_ref[d]) + bg_ref[d]
            la_col[d, cb] = _log_sigmoid(gate) * (1.0 / GLA_GATE_NORMALIZER)
    pre = {}
    for step in range(nstep):
        for d in range(2):
            cb, c = order[d][step]
            qk_ref, v_ref, _, _ = streams[d]
            m = masks[d]
            r0 = c * c_
            la = la_col[d, cb][r0:r0 + c_, :]
            bc = _dot_exact_lhs(m["incl"], la, terms=2)
            bex = bc - la
            b_last = bc[m["last"]:m["last"] + 1, :]
            pieces = [jnp.broadcast_to(bex[fr:fr + 1, :], (SUB, GLA_K_WIDTH)) for fr in m["first_rows"]]
            refd = jnp.concatenate(pieces[::-1] if m["rev"] else pieces, axis=0)
            mid = m["first_rows"][nsub // 2]
            rmid = bex[mid:mid + 1, :]
            qk0 = cb * 2 * GLA_K_WIDTH
            qa = qk_ref[0, r0:r0 + c_, qk0:qk0 + GLA_K_WIDTH] * GLA_DK ** -0.5
            ka = qk_ref[0, r0:r0 + c_, qk0 + GLA_K_WIDTH:qk0 + 2 * GLA_K_WIDTH]
            q_in = qa * jnp.exp(bc)
            k_st = ka * jnp.exp(b_last - bc)
            q_d = qa * jnp.exp(bc - refd)
            k_d = ka * jnp.exp(jnp.minimum(refd - bc, EXP_CLAMP))
            q_o = qa * jnp.exp(jnp.minimum(bc - rmid, EXP_CLAMP))
            k_o = ka * jnp.exp(jnp.minimum(rmid - bc, EXP_CLAMP))
            e_last = jnp.exp(b_last)
            v0 = cb * GLA_V_WIDTH
            for h in range(GLA_HEADS):
                ks = slice(h * GLA_DK, (h + 1) * GLA_DK)
                vh = v_ref[0, r0:r0 + c_, v0 + h * GLA_DV:v0 + (h + 1) * GLA_DV]
                attn = (jnp.where(m["diag"], _bdot_nt(q_d[:, ks], k_d[:, ks]), 0.0)
                        + jnp.where(m["off"], _bdot_nt(q_o[:, ks], k_o[:, ks]), 0.0))
                pre[d, step, h] = dict(attn=attn, vh=vh, q_in=q_in[:, ks], e_last=e_last[:, ks],
                                       kv=_bdot_tn(vh, k_st[:, ks]))
    st = {(d, h): s_scr[d, h] for d in range(2) for h in range(GLA_HEADS)}
    for step in range(nstep):
        for d in range(2):
            cb, c = order[d][step]
            r0 = c * c_
            v0 = cb * GLA_V_WIDTH
            for h in range(GLA_HEADS):
                i = (d, step, h)
                streams[d][3][0, r0:r0 + c_, v0 + h * GLA_DV:v0 + (h + 1) * GLA_DV] = (
                    _bdot_nt(pre[i]["q_in"], st[d, h]) + _bdot(pre[i]["attn"], pre[i]["vh"]))
                st[d, h] = st[d, h] * pre[i]["e_last"] + pre[i]["kv"]
    for d in range(2):
        for h in range(GLA_HEADS):
            s_scr[d, h] = st[d, h]

    @pl.when((col == ncol - 1) & (rb == nrb - 1))
    def _():
        st_ref[0] = s_scr[...]


def _gla_scan(qk, v, p, wg, bg, s0, ncol, tr):
    b_, rows, _ = qk.shape
    nrb = rows // tr
    ncb = min(GLA_COLS_PER_STEP, ncol)
    ncblk = ncol // ncb
    assert ncb == 1 or nrb == 1
    fwd = lambda wd: pl.BlockSpec((1, tr, ncb * wd), lambda b, c, r: (b, r, c))
    bwd = lambda wd: pl.BlockSpec((1, tr, ncb * wd), lambda b, c, r: (b, nrb - 1 - r, ncblk - 1 - c))
    widths = (2 * GLA_K_WIDTH, GLA_V_WIDTH, SMALL_W)
    st_shape = (2, GLA_HEADS, GLA_DV, GLA_DK)
    st_spec = pl.BlockSpec((1,) + st_shape, lambda b, c, r: (b, 0, 0, 0, 0))
    return pl.pallas_call(
        _gla_kernel,
        grid=(b_, ncblk, nrb),
        in_specs=[fwd(w) for w in widths] + [bwd(w) for w in widths] + [
            pl.BlockSpec((2, SMALL_W, GLA_K_WIDTH), lambda b, c, r: (0, 0, 0)),
            pl.BlockSpec((2, 1, GLA_K_WIDTH), lambda b, c, r: (0, 0, 0)),
            st_spec],
        out_specs=[fwd(GLA_V_WIDTH), bwd(GLA_V_WIDTH), st_spec],
        out_shape=[jax.ShapeDtypeStruct((b_, rows, ncol * GLA_V_WIDTH), F32)] * 2 + [
            jax.ShapeDtypeStruct((b_,) + st_shape, F32)],
        scratch_shapes=[pltpu.VMEM(st_shape, F32)],
        compiler_params=_cparams(("parallel", "arbitrary", "arbitrary")),
        name="gla",
    )(qk, v, p, qk, v, p, wg, bg, s0)


def _head_norm_gate(o, z, gain):
    parts = []
    for h in range(o.shape[1] // LANES):
        hs = slice(h * LANES, (h + 1) * LANES)
        parts.append(_rms(o[:, hs], gain) * _silu(z[:, hs].astype(F32)))
    return jnp.concatenate(parts, axis=1)


def _mixout_kernel(*refs, with_router, ncol):
    (x_ref, odf_ref, odb_ref, dnz_ref, ogf_ref, ogb_ref, glz_ref, mod_ref, gd_ref, gg_ref, w_ref, gf_ref) = refs[:12]
    refs = list(refs[12:])
    col_scr = refs.pop() if ncol > 1 else None
    if with_router:
        rw_ref, xo_ref, h_ref, cmb_ref = refs
    else:
        xo_ref, h_ref = refs
    m_dn = _head_norm_gate(odf_ref[0] + odb_ref[0], dnz_ref[0], gd_ref[...])
    o_gl = ogf_ref[0] + ogb_ref[0]
    if ncol > 1:
        o_gl = _from_columns(o_gl, col_scr, ncol)
    m_gl = _head_norm_gate(o_gl, glz_ref[0], gg_ref[...])
    m = jnp.concatenate([m_dn, m_gl], axis=1).astype(BF16)
    x = x_ref[0] + mod_ref[0, 2:3, :] * jnp.dot(m, w_ref[...], preferred_element_type=F32)
    xo_ref[0] = x
    h = _rms(x, gf_ref[...]) * (1.0 + mod_ref[0, 4:5, :]) + mod_ref[0, 3:4, :]
    h_b = h.astype(BF16)
    h_ref[0] = h_b
    if with_router:
        rw_hi, rw_lo = _split_bf16(rw_ref[...], 2)
        logits = (jnp.dot(h_b, rw_hi, preferred_element_type=F32)
                  + jnp.dot(h_b, rw_lo, preferred_element_type=F32))
        lane = lax.broadcasted_iota(jnp.int32, logits.shape, 1).astype(F32)
        logits = jnp.where(lane < N_EXPERTS, logits, -jnp.inf)
        e = jnp.exp(logits - jnp.max(logits, axis=-1, keepdims=True))
        probs = e / jnp.sum(e, axis=-1, keepdims=True)
        p1 = jnp.max(probs, axis=-1, keepdims=True)
        i1 = jnp.min(jnp.where(probs == p1, lane, float(LANES)), axis=-1, keepdims=True)
        rest = jnp.where(lane == i1, -1.0, probs)
        p2 = jnp.max(rest, axis=-1, keepdims=True)
        i2 = jnp.min(jnp.where(rest == p2, lane, float(LANES)), axis=-1, keepdims=True)
        cmb_ref[0] = (jnp.where(lane == i1, p1, 0.0) + jnp.where(lane == i2, p2, 0.0)) / (p1 + p2)


def _mixout(x, odf, odb, dnz, ogf, ogb, glz, mods, mod_row, gd, gg, w_out, gf, router, tm, ncol):
    b_, t_, _ = x.shape
    with_router = router is not None
    tok = lambda wd: pl.BlockSpec((1, tm, wd), lambda b, j: (b, j, 0))
    colm = lambda wd: pl.BlockSpec((1, tm // ncol, ncol * wd), lambda b, j: (b, j, 0))
    const = lambda shp: pl.BlockSpec(shp, lambda b, j: (0,) * len(shp))
    in_specs = [tok(D_MODEL)] + [tok(DN_WIDTH)] * 3 + [colm(GLA_V_WIDTH)] * 2 + [tok(GLA_V_WIDTH)] + [
        pl.BlockSpec((1, SUBLANES, D_MODEL), lambda b, j: (mod_row(b), 0, 0)),
        const((1, LANES)), const((1, LANES)), const((D_MODEL, D_MODEL)), const((1, D_MODEL))]
    args = [x, odf, odb, dnz, ogf, ogb, glz, mods, gd.reshape(1, LANES), gg.reshape(1, LANES), w_out,
            gf.reshape(1, D_MODEL)]
    out_specs = [tok(D_MODEL), tok(D_MODEL)]
    out_shape = [jax.ShapeDtypeStruct((b_, t_, D_MODEL), F32), jax.ShapeDtypeStruct((b_, t_, D_MODEL), BF16)]
    if with_router:
        in_specs.append(const((D_MODEL, LANES)))
        args.append(router)
        out_specs.append(tok(LANES))
        out_shape.append(jax.ShapeDtypeStruct((b_, t_, LANES), F32))
    scratch = [_col_scratch(GLA_V_WIDTH // LANES, tm // ncol, ncol)] if ncol > 1 else []
    return pl.pallas_call(
        functools.partial(_mixout_kernel, with_router=with_router, ncol=ncol),
        grid=(b_, t_ // tm),
        in_specs=in_specs,
        out_specs=out_specs,
        out_shape=out_shape,
        scratch_shapes=scratch,
        compiler_params=_cparams(("parallel", "parallel")),
        name="mixout",
    )(*args)


def _ffn_kernel(x_ref, h_ref, mod_ref, wg_ref, wu_ref, wd_ref, o_ref, *, nsplit):
    h = h_ref[0]
    f = wg_ref.shape[1]
    fs = f // nsplit
    acc = None
    for s in range(nsplit):
        g = jnp.dot(h, wg_ref[:, s * fs:(s + 1) * fs], preferred_element_type=F32)
        u = jnp.dot(h, wu_ref[:, s * fs:(s + 1) * fs], preferred_element_type=F32)
        part = jnp.dot((_silu(g) * u).astype(BF16), wd_ref[s * fs:(s + 1) * fs, :], preferred_element_type=F32)
        acc = part if acc is None else acc + part
    o_ref[0] = x_ref[0] + mod_ref[0, 5:6, :] * acc


def _ffn(x, h, mods, mod_row, w_gu, w_down, tm):
    b_, t_, _ = x.shape
    f = w_down.shape[0]
    tok = lambda: pl.BlockSpec((1, tm, D_MODEL), lambda b, j: (b, j, 0))
    return pl.pallas_call(
        functools.partial(_ffn_kernel, nsplit=2),
        grid=(b_, t_ // tm),
        in_specs=[tok(), tok(),
                  pl.BlockSpec((1, SUBLANES, D_MODEL), lambda b, j: (mod_row(b), 0, 0)),
                  pl.BlockSpec((D_MODEL, f), lambda b, j: (0, 0)),
                  pl.BlockSpec((D_MODEL, f), lambda b, j: (0, 1)),
                  pl.BlockSpec((f, D_MODEL), lambda b, j: (0, 0))],
        out_specs=tok(),
        out_shape=jax.ShapeDtypeStruct((b_, t_, D_MODEL), F32),
        compiler_params=_cparams(("parallel", "parallel")),
        name="ffn",
    )(x, h, mods, w_gu, w_gu, w_down)


def _for_row_tiles(ntiles, fn):
    rows = MOE_GROUP * MOE_TILE
    single = ntiles == MOE_GROUP + 1
    keep = 1 - single.astype(jnp.int32)
    ngroup = (ntiles // MOE_GROUP) * keep
    rest = (ntiles % MOE_GROUP) * keep

    def body(i, carry):
        fn(pl.multiple_of(i * rows, rows), rows)
        return carry

    lax.fori_loop(0, ngroup, body, 0)
    for k in range(1, MOE_GROUP):
        @pl.when(rest == k)
        def _():
            fn(pl.multiple_of(ngroup * rows, rows), k * MOE_TILE)

    @pl.when(single)
    def _():
        fn(0, rows + MOE_TILE)


def _moe_kernel(cnt_ref, x_ref, h_ref, cmb_ref, mod_ref, wg_ref, wu_ref, wd_ref, fg_ref, o_ref,
                acc_ref, xe_ref, ye_ref, rkc_ref, rkr_ref, *, final_norm):
    b = pl.program_id(0)
    j = pl.program_id(1)
    e = pl.program_id(2)
    f = pl.program_id(3)
    nj = pl.num_programs(1)
    ne = pl.num_programs(2)
    nf = pl.num_programs(3)
    tm = h_ref.shape[1]
    ntiles = (cnt_ref[(b * nj + j) * ne + e] + MOE_TILE - 1) // MOE_TILE

    @pl.when((e == 0) & (f == 0))
    def _():
        acc_ref[...] = jnp.zeros_like(acc_ref)
        sb = 2 * LANES
        ri = lax.broadcasted_iota(jnp.int32, (sb, sb), 0)
        ci = lax.broadcasted_iota(jnp.int32, (sb, sb), 1)
        before = (ci < ri).astype(BF16)
        carry = jnp.zeros((1, LANES), F32)
        for s in range(tm // sb):
            sel = (cmb_ref[0, s * sb:(s + 1) * sb, :] > 0.0).astype(F32)
            rank = jnp.dot(before, sel.astype(BF16), preferred_element_type=F32) + carry
            rkc_ref[s * sb:(s + 1) * sb, :] = jnp.where(sel > 0.0, rank, -1.0)
            carry = carry + jnp.sum(sel, axis=0, keepdims=True)
        rkr_ref[...] = rkc_ref[...].T

    @pl.when(f == 0)
    def _():
        rank_row = rkr_ref[pl.ds(e, 1), :]

        def gather(r0, n):
            slot = (lax.broadcasted_iota(jnp.int32, (n, tm), 0) + r0).astype(F32)
            onehot = (rank_row == slot).astype(BF16)
            xe_ref[pl.ds(r0, n), :] = jnp.dot(onehot, h_ref[0], preferred_element_type=F32).astype(BF16)

        _for_row_tiles(ntiles, gather)

    def expert(r0, n):
        xr = xe_ref[pl.ds(r0, n), :]
        g = jnp.dot(xr, wg_ref[0], preferred_element_type=F32)
        u = jnp.dot(xr, wu_ref[0], preferred_element_type=F32)
        y = jnp.dot((_silu(g) * u).astype(BF16), wd_ref[0], preferred_element_type=F32)

        @pl.when(f == 0)
        def _():
            ye_ref[pl.ds(r0, n), :] = y

        @pl.when(f > 0)
        def _():
            ye_ref[pl.ds(r0, n), :] += y

    _for_row_tiles(ntiles, expert)

    @pl.when(f == nf - 1)
    def _():
        lane = lax.broadcasted_iota(jnp.int32, (tm, LANES), 1)
        rank_col = jnp.sum(jnp.where(lane == e, rkc_ref[...], 0.0), axis=-1, keepdims=True)
        ce = jnp.sum(jnp.where(lane == e, cmb_ref[0], 0.0), axis=-1, keepdims=True)

        def scatter(r0, n):
            slot = (lax.broadcasted_iota(jnp.int32, (tm, n), 1) + r0).astype(F32)
            onehot = (rank_col == slot).astype(BF16)
            back = jnp.dot(onehot, ye_ref[pl.ds(r0, n), :].astype(BF16), preferred_element_type=F32)
            acc_ref[...] += ce * back

        _for_row_tiles(ntiles, scatter)

    @pl.when((e == ne - 1) & (f == nf - 1))
    def _():
        y = x_ref[0] + mod_ref[0, 5:6, :] * acc_ref[...]
        if final_norm:
            y = _rms(y, fg_ref[...])
        o_ref[0] = y


def _moe(x, h, cmb, mods, mod_row, w_gu, w_down, final_gain, tm, tf):
    b_, t_, _ = x.shape
    ne, _, f2 = w_gu.shape
    fe = f2 // 2
    assert fe % tf == 0 and tm % (MOE_GROUP * MOE_TILE) == 0 and ne <= SUBLANES
    nf = fe // tf
    nj = t_ // tm
    counts = jnp.sum((cmb[..., :ne] > 0.0).reshape(b_, nj, tm, ne), axis=2, dtype=jnp.int32).reshape(-1)
    tok = lambda wd: pl.BlockSpec((1, tm, wd), lambda b, j, e, f, cnt: (b, j, 0))
    final_norm = final_gain is not None
    fg = (final_gain if final_norm else jnp.ones((D_MODEL,), F32)).reshape(1, D_MODEL)
    grid_spec = pltpu.PrefetchScalarGridSpec(
        num_scalar_prefetch=1,
        grid=(b_, nj, ne, nf),
        in_specs=[tok(D_MODEL), tok(D_MODEL), tok(LANES),
                  pl.BlockSpec((1, SUBLANES, D_MODEL), lambda b, j, e, f, cnt: (mod_row(b), 0, 0)),
                  pl.BlockSpec((1, D_MODEL, tf), lambda b, j, e, f, cnt: (e, 0, f)),
                  pl.BlockSpec((1, D_MODEL, tf), lambda b, j, e, f, cnt: (e, 0, nf + f)),
                  pl.BlockSpec((1, tf, D_MODEL), lambda b, j, e, f, cnt: (e, f, 0)),
                  pl.BlockSpec((1, D_MODEL), lambda b, j, e, f, cnt: (0, 0))],
        out_specs=tok(D_MODEL),
        scratch_shapes=[pltpu.VMEM((tm, D_MODEL), F32),
                        pltpu.VMEM((tm, D_MODEL), BF16),
                        pltpu.VMEM((tm, D_MODEL), F32),
                        pltpu.VMEM((tm, LANES), F32),
                        pltpu.VMEM((LANES, tm), F32)])
    return pl.pallas_call(
        functools.partial(_moe_kernel, final_norm=final_norm),
        grid_spec=grid_spec,
        out_shape=jax.ShapeDtypeStruct((b_, t_, D_MODEL), F32),
        compiler_params=_cparams(("parallel", "parallel", "arbitrary", "arbitrary")),
        name="moe",
    )(counts, x, h, cmb, mods, w_gu, w_gu, w_down, fg)


def _pack_w_in(w):
    offs = np.cumsum((0, QKV_W, DN_WIDTH, 2 * DN_HEADS, 2 * DN_HEADS, GLA_K_WIDTH, GLA_K_WIDTH, GLA_V_WIDTH,
                      GLA_V_WIDTH, 2 * GLA_GATE_RANK))
    seg = lambda i: w[:, offs[i]:offs[i + 1]]
    small = jnp.concatenate([seg(2), seg(3), seg(8)], axis=1)
    small = jnp.pad(small, ((0, 0), (0, SMALL_W - small.shape[1])))
    return jnp.concatenate([seg(0), seg(1), seg(4), seg(5), seg(6), seg(7), small], axis=1).astype(BF16)


def _pack_gate(w_gate2, b_gate):
    ws = []
    for d in range(2):
        lo = GLR_LANE + d * GLA_GATE_RANK
        ws.append(jnp.zeros((SMALL_W, GLA_K_WIDTH), F32).at[lo:lo + GLA_GATE_RANK].set(w_gate2[d]))
    return jnp.stack(ws), b_gate.reshape(2, 1, GLA_K_WIDTH)


def _pack_dn_params(a_log, dt_bias):
    neg_a = jnp.zeros((SMALL_W,), F32).at[GDEC_LANE:GLR_LANE].set(-jnp.exp(a_log.reshape(-1)))
    dtb = jnp.zeros((SMALL_W,), F32).at[GDEC_LANE:GLR_LANE].set(dt_bias.reshape(-1))
    return jnp.zeros((SUBLANES, SMALL_W), F32).at[0].set(neg_a).at[1].set(dtb)


def _pad_rows(a, rows):
    return jnp.pad(a, ((0, rows - a.shape[0]),) + ((0, 0),) * (a.ndim - 1))


def _mixers(x_seq, mods, mod_row, gain_mix, w_in_p, cw, ap, wgs, bgs, states, ncol, tm, tb):
    b_, t_, _ = x_seq.shape
    q, k, v, p, p_col, dnz, gqk, gv, gz = _proj(x_seq, mods, mod_row, gain_mix, w_in_p, cw, ap, tm, ncol)
    rows = t_ // ncol
    o_dn_f, o_dn_b, s_dn = _dn_scan(q, k, v, p, states[0], tb)
    o_gl_f, o_gl_b, s_gl = _gla_scan(gqk, gv, p_col, wgs, bgs, states[1], ncol, min(rows, 2 * CHUNK))
    outs = [(o_dn_f, o_gl_f), (o_dn_b, o_gl_b)]
    return outs, dnz, gz, (s_dn, s_gl)


def kernel(x, c, ctx, c_ctx, w_ada, b_ada, norm_mix, norm_ffn, w_in, conv_qkv, dn_a_log, dn_dt_bias, dn_norm,
           gla_w_gate2, gla_b_gate, gla_norm, w_out, ffn_w_gu, ffn_w_down, moe_router, moe_w_gu, moe_w_down,
           final_norm):
    b_, seq, _ = x.shape
    depth = w_ada.shape[0]
    ctx_len = ctx.shape[1]
    mod_rows = -(-(b_ + 1) // SUBLANES) * SUBLANES
    c_rows = _pad_rows(jnp.concatenate([c, c_ctx[None, :]], axis=0), mod_rows)
    lat_row = lambda b: b
    ctx_row = lambda b: b_
    tm_lat, tm_ctx = 512, ctx_len
    tb = 4 * CHUNK

    for i in range(depth):
        last = i == depth - 1
        mods = _ada(c_rows, w_ada[i], b_ada[i]).reshape(mod_rows, N_MOD, D_MODEL)
        mods = jnp.pad(mods, ((0, 0), (0, SUBLANES - N_MOD), (0, 0)))
        w_in_p = _pack_w_in(w_in[i])
        cw = _pad_rows(conv_qkv[i], SUBLANES)
        ap = _pack_dn_params(dn_a_log[i], dn_dt_bias[i])
        wgs, bgs = _pack_gate(gla_w_gate2[i], gla_b_gate[i])
        s0 = (jnp.zeros((b_, 2, DN_HEADS, DN_HEAD_DIM, DN_HEAD_DIM), F32),
              jnp.zeros((b_, 2, GLA_HEADS, GLA_DV, GLA_DK), F32))
        outs_c, dnz_c, gz_c, s_ctx = _mixers(ctx, mods, ctx_row, norm_mix[i], w_in_p, cw, ap, wgs, bgs, s0, 1,
                                             tm_ctx, tb)
        outs_l, dnz_l, gz_l, _ = _mixers(x, mods, lat_row, norm_mix[i], w_in_p, cw, ap, wgs, bgs, s_ctx, GRID_W,
                                         tm_lat, tb)
        w_out_b = w_out[i].astype(BF16)
        is_moe = i % 2 == 1
        if is_moe:
            router = jnp.pad(moe_router[i // 2], ((0, 0), (0, LANES - N_EXPERTS)))
            w_gu = moe_w_gu[i // 2].astype(BF16)
            w_dn = moe_w_down[i // 2].astype(BF16)
        else:
            router = None
            w_gu = ffn_w_gu[i // 2].astype(BF16)
            w_dn = ffn_w_down[i // 2].astype(BF16)

        def channel(tokens, outs, dnz, gz, row, tm, ncol, final_gain):
            res = _mixout(tokens, outs[0][0], outs[1][0], dnz, outs[0][1], outs[1][1], gz, mods, row, dn_norm[i],
                          gla_norm[i], w_out_b, norm_ffn[i], router, tm, ncol)
            if is_moe:
                x_mid, h, cmb = res
                tm_moe = min(1024, tokens.shape[1])
                return _moe(x_mid, h, cmb, mods, row, w_gu, w_dn, final_gain, tm_moe, D_EXPERT_TILE)
            x_mid, h = res
            y = _ffn(x_mid, h, mods, row, w_gu, w_dn, tm)
            if final_gain is not None:
                raise NotImplementedError("final norm is fused into the expert mixer only")
            return y

        x = channel(x, outs_l, dnz_l, gz_l, lat_row, tm_lat, GRID_W, final_norm if last else None)
        if not last:
            ctx = channel(ctx, outs_c, dnz_c, gz_c, ctx_row, tm_ctx, 1, None)
    return x
```

```python
import functools

import jax
import jax.numpy as jnp
import numpy as np
from jax import lax
from jax.experimental import pallas as pl
from jax.experimental.pallas import tpu as pltpu

F32 = jnp.float32
BF16 = jnp.bfloat16
HIGHEST = lax.Precision.HIGHEST

D_MODEL = 1024
GRID_W = 64
DN_HEADS = 4
DN_HEAD_DIM = 128
DN_WIDTH = DN_HEADS * DN_HEAD_DIM
CONV_W = 5
GLA_HEADS = 4
GLA_DK = 64
GLA_DV = 128
GLA_K_WIDTH = GLA_HEADS * GLA_DK
GLA_V_WIDTH = GLA_HEADS * GLA_DV
GLA_GATE_RANK = 16
GLA_GATE_NORMALIZER = 16.0
N_EXPERTS = 8
N_MOD = 6
NORM_EPS = 1e-6

LANES = 128
SUBLANES = 8
CHUNK = 64
SUB = 16
EXP_CLAMP = 80.0
D_EXPERT_TILE = 1792
MOE_TILE = 64
MOE_GROUP = 4
GLA_COLS_PER_STEP = 8
VMEM_LIMIT = 56 * 1024 * 1024

QKV_W = 3 * DN_WIDTH
SMALL_W = LANES
PROJ_WIDTHS = (QKV_W, DN_WIDTH, 2 * GLA_K_WIDTH, GLA_V_WIDTH, GLA_V_WIDTH, SMALL_W)
PROJ_COLS = sum(PROJ_WIDTHS)
BETA_LANE = 0
GDEC_LANE = 2 * DN_HEADS
GLR_LANE = 4 * DN_HEADS


def _cparams(sem, vmem=VMEM_LIMIT):
    return pltpu.CompilerParams(dimension_semantics=sem, vmem_limit_bytes=vmem)


def _sigmoid(t):
    return 1.0 / (1.0 + jnp.exp(-t))


def _silu(t):
    return t * _sigmoid(t)


def _softplus(t):
    return jnp.maximum(t, 0.0) + jnp.log(1.0 + jnp.exp(-jnp.abs(t)))


def _bdot(a, b):
    return jnp.dot(a.astype(BF16), b.astype(BF16), preferred_element_type=F32)


def _bdot_nt(a, b):
    return lax.dot_general(a.astype(BF16), b.astype(BF16), (((1,), (1,)), ((), ())), preferred_element_type=F32)


def _bdot_tn(a, b):
    return lax.dot_general(a.astype(BF16), b.astype(BF16), (((0,), (0,)), ((), ())), preferred_element_type=F32)


def _split_bf16(t, terms):
    parts = []
    for _ in range(terms):
        part = t.astype(BF16)
        parts.append(part)
        t = t - part.astype(F32)
    return parts


def _dot_exact_lhs(a, x, terms=2):
    a = a.astype(BF16)
    out = None
    for part in _split_bf16(x, terms):
        d = jnp.dot(a, part, preferred_element_type=F32)
        out = d if out is None else out + d
    return out


def _dot_split(a, b):
    a_hi, a_lo = _split_bf16(a, 2)
    b_hi, b_lo = _split_bf16(b, 2)
    dot = lambda u, v: jnp.dot(u, v, preferred_element_type=F32)
    return dot(a_hi, b_hi) + (dot(a_lo, b_hi) + dot(a_hi, b_lo))


def _rms(t, gain):
    return t * lax.rsqrt(jnp.mean(t * t, axis=-1, keepdims=True) + NORM_EPS) * gain


def _ada_kernel(c_ref, w_ref, b_ref, o_ref):
    s = _silu(c_ref[...])
    o_ref[...] = jnp.dot(s, w_ref[...], precision=HIGHEST, preferred_element_type=F32) + b_ref[...]


def _ada(c_rows, w, b):
    rows = c_rows.shape[0]
    n = w.shape[1]
    tn = D_MODEL
    return pl.pallas_call(
        _ada_kernel,
        grid=(n // tn,),
        in_specs=[pl.BlockSpec((rows, D_MODEL), lambda j: (0, 0)),
                  pl.BlockSpec((D_MODEL, tn), lambda j: (0, j)),
                  pl.BlockSpec((1, tn), lambda j: (0, j))],
        out_specs=pl.BlockSpec((rows, tn), lambda j: (0, j)),
        out_shape=jax.ShapeDtypeStruct((rows, n), F32),
        compiler_params=_cparams(("arbitrary",)),
        name="ada",
    )(c_rows, w, b.reshape(1, n))


def _to_columns(val, out_ref, scr_ref, ncol):
    nrows, width = val.shape
    groups = width // LANES
    grid_rows = nrows // ncol
    pitch = _col_pitch(ncol)
    for g in range(groups):
        for r in range(grid_rows):
            scr_ref[g, r * pitch:r * pitch + ncol, :] = val[r * ncol:(r + 1) * ncol, g * LANES:(g + 1) * LANES]
    for c in range(ncol):
        for g in range(groups):
            lo = c * width + g * LANES
            out_ref[0, :, lo:lo + LANES] = scr_ref[g, pl.ds(c, grid_rows, stride=pitch), :]


def _from_columns(in_val, scr_ref, ncol):
    grid_rows = in_val.shape[0]
    width = in_val.shape[1] // ncol
    groups = width // LANES
    pitch = _col_pitch(ncol)
    for c in range(ncol):
        for g in range(groups):
            lo = c * width + g * LANES
            scr_ref[g, pl.ds(c, grid_rows, stride=pitch), :] = in_val[:, lo:lo + LANES]
    return jnp.concatenate(
        [jnp.concatenate([scr_ref[g, r * pitch:r * pitch + ncol, :] for r in range(grid_rows)], axis=0)
         for g in range(groups)], axis=1)


def _col_pitch(ncol):
    return ncol + SUBLANES // 2


def _col_scratch(groups, grid_rows, ncol):
    rows = -(-grid_rows * _col_pitch(ncol) // SUBLANES) * SUBLANES
    return pltpu.VMEM((groups, rows, LANES), F32)


def _proj_kernel(x_ref, xp_ref, xn_ref, mod_ref, gain_ref, w_ref, cw_ref, ap_ref,
                 q_ref, k_ref, v_ref, p_ref, dnz_ref, gqk_ref, gv_ref, gz_ref, *rest, ncol):
    if ncol > 1:
        pcol_ref, ext_ref, col_scr = rest
    else:
        (ext_ref,) = rest
    j = pl.program_id(1)
    nj = pl.num_programs(1)
    tm = x_ref.shape[1]
    halo = SUBLANES
    pad = CONV_W // 2
    norm_mod = lambda t: _rms(t, gain_ref[...]) * (1.0 + mod_ref[0, 1:2, :]) + mod_ref[0, 0:1, :]
    h = norm_mod(x_ref[0])
    h_all = jnp.concatenate([h, norm_mod(xp_ref[0]), norm_mod(xn_ref[0])], axis=0).astype(BF16)
    r_qkv = jnp.dot(h_all, w_ref[:, 0:QKV_W], preferred_element_type=F32)
    r = jnp.dot(h.astype(BF16), w_ref[:, QKV_W:], preferred_element_type=F32)
    o = 0
    for ref, wd in zip((dnz_ref, gqk_ref, gv_ref, gz_ref), PROJ_WIDTHS[1:5]):
        if ncol > 1 and (ref is gqk_ref or ref is gv_ref):
            _to_columns(r[:, o:o + wd], ref, col_scr, ncol)
        else:
            ref[0] = r[:, o:o + wd].astype(ref.dtype)
        o += wd
    sm = r[:, o:o + SMALL_W]

    ext_ref[0:halo, :] = jnp.where(j > 0, r_qkv[tm:tm + halo], 0.0)
    ext_ref[halo:halo + tm, :] = r_qkv[0:tm]
    ext_ref[halo + tm:2 * halo + tm, :] = jnp.where(j < nj - 1, r_qkv[tm + halo:], 0.0)
    acc = None
    for w in range(CONV_W):
        term = ext_ref[pl.ds(halo - pad + w, tm), :] * cw_ref[w:w + 1, :]
        acc = term if acc is None else acc + term
    s = _silu(acc)
    for hd in range(DN_HEADS):
        lo = hd * DN_HEAD_DIM
        qh = s[:, lo:lo + DN_HEAD_DIM]
        kh = s[:, DN_WIDTH + lo:DN_WIDTH + lo + DN_HEAD_DIM]
        q_ref[0, :, lo:lo + DN_HEAD_DIM] = (qh * lax.rsqrt(jnp.sum(qh * qh, axis=-1, keepdims=True) + NORM_EPS)
                                            * DN_HEAD_DIM ** -0.5)
        k_ref[0, :, lo:lo + DN_HEAD_DIM] = kh * lax.rsqrt(jnp.sum(kh * kh, axis=-1, keepdims=True) + NORM_EPS)
    v_ref[0] = s[:, 2 * DN_WIDTH:]
    lane = lax.broadcasted_iota(jnp.int32, sm.shape, 1)
    beta = _sigmoid(sm)
    gdec = ap_ref[0:1, :] * _softplus(sm + ap_ref[1:2, :])
    p = jnp.where(lane < GDEC_LANE, beta, jnp.where(lane < GLR_LANE, gdec, sm))
    p_ref[0] = p
    if ncol > 1:
        _to_columns(p, pcol_ref, col_scr, ncol)


def _proj(x, mods, mod_row, gain, w, cw, ap, tm, ncol):
    b_, t_, _ = x.shape
    r8 = tm // SUBLANES
    last8 = t_ // SUBLANES - 1
    tok = lambda wd: pl.BlockSpec((1, tm, wd), lambda b, j: (b, j, 0))
    colm = lambda wd: pl.BlockSpec((1, tm // ncol, ncol * wd), lambda b, j: (b, j, 0))
    const = lambda shp: pl.BlockSpec(shp, lambda b, j: (0,) * len(shp))
    out_specs = [tok(DN_WIDTH), tok(DN_WIDTH), tok(DN_WIDTH), tok(SMALL_W), tok(DN_WIDTH),
                 colm(2 * GLA_K_WIDTH), colm(GLA_V_WIDTH), tok(GLA_V_WIDTH)]
    tshape = lambda wd, dt=F32: jax.ShapeDtypeStruct((b_, t_, wd), dt)
    cshape = lambda wd: jax.ShapeDtypeStruct((b_, t_ // ncol, ncol * wd), F32)
    out_shape = [tshape(DN_WIDTH)] * 3 + [tshape(SMALL_W), tshape(DN_WIDTH, BF16), cshape(2 * GLA_K_WIDTH),
                                          cshape(GLA_V_WIDTH), tshape(GLA_V_WIDTH, BF16)]
    scratch = [pltpu.VMEM((tm + 2 * SUBLANES, QKV_W), F32)]
    if ncol > 1:
        out_specs.append(colm(SMALL_W))
        out_shape.append(cshape(SMALL_W))
        scratch.append(_col_scratch(GLA_V_WIDTH // LANES, tm // ncol, ncol))
    outs = pl.pallas_call(
        functools.partial(_proj_kernel, ncol=ncol),
        grid=(b_, t_ // tm),
        in_specs=[tok(D_MODEL),
                  pl.BlockSpec((1, SUBLANES, D_MODEL), lambda b, j: (b, jnp.maximum(j * r8 - 1, 0), 0)),
                  pl.BlockSpec((1, SUBLANES, D_MODEL), lambda b, j: (b, jnp.minimum((j + 1) * r8, last8), 0)),
                  pl.BlockSpec((1, SUBLANES, D_MODEL), lambda b, j: (mod_row(b), 0, 0)),
                  const((1, D_MODEL)), const((D_MODEL, PROJ_COLS)), const((SUBLANES, QKV_W)),
                  const((SUBLANES, SMALL_W))],
        out_specs=out_specs,
        out_shape=out_shape,
        scratch_shapes=scratch,
        compiler_params=_cparams(("parallel", "parallel")),
        name="proj",
    )(x, x, x, mods, gain.reshape(1, D_MODEL), w, cw, ap)
    q, k, v, p, dnz, gqk, gv, gz = outs[:8]
    p_col = outs[8] if ncol > 1 else p
    return q, k, v, p, p_col, dnz, gqk, gv, gz


def _order_masks(n, rev):
    ri = lax.broadcasted_iota(jnp.int32, (n, n), 0)
    ci = lax.broadcasted_iota(jnp.int32, (n, n), 1)
    if rev:
        return ri, ci, ci >= ri, ci > ri
    return ri, ci, ci <= ri, ci < ri


def _dn_kernel(qf_ref, kf_ref, vf_ref, pf_ref, qb_ref, kb_ref, vb_ref, pb_ref, s0_ref, of_ref, ob_ref, st_ref, s_scr):
    j = pl.program_id(1)
    nj = pl.num_programs(1)

    @pl.when(j == 0)
    def _():
        s_scr[...] = s0_ref[0]

    c_ = CHUNK
    tb = qf_ref.shape[1]
    nchunk = tb // c_
    streams = ((qf_ref, kf_ref, vf_ref, pf_ref, of_ref), (qb_ref, kb_ref, vb_ref, pb_ref, ob_ref))
    masks, gcum, gcum_t, order = [], [], [], []
    for d, rev in enumerate((False, True)):
        ri, ci, incl, strict = _order_masks(c_, rev)
        levels = []
        s = 1
        while s < c_:
            levels.append(strict & ((ri // (2 * s)) == (ci // (2 * s))) & ((ri // s) != (ci // s)))
            s *= 2
        masks.append(dict(incl=incl, strict=strict, eye=(ri == ci).astype(F32), levels=levels,
                          last=0 if rev else c_ - 1))
        rb, cb, incl_b, _ = _order_masks(tb, rev)
        cum_mat = incl_b & ((rb // c_) == (cb // c_))
        gcum.append(_dot_exact_lhs(cum_mat, streams[d][3][0], terms=3))
        gcum_t.append(gcum[d].T)
        order.append([nchunk - 1 - cc if rev else cc for cc in range(nchunk)])

    inst = [(d, c, h) for step in range(nchunk) for d in range(2) for c in (order[d][step],)
            for h in range(DN_HEADS)]
    pre = {}
    for d, c, h in inst:
        q_ref, k_ref, v_ref, p_ref, _ = streams[d]
        m = masks[d]
        r0 = c * c_
        lo = h * DN_HEAD_DIM
        lb = BETA_LANE + d * DN_HEADS + h
        lg = GDEC_LANE + d * DN_HEADS + h
        beta = p_ref[0, r0:r0 + c_, lb:lb + 1]
        gc = gcum[d][r0:r0 + c_, lg:lg + 1]
        gr = gcum_t[d][lg:lg + 1, r0:r0 + c_]
        glast = gcum[d][r0 + m["last"]:r0 + m["last"] + 1, lg:lg + 1]
        decay = jnp.exp(jnp.where(m["incl"], gc - gr, -jnp.inf))
        qh = q_ref[0, r0:r0 + c_, lo:lo + DN_HEAD_DIM]
        kh = k_ref[0, r0:r0 + c_, lo:lo + DN_HEAD_DIM]
        vh = v_ref[0, r0:r0 + c_, lo:lo + DN_HEAD_DIM]
        kb = kh * beta
        eg = jnp.exp(gc)
        pre[d, c, h] = dict(decay=decay, kq_lhs=jnp.concatenate([kb, qh], axis=0), kh=kh,
                            rhs=jnp.concatenate([vh * beta, kb * eg], axis=1), q_dec=qh * eg,
                            k_dec=kh * jnp.exp(glast - gc), e_last=jnp.exp(glast))
    kq = {i: _bdot_nt(pre[i]["kq_lhs"], pre[i]["kh"]) for i in inst}
    a = {i: jnp.where(masks[i[0]]["strict"], kq[i][:c_] * pre[i]["decay"], 0.0) for i in inst}
    attn = {i: kq[i][c_:] * pre[i]["decay"] for i in inst}
    t = {i: masks[i[0]]["eye"] - jnp.where(masks[i[0]]["levels"][0], a[i], 0.0) for i in inst}
    for lvl in range(1, len(masks[0]["levels"])):
        ta = {i: _bdot(t[i], jnp.where(masks[i[0]]["levels"][lvl], a[i], 0.0)) for i in inst}
        t = {i: t[i] - _bdot(ta[i], t[i]) for i in inst}
    uw = {i: _bdot(t[i], pre[i]["rhs"]) for i in inst}
    for step in range(nchunk):
        group = [(d, order[d][step], h) for d in range(2) for h in range(DN_HEADS)]
        st = {i: s_scr[i[0], i[2]] for i in group}
        ws = {i: _bdot(jnp.concatenate([uw[i][:, DN_HEAD_DIM:], pre[i]["q_dec"]], axis=0), st[i]) for i in group}
        v_new = {i: uw[i][:, :DN_HEAD_DIM] - ws[i][:c_] for i in group}
        for i in group:
            d, c, h = i
            streams[d][4][0, c * c_:(c + 1) * c_, h * DN_HEAD_DIM:(h + 1) * DN_HEAD_DIM] = (
                ws[i][c_:] + _bdot(attn[i], v_new[i]))
        for i in group:
            s_scr[i[0], i[2]] = st[i] * pre[i]["e_last"] + _bdot_tn(pre[i]["k_dec"], v_new[i])

    @pl.when(j == nj - 1)
    def _():
        st_ref[0] = s_scr[...]


def _dn_scan(q, k, v, p, s0, tb):
    b_, t_, _ = q.shape
    nb = t_ // tb
    fwd = lambda wd: pl.BlockSpec((1, tb, wd), lambda b, j: (b, j, 0))
    bwd = lambda wd: pl.BlockSpec((1, tb, wd), lambda b, j: (b, nb - 1 - j, 0))
    widths = (DN_WIDTH, DN_WIDTH, DN_WIDTH, SMALL_W)
    st_shape = (2, DN_HEADS, DN_HEAD_DIM, DN_HEAD_DIM)
    st_spec = pl.BlockSpec((1,) + st_shape, lambda b, j: (b, 0, 0, 0, 0))
    return pl.pallas_call(
        _dn_kernel,
        grid=(b_, nb),
        in_specs=[fwd(w) for w in widths] + [bwd(w) for w in widths] + [st_spec],
        out_specs=[fwd(DN_WIDTH), bwd(DN_WIDTH), st_spec],
        out_shape=[jax.ShapeDtypeStruct((b_, t_, DN_WIDTH), F32)] * 2 + [jax.ShapeDtypeStruct((b_,) + st_shape, F32)],
        scratch_shapes=[pltpu.VMEM(st_shape, F32)],
        compiler_params=_cparams(("parallel", "arbitrary")),
        name="dn",
    )(q, k, v, p, q, k, v, p, s0)


def _log_sigmoid(t):
    return jnp.minimum(t, 0.0) - jnp.log(1.0 + jnp.exp(-jnp.abs(t)))


def _gla_kernel(qkf_ref, vf_ref, pf_ref, qkb_ref, vb_ref, pb_ref, wg_ref, bg_ref, s0_ref, of_ref, ob_ref, st_ref,
                s_scr):
    col = pl.program_id(1)
    rb = pl.program_id(2)
    ncol = pl.num_programs(1)
    nrb = pl.num_programs(2)

    @pl.when((col == 0) & (rb == 0))
    def _():
        s_scr[...] = s0_ref[0]

    c_ = CHUNK
    nchunk = qkf_ref.shape[1] // c_
    ncb = pf_ref.shape[2] // SMALL_W
    nsub = c_ // SUB
    streams = ((qkf_ref, vf_ref, pf_ref, of_ref), (qkb_ref, vb_ref, pb_ref, ob_ref))
    masks, order = [], []
    for rev in (False, True):
        ri, ci, incl, _ = _order_masks(c_, rev)
        pos_r = (c_ - 1 - ri) if rev else ri
        pos_c = (c_ - 1 - ci) if rev else ci
        first_rows = [(c_ - 1 - SUB * i) if rev else SUB * i for i in range(nsub)]
        masks.append(dict(incl=incl, diag=incl & ((pos_r // SUB) == (pos_c // SUB)),
                          off=(pos_c // SUB) < (pos_r // SUB), last=0 if rev else c_ - 1, first_rows=first_rows,
                          rev=rev))
        cols = [ncb - 1 - i if rev else i for i in range(ncb)]
        chunks = [nchunk - 1 - cc if rev else cc for cc in range(nchunk)]
        order.append([(cb, c) for cb in cols for c in chunks])
    nstep = len(order[0])
    la_col = {}
    for d in range(2):
        for cb in range(ncb):
            gate = _dot_split(streams[d][2][0, :, cb * SMALL_W:(cb + 1) * SMALL_W], wg_ref[d]) + bg_ref[d]
            la_col[d, cb] = _log_sigmoid(gate) * (1.0 / GLA_GATE_NORMALIZER)
    pre = {}
    for step in range(nstep):
        for d in range(2):
            cb, c = order[d][step]
            qk_ref, v_ref, _, _ = streams[d]
            m = masks[d]
            r0 = c * c_
            la = la_col[d, cb][r0:r0 + c_, :]
            bc = _dot_exact_lhs(m["incl"], la, terms=2)
            bex = bc - la
            b_last = bc[m["last"]:m["last"] + 1, :]
            pieces = [jnp.broadcast_to(bex[fr:fr + 1, :], (SUB, GLA_K_WIDTH)) for fr in m["first_rows"]]
            refd = jnp.concatenate(pieces[::-1] if m["rev"] else pieces, axis=0)
            mid = m["first_rows"][nsub // 2]
            rmid = bex[mid:mid + 1, :]
            qk0 = cb * 2 * GLA_K_WIDTH
            qa = qk_ref[0, r0:r0 + c_, qk0:qk0 + GLA_K_WIDTH] * GLA_DK ** -0.5
            ka = qk_ref[0, r0:r0 + c_, qk0 + GLA_K_WIDTH:qk0 + 2 * GLA_K_WIDTH]
            q_in = qa * jnp.exp(bc)
            k_st = ka * jnp.exp(b_last - bc)
            q_d = qa * jnp.exp(bc - refd)
            k_d = ka * jnp.exp(jnp.minimum(refd - bc, EXP_CLAMP))
            q_o = qa * jnp.exp(jnp.minimum(bc - rmid, EXP_CLAMP))
            k_o = ka * jnp.exp(jnp.minimum(rmid - bc, EXP_CLAMP))
            e_last = jnp.exp(b_last)
            v0 = cb * GLA_V_WIDTH
            for h in range(GLA_HEADS):
                ks = slice(h * GLA_DK, (h + 1) * GLA_DK)
                vh = v_ref[0, r0:r0 + c_, v0 + h * GLA_DV:v0 + (h + 1) * GLA_DV]
                attn = (jnp.where(m["diag"], _bdot_nt(q_d[:, ks], k_d[:, ks]), 0.0)
                        + jnp.where(m["off"], _bdot_nt(q_o[:, ks], k_o[:, ks]), 0.0))
                pre[d, step, h] = dict(attn=attn, vh=vh, q_in=q_in[:, ks], e_last=e_last[:, ks],
                                       kv=_bdot_tn(vh, k_st[:, ks]))
    st = {(d, h): s_scr[d, h] for d in range(2) for h in range(GLA_HEADS)}
    for step in range(nstep):
        for d in range(2):
            cb, c = order[d][step]
            r0 = c * c_
            v0 = cb * GLA_V_WIDTH
            for h in range(GLA_HEADS):
                i = (d, step, h)
                streams[d][3][0, r0:r0 + c_, v0 + h * GLA_DV:v0 + (h + 1) * GLA_DV] = (
                    _bdot_nt(pre[i]["q_in"], st[d, h]) + _bdot(pre[i]["attn"], pre[i]["vh"]))
                st[d, h] = st[d, h] * pre[i]["e_last"] + pre[i]["kv"]
    for d in range(2):
        for h in range(GLA_HEADS):
            s_scr[d, h] = st[d, h]

    @pl.when((col == ncol - 1) & (rb == nrb - 1))
    def _():
        st_ref[0] = s_scr[...]


def _gla_scan(qk, v, p, wg, bg, s0, ncol, tr):
    b_, rows, _ = qk.shape
    nrb = rows // tr
    ncb = min(GLA_COLS_PER_STEP, ncol)
    ncblk = ncol // ncb
    assert ncb == 1 or nrb == 1
    fwd = lambda wd: pl.BlockSpec((1, tr, ncb * wd), lambda b, c, r: (b, r, c))
    bwd = lambda wd: pl.BlockSpec((1, tr, ncb * wd), lambda b, c, r: (b, nrb - 1 - r, ncblk - 1 - c))
    widths = (2 * GLA_K_WIDTH, GLA_V_WIDTH, SMALL_W)
    st_shape = (2, GLA_HEADS, GLA_DV, GLA_DK)
    st_spec = pl.BlockSpec((1,) + st_shape, lambda b, c, r: (b, 0, 0, 0, 0))
    return pl.pallas_call(
        _gla_kernel,
        grid=(b_, ncblk, nrb),
        in_specs=[fwd(w) for w in widths] + [bwd(w) for w in widths] + [
            pl.BlockSpec((2, SMALL_W, GLA_K_WIDTH), lambda b, c, r: (0, 0, 0)),
            pl.BlockSpec((2, 1, GLA_K_WIDTH), lambda b, c, r: (0, 0, 0)),
            st_spec],
        out_specs=[fwd(GLA_V_WIDTH), bwd(GLA_V_WIDTH), st_spec],
        out_shape=[jax.ShapeDtypeStruct((b_, rows, ncol * GLA_V_WIDTH), F32)] * 2 + [
            jax.ShapeDtypeStruct((b_,) + st_shape, F32)],
        scratch_shapes=[pltpu.VMEM(st_shape, F32)],
        compiler_params=_cparams(("parallel", "arbitrary", "arbitrary")),
        name="gla",
    )(qk, v, p, qk, v, p, wg, bg, s0)


def _head_norm_gate(o, z, gain):
    parts = []
    for h in range(o.shape[1] // LANES):
        hs = slice(h * LANES, (h + 1) * LANES)
        parts.append(_rms(o[:, hs], gain) * _silu(z[:, hs].astype(F32)))
    return jnp.concatenate(parts, axis=1)


def _mixout_kernel(*refs, with_router, ncol):
    (x_ref, odf_ref, odb_ref, dnz_ref, ogf_ref, ogb_ref, glz_ref, mod_ref, gd_ref, gg_ref, w_ref, gf_ref) = refs[:12]
    refs = list(refs[12:])
    col_scr = refs.pop() if ncol > 1 else None
    if with_router:
        rw_ref, xo_ref, h_ref, cmb_ref = refs
    else:
        xo_ref, h_ref = refs
    m_dn = _head_norm_gate(odf_ref[0] + odb_ref[0], dnz_ref[0], gd_ref[...])
    o_gl = ogf_ref[0] + ogb_ref[0]
    if ncol > 1:
        o_gl = _from_columns(o_gl, col_scr, ncol)
    m_gl = _head_norm_gate(o_gl, glz_ref[0], gg_ref[...])
    m = jnp.concatenate([m_dn, m_gl], axis=1).astype(BF16)
    x = x_ref[0] + mod_ref[0, 2:3, :] * jnp.dot(m, w_ref[...], preferred_element_type=F32)
    xo_ref[0] = x
    h = _rms(x, gf_ref[...]) * (1.0 + mod_ref[0, 4:5, :]) + mod_ref[0, 3:4, :]
    h_b = h.astype(BF16)
    h_ref[0] = h_b
    if with_router:
        rw_hi, rw_lo = _split_bf16(rw_ref[...], 2)
        logits = (jnp.dot(h_b, rw_hi, preferred_element_type=F32)
                  + jnp.dot(h_b, rw_lo, preferred_element_type=F32))
        lane = lax.broadcasted_iota(jnp.int32, logits.shape, 1).astype(F32)
        logits = jnp.where(lane < N_EXPERTS, logits, -jnp.inf)
        e = jnp.exp(logits - jnp.max(logits, axis=-1, keepdims=True))
        probs = e / jnp.sum(e, axis=-1, keepdims=True)
        p1 = jnp.max(probs, axis=-1, keepdims=True)
        i1 = jnp.min(jnp.where(probs == p1, lane, float(LANES)), axis=-1, keepdims=True)
        rest = jnp.where(lane == i1, -1.0, probs)
        p2 = jnp.max(rest, axis=-1, keepdims=True)
        i2 = jnp.min(jnp.where(rest == p2, lane, float(LANES)), axis=-1, keepdims=True)
        cmb_ref[0] = (jnp.where(lane == i1, p1, 0.0) + jnp.where(lane == i2, p2, 0.0)) / (p1 + p2)


def _mixout(x, odf, odb, dnz, ogf, ogb, glz, mods, mod_row, gd, gg, w_out, gf, router, tm, ncol):
    b_, t_, _ = x.shape
    with_router = router is not None
    tok = lambda wd: pl.BlockSpec((1, tm, wd), lambda b, j: (b, j, 0))
    colm = lambda wd: pl.BlockSpec((1, tm // ncol, ncol * wd), lambda b, j: (b, j, 0))
    const = lambda shp: pl.BlockSpec(shp, lambda b, j: (0,) * len(shp))
    in_specs = [tok(D_MODEL)] + [tok(DN_WIDTH)] * 3 + [colm(GLA_V_WIDTH)] * 2 + [tok(GLA_V_WIDTH)] + [
        pl.BlockSpec((1, SUBLANES, D_MODEL), lambda b, j: (mod_row(b), 0, 0)),
        const((1, LANES)), const((1, LANES)), const((D_MODEL, D_MODEL)), const((1, D_MODEL))]
    args = [x, odf, odb, dnz, ogf, ogb, glz, mods, gd.reshape(1, LANES), gg.reshape(1, LANES), w_out,
            gf.reshape(1, D_MODEL)]
    out_specs = [tok(D_MODEL), tok(D_MODEL)]
    out_shape = [jax.ShapeDtypeStruct((b_, t_, D_MODEL), F32), jax.ShapeDtypeStruct((b_, t_, D_MODEL), BF16)]
    if with_router:
        in_specs.append(const((D_MODEL, LANES)))
        args.append(router)
        out_specs.append(tok(LANES))
        out_shape.append(jax.ShapeDtypeStruct((b_, t_, LANES), F32))
    scratch = [_col_scratch(GLA_V_WIDTH // LANES, tm // ncol, ncol)] if ncol > 1 else []
    return pl.pallas_call(
        functools.partial(_mixout_kernel, with_router=with_router, ncol=ncol),
        grid=(b_, t_ // tm),
        in_specs=in_specs,
        out_specs=out_specs,
        out_shape=out_shape,
        scratch_shapes=scratch,
        compiler_params=_cparams(("parallel", "parallel")),
        name="mixout",
    )(*args)


def _ffn_kernel(x_ref, h_ref, mod_ref, wg_ref, wu_ref, wd_ref, o_ref, *, nsplit):
    h = h_ref[0]
    f = wg_ref.shape[1]
    fs = f // nsplit
    acc = None
    for s in range(nsplit):
        g = jnp.dot(h, wg_ref[:, s * fs:(s + 1) * fs], preferred_element_type=F32)
        u = jnp.dot(h, wu_ref[:, s * fs:(s + 1) * fs], preferred_element_type=F32)
        part = jnp.dot((_silu(g) * u).astype(BF16), wd_ref[s * fs:(s + 1) * fs, :], preferred_element_type=F32)
        acc = part if acc is None else acc + part
    o_ref[0] = x_ref[0] + mod_ref[0, 5:6, :] * acc


def _ffn(x, h, mods, mod_row, w_gu, w_down, tm):
    b_, t_, _ = x.shape
    f = w_down.shape[0]
    tok = lambda: pl.BlockSpec((1, tm, D_MODEL), lambda b, j: (b, j, 0))
    return pl.pallas_call(
        functools.partial(_ffn_kernel, nsplit=2),
        grid=(b_, t_ // tm),
        in_specs=[tok(), tok(),
                  pl.BlockSpec((1, SUBLANES, D_MODEL), lambda b, j: (mod_row(b), 0, 0)),
                  pl.BlockSpec((D_MODEL, f), lambda b, j: (0, 0)),
                  pl.BlockSpec((D_MODEL, f), lambda b, j: (0, 1)),
                  pl.BlockSpec((f, D_MODEL), lambda b, j: (0, 0))],
        out_specs=tok(),
        out_shape=jax.ShapeDtypeStruct((b_, t_, D_MODEL), F32),
        compiler_params=_cparams(("parallel", "parallel")),
        name="ffn",
    )(x, h, mods, w_gu, w_gu, w_down)


def _for_row_tiles(ntiles, fn):
    rows = MOE_GROUP * MOE_TILE
    single = ntiles == MOE_GROUP + 1
    keep = 1 - single.astype(jnp.int32)
    ngroup = (ntiles // MOE_GROUP) * keep
    rest = (ntiles % MOE_GROUP) * keep

    def body(i, carry):
        fn(pl.multiple_of(i * rows, rows), rows)
        return carry

    lax.fori_loop(0, ngroup, body, 0)

    def tail(i, carry):
        fn(pl.multiple_of(ngroup * rows + i * MOE_TILE, MOE_TILE), MOE_TILE)
        return carry

    lax.fori_loop(0, rest, tail, 0)

    @pl.when(single)
    def _():
        fn(0, rows + MOE_TILE)


def _moe_kernel(cnt_ref, x_ref, h_ref, cmb_ref, mod_ref, wg_ref, wu_ref, wd_ref, fg_ref, o_ref,
                acc_ref, xe_ref, ye_ref, rkc_ref, rkr_ref, *, final_norm):
    b = pl.program_id(0)
    j = pl.program_id(1)
    e = pl.program_id(2)
    f = pl.program_id(3)
    nj = pl.num_programs(1)
    ne = pl.num_programs(2)
    nf = pl.num_programs(3)
    tm = h_ref.shape[1]
    ntiles = (cnt_ref[(b * nj + j) * ne + e] + MOE_TILE - 1) // MOE_TILE

    @pl.when((e == 0) & (f == 0))
    def _():
        acc_ref[...] = jnp.zeros_like(acc_ref)
        sb = 2 * LANES
        ri = lax.broadcasted_iota(jnp.int32, (sb, sb), 0)
        ci = lax.broadcasted_iota(jnp.int32, (sb, sb), 1)
        before = (ci < ri).astype(BF16)
        carry = jnp.zeros((1, LANES), F32)
        for s in range(tm // sb):
            sel = (cmb_ref[0, s * sb:(s + 1) * sb, :] > 0.0).astype(F32)
            rank = jnp.dot(before, sel.astype(BF16), preferred_element_type=F32) + carry
            rkc_ref[s * sb:(s + 1) * sb, :] = jnp.where(sel > 0.0, rank, -1.0)
            carry = carry + jnp.sum(sel, axis=0, keepdims=True)
        rkr_ref[...] = rkc_ref[...].T

    @pl.when(f == 0)
    def _():
        rank_row = rkr_ref[pl.ds(e, 1), :]

        def gather(r0, n):
            slot = (lax.broadcasted_iota(jnp.int32, (n, tm), 0) + r0).astype(F32)
            onehot = (rank_row == slot).astype(BF16)
            xe_ref[pl.ds(r0, n), :] = jnp.dot(onehot, h_ref[0], preferred_element_type=F32).astype(BF16)

        _for_row_tiles(ntiles, gather)

    def expert(r0, n):
        xr = xe_ref[pl.ds(r0, n), :]
        g = jnp.dot(xr, wg_ref[0], preferred_element_type=F32)
        u = jnp.dot(xr, wu_ref[0], preferred_element_type=F32)
        y = jnp.dot((_silu(g) * u).astype(BF16), wd_ref[0], preferred_element_type=F32)

        @pl.when(f == 0)
        def _():
            ye_ref[pl.ds(r0, n), :] = y

        @pl.when(f > 0)
        def _():
            ye_ref[pl.ds(r0, n), :] += y

    _for_row_tiles(ntiles, expert)

    @pl.when(f == nf - 1)
    def _():
        lane = lax.broadcasted_iota(jnp.int32, (tm, LANES), 1)
        rank_col = jnp.sum(jnp.where(lane == e, rkc_ref[...], 0.0), axis=-1, keepdims=True)
        ce = jnp.sum(jnp.where(lane == e, cmb_ref[0], 0.0), axis=-1, keepdims=True)

        def scatter(r0, n):
            slot = (lax.broadcasted_iota(jnp.int32, (tm, n), 1) + r0).astype(F32)
            onehot = (rank_col == slot).astype(BF16)
            back = jnp.dot(onehot, ye_ref[pl.ds(r0, n), :].astype(BF16), preferred_element_type=F32)
            acc_ref[...] += ce * back

        _for_row_tiles(ntiles, scatter)

    @pl.when((e == ne - 1) & (f == nf - 1))
    def _():
        y = x_ref[0] + mod_ref[0, 5:6, :] * acc_ref[...]
        if final_norm:
            y = _rms(y, fg_ref[...])
        o_ref[0] = y


def _moe(x, h, cmb, mods, mod_row, w_gu, w_down, final_gain, tm, tf):
    b_, t_, _ = x.shape
    ne, _, f2 = w_gu.shape
    fe = f2 // 2
    assert fe % tf == 0 and tm % (MOE_GROUP * MOE_TILE) == 0 and ne <= SUBLANES
    nf = fe // tf
    nj = t_ // tm
    counts = jnp.sum((cmb[..., :ne] > 0.0).reshape(b_, nj, tm, ne), axis=2, dtype=jnp.int32).reshape(-1)
    tok = lambda wd: pl.BlockSpec((1, tm, wd), lambda b, j, e, f, cnt: (b, j, 0))
    final_norm = final_gain is not None
    fg = (final_gain if final_norm else jnp.ones((D_MODEL,), F32)).reshape(1, D_MODEL)
    grid_spec = pltpu.PrefetchScalarGridSpec(
        num_scalar_prefetch=1,
        grid=(b_, nj, ne, nf),
        in_specs=[tok(D_MODEL), tok(D_MODEL), tok(LANES),
                  pl.BlockSpec((1, SUBLANES, D_MODEL), lambda b, j, e, f, cnt: (mod_row(b), 0, 0)),
                  pl.BlockSpec((1, D_MODEL, tf), lambda b, j, e, f, cnt: (e, 0, f)),
                  pl.BlockSpec((1, D_MODEL, tf), lambda b, j, e, f, cnt: (e, 0, nf + f)),
                  pl.BlockSpec((1, tf, D_MODEL), lambda b, j, e, f, cnt: (e, f, 0)),
                  pl.BlockSpec((1, D_MODEL), lambda b, j, e, f, cnt: (0, 0))],
        out_specs=tok(D_MODEL),
        scratch_shapes=[pltpu.VMEM((tm, D_MODEL), F32),
                        pltpu.VMEM((tm, D_MODEL), BF16),
                        pltpu.VMEM((tm, D_MODEL), F32),
                        pltpu.VMEM((tm, LANES), F32),
                        pltpu.VMEM((LANES, tm), F32)])
    return pl.pallas_call(
        functools.partial(_moe_kernel, final_norm=final_norm),
        grid_spec=grid_spec,
        out_shape=jax.ShapeDtypeStruct((b_, t_, D_MODEL), F32),
        compiler_params=_cparams(("parallel", "parallel", "arbitrary", "arbitrary")),
        name="moe",
    )(counts, x, h, cmb, mods, w_gu, w_gu, w_down, fg)


def _pack_w_in(w):
    offs = np.cumsum((0, QKV_W, DN_WIDTH, 2 * DN_HEADS, 2 * DN_HEADS, GLA_K_WIDTH, GLA_K_WIDTH, GLA_V_WIDTH,
                      GLA_V_WIDTH, 2 * GLA_GATE_RANK))
    seg = lambda i: w[:, offs[i]:offs[i + 1]]
    small = jnp.concatenate([seg(2), seg(3), seg(8)], axis=1)
    small = jnp.pad(small, ((0, 0), (0, SMALL_W - small.shape[1])))
    return jnp.concatenate([seg(0), seg(1), seg(4), seg(5), seg(6), seg(7), small], axis=1).astype(BF16)


def _pack_gate(w_gate2, b_gate):
    ws = []
    for d in range(2):
        lo = GLR_LANE + d * GLA_GATE_RANK
        ws.append(jnp.zeros((SMALL_W, GLA_K_WIDTH), F32).at[lo:lo + GLA_GATE_RANK].set(w_gate2[d]))
    return jnp.stack(ws), b_gate.reshape(2, 1, GLA_K_WIDTH)


def _pack_dn_params(a_log, dt_bias):
    neg_a = jnp.zeros((SMALL_W,), F32).at[GDEC_LANE:GLR_LANE].set(-jnp.exp(a_log.reshape(-1)))
    dtb = jnp.zeros((SMALL_W,), F32).at[GDEC_LANE:GLR_LANE].set(dt_bias.reshape(-1))
    return jnp.zeros((SUBLANES, SMALL_W), F32).at[0].set(neg_a).at[1].set(dtb)


def _pad_rows(a, rows):
    return jnp.pad(a, ((0, rows - a.shape[0]),) + ((0, 0),) * (a.ndim - 1))


def _mixers(x_seq, mods, mod_row, gain_mix, w_in_p, cw, ap, wgs, bgs, states, ncol, tm, tb):
    b_, t_, _ = x_seq.shape
    q, k, v, p, p_col, dnz, gqk, gv, gz = _proj(x_seq, mods, mod_row, gain_mix, w_in_p, cw, ap, tm, ncol)
    rows = t_ // ncol
    o_dn_f, o_dn_b, s_dn = _dn_scan(q, k, v, p, states[0], tb)
    o_gl_f, o_gl_b, s_gl = _gla_scan(gqk, gv, p_col, wgs, bgs, states[1], ncol, min(rows, 2 * CHUNK))
    outs = [(o_dn_f, o_gl_f), (o_dn_b, o_gl_b)]
    return outs, dnz, gz, (s_dn, s_gl)


def kernel(x, c, ctx, c_ctx, w_ada, b_ada, norm_mix, norm_ffn, w_in, conv_qkv, dn_a_log, dn_dt_bias, dn_norm,
           gla_w_gate2, gla_b_gate, gla_norm, w_out, ffn_w_gu, ffn_w_down, moe_router, moe_w_gu, moe_w_down,
           final_norm):
    b_, seq, _ = x.shape
    depth = w_ada.shape[0]
    ctx_len = ctx.shape[1]
    mod_rows = -(-(b_ + 1) // SUBLANES) * SUBLANES
    c_rows = _pad_rows(jnp.concatenate([c, c_ctx[None, :]], axis=0), mod_rows)
    lat_row = lambda b: b
    ctx_row = lambda b: b_
    tm_lat, tm_ctx = 512, ctx_len
    tb = 4 * CHUNK

    for i in range(depth):
        last = i == depth - 1
        mods = _ada(c_rows, w_ada[i], b_ada[i]).reshape(mod_rows, N_MOD, D_MODEL)
        mods = jnp.pad(mods, ((0, 0), (0, SUBLANES - N_MOD), (0, 0)))
        w_in_p = _pack_w_in(w_in[i])
        cw = _pad_rows(conv_qkv[i], SUBLANES)
        ap = _pack_dn_params(dn_a_log[i], dn_dt_bias[i])
        wgs, bgs = _pack_gate(gla_w_gate2[i], gla_b_gate[i])
        s0 = (jnp.zeros((b_, 2, DN_HEADS, DN_HEAD_DIM, DN_HEAD_DIM), F32),
              jnp.zeros((b_, 2, GLA_HEADS, GLA_DV, GLA_DK), F32))
        outs_c, dnz_c, gz_c, s_ctx = _mixers(ctx, mods, ctx_row, norm_mix[i], w_in_p, cw, ap, wgs, bgs, s0, 1,
                                             tm_ctx, tb)
        outs_l, dnz_l, gz_l, _ = _mixers(x, mods, lat_row, norm_mix[i], w_in_p, cw, ap, wgs, bgs, s_ctx, GRID_W,
                                         tm_lat, tb)
        w_out_b = w_out[i].astype(BF16)
        is_moe = i % 2 == 1
        if is_moe:
            router = jnp.pad(moe_router[i // 2], ((0, 0), (0, LANES - N_EXPERTS)))
            w_gu = moe_w_gu[i // 2].astype(BF16)
            w_dn = moe_w_down[i // 2].astype(BF16)
        else:
            router = None
            w_gu = ffn_w_gu[i // 2].astype(BF16)
            w_dn = ffn_w_down[i // 2].astype(BF16)

        def channel(tokens, outs, dnz, gz, row, tm, ncol, final_gain):
            res = _mixout(tokens, outs[0][0], outs[1][0], dnz, outs[0][1], outs[1][1], gz, mods, row, dn_norm[i],
                          gla_norm[i], w_out_b, norm_ffn[i], router, tm, ncol)
            if is_moe:
                x_mid, h, cmb = res
                tm_moe = min(1024, tokens.shape[1])
                return _moe(x_mid, h, cmb, mods, row, w_gu, w_dn, final_gain, tm_moe, D_EXPERT_TILE)
            x_mid, h = res
            y = _ffn(x_mid, h, mods, row, w_gu, w_dn, tm)
            if final_gain is not None:
                raise NotImplementedError("final norm is fused into the expert mixer only")
            return y

        x = channel(x, outs_l, dnz_l, gz_l, lat_row, tm_lat, GRID_W, final_norm if last else None)
        if not last:
            ctx = channel(ctx, outs_c, dnz_c, gz_c, ctx_row, tm_ctx, 1, None)
    return x
```

```python
import functools

import jax
import jax.numpy as jnp
import numpy as np
from jax import lax
from jax.experimental import pallas as pl
from jax.experimental.pallas import tpu as pltpu

F32 = jnp.float32
BF16 = jnp.bfloat16
HIGHEST = lax.Precision.HIGHEST

D_MODEL = 1024
GRID_W = 64
DN_HEADS = 4
DN_HEAD_DIM = 128
DN_WIDTH = DN_HEADS * DN_HEAD_DIM
CONV_W = 5
GLA_HEADS = 4
GLA_DK = 64
GLA_DV = 128
GLA_K_WIDTH = GLA_HEADS * GLA_DK
GLA_V_WIDTH = GLA_HEADS * GLA_DV
GLA_GATE_RANK = 16
GLA_GATE_NORMALIZER = 16.0
N_EXPERTS = 8
N_MOD = 6
NORM_EPS = 1e-6

LANES = 128
SUBLANES = 8
CHUNK = 64
SUB = 16
EXP_CLAMP = 80.0
D_EXPERT_TILE = 896
MOE_TILE = 64
MOE_GROUP = 4
GLA_COLS_PER_STEP = 8
VMEM_LIMIT = 56 * 1024 * 1024

QKV_W = 3 * DN_WIDTH
SMALL_W = LANES
PROJ_WIDTHS = (QKV_W, DN_WIDTH, 2 * GLA_K_WIDTH, GLA_V_WIDTH, GLA_V_WIDTH, SMALL_W)
PROJ_COLS = sum(PROJ_WIDTHS)
BETA_LANE = 0
GDEC_LANE = 2 * DN_HEADS
GLR_LANE = 4 * DN_HEADS


def _cparams(sem, vmem=VMEM_LIMIT):
    return pltpu.CompilerParams(dimension_semantics=sem, vmem_limit_bytes=vmem)


def _sigmoid(t):
    return 1.0 / (1.0 + jnp.exp(-t))


def _silu(t):
    return t * _sigmoid(t)


def _softplus(t):
    return jnp.maximum(t, 0.0) + jnp.log(1.0 + jnp.exp(-jnp.abs(t)))


def _bdot(a, b):
    return jnp.dot(a.astype(BF16), b.astype(BF16), preferred_element_type=F32)


def _bdot_nt(a, b):
    return lax.dot_general(a.astype(BF16), b.astype(BF16), (((1,), (1,)), ((), ())), preferred_element_type=F32)


def _bdot_tn(a, b):
    return lax.dot_general(a.astype(BF16), b.astype(BF16), (((0,), (0,)), ((), ())), preferred_element_type=F32)


def _split_bf16(t, terms):
    parts = []
    for _ in range(terms):
        part = t.astype(BF16)
        parts.append(part)
        t = t - part.astype(F32)
    return parts


def _dot_exact_lhs(a, x, terms=2):
    a = a.astype(BF16)
    out = None
    for part in _split_bf16(x, terms):
        d = jnp.dot(a, part, preferred_element_type=F32)
        out = d if out is None else out + d
    return out


def _dot_split(a, b):
    a_hi, a_lo = _split_bf16(a, 2)
    b_hi, b_lo = _split_bf16(b, 2)
    dot = lambda u, v: jnp.dot(u, v, preferred_element_type=F32)
    return dot(a_hi, b_hi) + (dot(a_lo, b_hi) + dot(a_hi, b_lo))


def _rms(t, gain):
    return t * lax.rsqrt(jnp.mean(t * t, axis=-1, keepdims=True) + NORM_EPS) * gain


def _ada_kernel(c_ref, w_ref, b_ref, o_ref):
    s = _silu(c_ref[...])
    o_ref[...] = jnp.dot(s, w_ref[...], precision=HIGHEST, preferred_element_type=F32) + b_ref[...]


def _ada(c_rows, w, b):
    rows = c_rows.shape[0]
    n = w.shape[1]
    tn = D_MODEL
    return pl.pallas_call(
        _ada_kernel,
        grid=(n // tn,),
        in_specs=[pl.BlockSpec((rows, D_MODEL), lambda j: (0, 0)),
                  pl.BlockSpec((D_MODEL, tn), lambda j: (0, j)),
                  pl.BlockSpec((1, tn), lambda j: (0, j))],
        out_specs=pl.BlockSpec((rows, tn), lambda j: (0, j)),
        out_shape=jax.ShapeDtypeStruct((rows, n), F32),
        compiler_params=_cparams(("arbitrary",)),
        name="ada",
    )(c_rows, w, b.reshape(1, n))


def _to_columns(val, out_ref, scr_ref, ncol):
    nrows, width = val.shape
    groups = width // LANES
    grid_rows = nrows // ncol
    pitch = _col_pitch(ncol)
    for g in range(groups):
        for r in range(grid_rows):
            scr_ref[g, r * pitch:r * pitch + ncol, :] = val[r * ncol:(r + 1) * ncol, g * LANES:(g + 1) * LANES]
    for c in range(ncol):
        for g in range(groups):
            lo = c * width + g * LANES
            out_ref[0, :, lo:lo + LANES] = scr_ref[g, pl.ds(c, grid_rows, stride=pitch), :]


def _from_columns(in_val, scr_ref, ncol):
    grid_rows = in_val.shape[0]
    width = in_val.shape[1] // ncol
    groups = width // LANES
    pitch = _col_pitch(ncol)
    for c in range(ncol):
        for g in range(groups):
            lo = c * width + g * LANES
            scr_ref[g, pl.ds(c, grid_rows, stride=pitch), :] = in_val[:, lo:lo + LANES]
    return jnp.concatenate(
        [jnp.concatenate([scr_ref[g, r * pitch:r * pitch + ncol, :] for r in range(grid_rows)], axis=0)
         for g in range(groups)], axis=1)


def _col_pitch(ncol):
    return ncol + SUBLANES // 2


def _col_scratch(groups, grid_rows, ncol):
    rows = -(-grid_rows * _col_pitch(ncol) // SUBLANES) * SUBLANES
    return pltpu.VMEM((groups, rows, LANES), F32)


def _proj_kernel(x_ref, xp_ref, xn_ref, mod_ref, gain_ref, w_ref, cw_ref, ap_ref,
                 q_ref, k_ref, v_ref, p_ref, dnz_ref, gqk_ref, gv_ref, gz_ref, *rest, ncol):
    if ncol > 1:
        pcol_ref, ext_ref, col_scr = rest
    else:
        (ext_ref,) = rest
    j = pl.program_id(1)
    nj = pl.num_programs(1)
    tm = x_ref.shape[1]
    halo = SUBLANES
    pad = CONV_W // 2
    norm_mod = lambda t: _rms(t, gain_ref[...]) * (1.0 + mod_ref[0, 1:2, :]) + mod_ref[0, 0:1, :]
    h = norm_mod(x_ref[0])
    h_all = jnp.concatenate([h, norm_mod(xp_ref[0]), norm_mod(xn_ref[0])], axis=0).astype(BF16)
    r_qkv = jnp.dot(h_all, w_ref[:, 0:QKV_W], preferred_element_type=F32)
    r = jnp.dot(h.astype(BF16), w_ref[:, QKV_W:], preferred_element_type=F32)
    o = 0
    for ref, wd in zip((dnz_ref, gqk_ref, gv_ref, gz_ref), PROJ_WIDTHS[1:5]):
        if ncol > 1 and (ref is gqk_ref or ref is gv_ref):
            _to_columns(r[:, o:o + wd], ref, col_scr, ncol)
        else:
            ref[0] = r[:, o:o + wd].astype(ref.dtype)
        o += wd
    sm = r[:, o:o + SMALL_W]

    ext_ref[0:halo, :] = jnp.where(j > 0, r_qkv[tm:tm + halo], 0.0)
    ext_ref[halo:halo + tm, :] = r_qkv[0:tm]
    ext_ref[halo + tm:2 * halo + tm, :] = jnp.where(j < nj - 1, r_qkv[tm + halo:], 0.0)
    acc = None
    for w in range(CONV_W):
        term = ext_ref[pl.ds(halo - pad + w, tm), :] * cw_ref[w:w + 1, :]
        acc = term if acc is None else acc + term
    s = _silu(acc)
    for hd in range(DN_HEADS):
        lo = hd * DN_HEAD_DIM
        qh = s[:, lo:lo + DN_HEAD_DIM]
        kh = s[:, DN_WIDTH + lo:DN_WIDTH + lo + DN_HEAD_DIM]
        q_ref[0, :, lo:lo + DN_HEAD_DIM] = (qh * lax.rsqrt(jnp.sum(qh * qh, axis=-1, keepdims=True) + NORM_EPS)
                                            * DN_HEAD_DIM ** -0.5)
        k_ref[0, :, lo:lo + DN_HEAD_DIM] = kh * lax.rsqrt(jnp.sum(kh * kh, axis=-1, keepdims=True) + NORM_EPS)
    v_ref[0] = s[:, 2 * DN_WIDTH:]
    lane = lax.broadcasted_iota(jnp.int32, sm.shape, 1)
    beta = _sigmoid(sm)
    gdec = ap_ref[0:1, :] * _softplus(sm + ap_ref[1:2, :])
    p = jnp.where(lane < GDEC_LANE, beta, jnp.where(lane < GLR_LANE, gdec, sm))
    p_ref[0] = p
    if ncol > 1:
        _to_columns(p, pcol_ref, col_scr, ncol)


def _proj(x, mods, mod_row, gain, w, cw, ap, tm, ncol):
    b_, t_, _ = x.shape
    r8 = tm // SUBLANES
    last8 = t_ // SUBLANES - 1
    tok = lambda wd: pl.BlockSpec((1, tm, wd), lambda b, j: (b, j, 0))
    colm = lambda wd: pl.BlockSpec((1, tm // ncol, ncol * wd), lambda b, j: (b, j, 0))
    const = lambda shp: pl.BlockSpec(shp, lambda b, j: (0,) * len(shp))
    out_specs = [tok(DN_WIDTH), tok(DN_WIDTH), tok(DN_WIDTH), tok(SMALL_W), tok(DN_WIDTH),
                 colm(2 * GLA_K_WIDTH), colm(GLA_V_WIDTH), tok(GLA_V_WIDTH)]
    tshape = lambda wd, dt=F32: jax.ShapeDtypeStruct((b_, t_, wd), dt)
    cshape = lambda wd: jax.ShapeDtypeStruct((b_, t_ // ncol, ncol * wd), F32)
    out_shape = [tshape(DN_WIDTH)] * 3 + [tshape(SMALL_W), tshape(DN_WIDTH, BF16), cshape(2 * GLA_K_WIDTH),
                                          cshape(GLA_V_WIDTH), tshape(GLA_V_WIDTH, BF16)]
    scratch = [pltpu.VMEM((tm + 2 * SUBLANES, QKV_W), F32)]
    if ncol > 1:
        out_specs.append(colm(SMALL_W))
        out_shape.append(cshape(SMALL_W))
        scratch.append(_col_scratch(GLA_V_WIDTH // LANES, tm // ncol, ncol))
    outs = pl.pallas_call(
        functools.partial(_proj_kernel, ncol=ncol),
        grid=(b_, t_ // tm),
        in_specs=[tok(D_MODEL),
                  pl.BlockSpec((1, SUBLANES, D_MODEL), lambda b, j: (b, jnp.maximum(j * r8 - 1, 0), 0)),
                  pl.BlockSpec((1, SUBLANES, D_MODEL), lambda b, j: (b, jnp.minimum((j + 1) * r8, last8), 0)),
                  pl.BlockSpec((1, SUBLANES, D_MODEL), lambda b, j: (mod_row(b), 0, 0)),
                  const((1, D_MODEL)), const((D_MODEL, PROJ_COLS)), const((SUBLANES, QKV_W)),
                  const((SUBLANES, SMALL_W))],
        out_specs=out_specs,
        out_shape=out_shape,
        scratch_shapes=scratch,
        compiler_params=_cparams(("parallel", "parallel")),
        name="proj",
    )(x, x, x, mods, gain.reshape(1, D_MODEL), w, cw, ap)
    q, k, v, p, dnz, gqk, gv, gz = outs[:8]
    p_col = outs[8] if ncol > 1 else p
    return q, k, v, p, p_col, dnz, gqk, gv, gz


def _order_masks(n, rev):
    ri = lax.broadcasted_iota(jnp.int32, (n, n), 0)
    ci = lax.broadcasted_iota(jnp.int32, (n, n), 1)
    if rev:
        return ri, ci, ci >= ri, ci > ri
    return ri, ci, ci <= ri, ci < ri


def _dn_kernel(qf_ref, kf_ref, vf_ref, pf_ref, qb_ref, kb_ref, vb_ref, pb_ref, s0_ref, of_ref, ob_ref, st_ref, s_scr):
    j = pl.program_id(1)
    nj = pl.num_programs(1)

    @pl.when(j == 0)
    def _():
        s_scr[...] = s0_ref[0]

    c_ = CHUNK
    tb = qf_ref.shape[1]
    nchunk = tb // c_
    streams = ((qf_ref, kf_ref, vf_ref, pf_ref, of_ref), (qb_ref, kb_ref, vb_ref, pb_ref, ob_ref))
    masks, gcum, gcum_t, order = [], [], [], []
    for d, rev in enumerate((False, True)):
        ri, ci, incl, strict = _order_masks(c_, rev)
        levels = []
        s = 1
        while s < c_:
            levels.append(strict & ((ri // (2 * s)) == (ci // (2 * s))) & ((ri // s) != (ci // s)))
            s *= 2
        masks.append(dict(incl=incl, strict=strict, eye=(ri == ci).astype(F32), levels=levels,
                          last=0 if rev else c_ - 1))
        rb, cb, incl_b, _ = _order_masks(tb, rev)
        cum_mat = incl_b & ((rb // c_) == (cb // c_))
        gcum.append(_dot_exact_lhs(cum_mat, streams[d][3][0], terms=3))
        gcum_t.append(gcum[d].T)
        order.append([nchunk - 1 - cc if rev else cc for cc in range(nchunk)])

    inst = [(d, c, h) for step in range(nchunk) for d in range(2) for c in (order[d][step],)
            for h in range(DN_HEADS)]
    pre = {}
    for d, c, h in inst:
        q_ref, k_ref, v_ref, p_ref, _ = streams[d]
        m = masks[d]
        r0 = c * c_
        lo = h * DN_HEAD_DIM
        lb = BETA_LANE + d * DN_HEADS + h
        lg = GDEC_LANE + d * DN_HEADS + h
        beta = p_ref[0, r0:r0 + c_, lb:lb + 1]
        gc = gcum[d][r0:r0 + c_, lg:lg + 1]
        gr = gcum_t[d][lg:lg + 1, r0:r0 + c_]
        glast = gcum[d][r0 + m["last"]:r0 + m["last"] + 1, lg:lg + 1]
        decay = jnp.exp(jnp.where(m["incl"], gc - gr, -jnp.inf))
        qh = q_ref[0, r0:r0 + c_, lo:lo + DN_HEAD_DIM]
        kh = k_ref[0, r0:r0 + c_, lo:lo + DN_HEAD_DIM]
        vh = v_ref[0, r0:r0 + c_, lo:lo + DN_HEAD_DIM]
        kb = kh * beta
        eg = jnp.exp(gc)
        pre[d, c, h] = dict(decay=decay, kq_lhs=jnp.concatenate([kb, qh], axis=0), kh=kh,
                            rhs=jnp.concatenate([vh * beta, kb * eg], axis=1), q_dec=qh * eg,
                            k_dec=kh * jnp.exp(glast - gc), e_last=jnp.exp(glast))
    kq = {i: _bdot_nt(pre[i]["kq_lhs"], pre[i]["kh"]) for i in inst}
    a = {i: jnp.where(masks[i[0]]["strict"], kq[i][:c_] * pre[i]["decay"], 0.0) for i in inst}
    attn = {i: kq[i][c_:] * pre[i]["decay"] for i in inst}
    t = {i: masks[i[0]]["eye"] - jnp.where(masks[i[0]]["levels"][0], a[i], 0.0) for i in inst}
    for lvl in range(1, len(masks[0]["levels"])):
        ta = {i: _bdot(t[i], jnp.where(masks[i[0]]["levels"][lvl], a[i], 0.0)) for i in inst}
        t = {i: t[i] - _bdot(ta[i], t[i]) for i in inst}
    uw = {i: _bdot(t[i], pre[i]["rhs"]) for i in inst}
    for step in range(nchunk):
        group = [(d, order[d][step], h) for d in range(2) for h in range(DN_HEADS)]
        st = {i: s_scr[i[0], i[2]] for i in group}
        ws = {i: _bdot(jnp.concatenate([uw[i][:, DN_HEAD_DIM:], pre[i]["q_dec"]], axis=0), st[i]) for i in group}
        v_new = {i: uw[i][:, :DN_HEAD_DIM] - ws[i][:c_] for i in group}
        for i in group:
            d, c, h = i
            streams[d][4][0, c * c_:(c + 1) * c_, h * DN_HEAD_DIM:(h + 1) * DN_HEAD_DIM] = (
                ws[i][c_:] + _bdot(attn[i], v_new[i]))
        for i in group:
            s_scr[i[0], i[2]] = st[i] * pre[i]["e_last"] + _bdot_tn(pre[i]["k_dec"], v_new[i])

    @pl.when(j == nj - 1)
    def _():
        st_ref[0] = s_scr[...]


def _dn_scan(q, k, v, p, s0, tb):
    b_, t_, _ = q.shape
    nb = t_ // tb
    fwd = lambda wd: pl.BlockSpec((1, tb, wd), lambda b, j: (b, j, 0))
    bwd = lambda wd: pl.BlockSpec((1, tb, wd), lambda b, j: (b, nb - 1 - j, 0))
    widths = (DN_WIDTH, DN_WIDTH, DN_WIDTH, SMALL_W)
    st_shape = (2, DN_HEADS, DN_HEAD_DIM, DN_HEAD_DIM)
    st_spec = pl.BlockSpec((1,) + st_shape, lambda b, j: (b, 0, 0, 0, 0))
    return pl.pallas_call(
        _dn_kernel,
        grid=(b_, nb),
        in_specs=[fwd(w) for w in widths] + [bwd(w) for w in widths] + [st_spec],
        out_specs=[fwd(DN_WIDTH), bwd(DN_WIDTH), st_spec],
        out_shape=[jax.ShapeDtypeStruct((b_, t_, DN_WIDTH), F32)] * 2 + [jax.ShapeDtypeStruct((b_,) + st_shape, F32)],
        scratch_shapes=[pltpu.VMEM(st_shape, F32)],
        compiler_params=_cparams(("parallel", "arbitrary")),
        name="dn",
    )(q, k, v, p, q, k, v, p, s0)


def _log_sigmoid(t):
    return jnp.minimum(t, 0.0) - jnp.log(1.0 + jnp.exp(-jnp.abs(t)))


def _gla_kernel(qkf_ref, vf_ref, pf_ref, qkb_ref, vb_ref, pb_ref, wg_ref, bg_ref, s0_ref, of_ref, ob_ref, st_ref,
                s_scr):
    col = pl.program_id(1)
    rb = pl.program_id(2)
    ncol = pl.num_programs(1)
    nrb = pl.num_programs(2)

    @pl.when((col == 0) & (rb == 0))
    def _():
        s_scr[...] = s0_ref[0]

    c_ = CHUNK
    nchunk = qkf_ref.shape[1] // c_
    ncb = pf_ref.shape[2] // SMALL_W
    nsub = c_ // SUB
    streams = ((qkf_ref, vf_ref, pf_ref, of_ref), (qkb_ref, vb_ref, pb_ref, ob_ref))
    masks, order = [], []
    for rev in (False, True):
        ri, ci, incl, _ = _order_masks(c_, rev)
        pos_r = (c_ - 1 - ri) if rev else ri
        pos_c = (c_ - 1 - ci) if rev else ci
        first_rows = [(c_ - 1 - SUB * i) if rev else SUB * i for i in range(nsub)]
        masks.append(dict(incl=incl, diag=incl & ((pos_r // SUB) == (pos_c // SUB)),
                          off=(pos_c // SUB) < (pos_r // SUB), last=0 if rev else c_ - 1, first_rows=first_rows,
                          rev=rev))
        cols = [ncb - 1 - i if rev else i for i in range(ncb)]
        chunks = [nchunk - 1 - cc if rev else cc for cc in range(nchunk)]
        order.append([(cb, c) for cb in cols for c in chunks])
    nstep = len(order[0])
    la_col = {}
    for d in range(2):
        for cb in range(ncb):
            gate = _dot_split(streams[d][2][0, :, cb * SMALL_W:(cb + 1) * SMALL_W], wg_ref[d]) + bg_ref[d]
            la_col[d, cb] = _log_sigmoid(gate) * (1.0 / GLA_GATE_NORMALIZER)
    pre = {}
    for step in range(nstep):
        for d in range(2):
            cb, c = order[d][step]
            qk_ref, v_ref, _, _ = streams[d]
            m = masks[d]
            r0 = c * c_
            la = la_col[d, cb][r0:r0 + c_, :]
            bc = _dot_exact_lhs(m["incl"], la, terms=2)
            bex = bc - la
            b_last = bc[m["last"]:m["last"] + 1, :]
            pieces = [jnp.broadcast_to(bex[fr:fr + 1, :], (SUB, GLA_K_WIDTH)) for fr in m["first_rows"]]
            refd = jnp.concatenate(pieces[::-1] if m["rev"] else pieces, axis=0)
            mid = m["first_rows"][nsub // 2]
            rmid = bex[mid:mid + 1, :]
            qk0 = cb * 2 * GLA_K_WIDTH
            qa = qk_ref[0, r0:r0 + c_, qk0:qk0 + GLA_K_WIDTH] * GLA_DK ** -0.5
            ka = qk_ref[0, r0:r0 + c_, qk0 + GLA_K_WIDTH:qk0 + 2 * GLA_K_WIDTH]
            q_in = qa * jnp.exp(bc)
            k_st = ka * jnp.exp(b_last - bc)
            q_d = qa * jnp.exp(bc - refd)
            k_d = ka * jnp.exp(jnp.minimum(refd - bc, EXP_CLAMP))
            q_o = qa * jnp.exp(jnp.minimum(bc - rmid, EXP_CLAMP))
            k_o = ka * jnp.exp(jnp.minimum(rmid - bc, EXP_CLAMP))
            e_last = jnp.exp(b_last)
            v0 = cb * GLA_V_WIDTH
            for h in range(GLA_HEADS):
                ks = slice(h * GLA_DK, (h + 1) * GLA_DK)
                vh = v_ref[0, r0:r0 + c_, v0 + h * GLA_DV:v0 + (h + 1) * GLA_DV]
                attn = (jnp.where(m["diag"], _bdot_nt(q_d[:, ks], k_d[:, ks]), 0.0)
                        + jnp.where(m["off"], _bdot_nt(q_o[:, ks], k_o[:, ks]), 0.0))
                pre[d, step, h] = dict(attn=attn, vh=vh, q_in=q_in[:, ks], e_last=e_last[:, ks],
                                       kv=_bdot_tn(vh, k_st[:, ks]))
    st = {(d, h): s_scr[d, h] for d in range(2) for h in range(GLA_HEADS)}
    for step in range(nstep):
        for d in range(2):
            cb, c = order[d][step]
            r0 = c * c_
            v0 = cb * GLA_V_WIDTH
            for h in range(GLA_HEADS):
                i = (d, step, h)
                streams[d][3][0, r0:r0 + c_, v0 + h * GLA_DV:v0 + (h + 1) * GLA_DV] = (
                    _bdot_nt(pre[i]["q_in"], st[d, h]) + _bdot(pre[i]["attn"], pre[i]["vh"]))
                st[d, h] = st[d, h] * pre[i]["e_last"] + pre[i]["kv"]
    for d in range(2):
        for h in range(GLA_HEADS):
            s_scr[d, h] = st[d, h]

    @pl.when((col == ncol - 1) & (rb == nrb - 1))
    def _():
        st_ref[0] = s_scr[...]


def _gla_scan(qk, v, p, wg, bg, s0, ncol, tr):
    b_, rows, _ = qk.shape
    nrb = rows // tr
    ncb = min(GLA_COLS_PER_STEP, ncol)
    ncblk = ncol // ncb
    assert ncb == 1 or nrb == 1
    fwd = lambda wd: pl.BlockSpec((1, tr, ncb * wd), lambda b, c, r: (b, r, c))
    bwd = lambda wd: pl.BlockSpec((1, tr, ncb * wd), lambda b, c, r: (b, nrb - 1 - r, ncblk - 1 - c))
    widths = (2 * GLA_K_WIDTH, GLA_V_WIDTH, SMALL_W)
    st_shape = (2, GLA_HEADS, GLA_DV, GLA_DK)
    st_spec = pl.BlockSpec((1,) + st_shape, lambda b, c, r: (b, 0, 0, 0, 0))
    return pl.pallas_call(
        _gla_kernel,
        grid=(b_, ncblk, nrb),
        in_specs=[fwd(w) for w in widths] + [bwd(w) for w in widths] + [
            pl.BlockSpec((2, SMALL_W, GLA_K_WIDTH), lambda b, c, r: (0, 0, 0)),
            pl.BlockSpec((2, 1, GLA_K_WIDTH), lambda b, c, r: (0, 0, 0)),
            st_spec],
        out_specs=[fwd(GLA_V_WIDTH), bwd(GLA_V_WIDTH), st_spec],
        out_shape=[jax.ShapeDtypeStruct((b_, rows, ncol * GLA_V_WIDTH), F32)] * 2 + [
            jax.ShapeDtypeStruct((b_,) + st_shape, F32)],
        scratch_shapes=[pltpu.VMEM(st_shape, F32)],
        compiler_params=_cparams(("parallel", "arbitrary", "arbitrary")),
        name="gla",
    )(qk, v, p, qk, v, p, wg, bg, s0)


def _head_norm_gate(o, z, gain):
    parts = []
    for h in range(o.shape[1] // LANES):
        hs = slice(h * LANES, (h + 1) * LANES)
        parts.append(_rms(o[:, hs], gain) * _silu(z[:, hs].astype(F32)))
    return jnp.concatenate(parts, axis=1)


def _mixout_kernel(*refs, with_router, ncol):
    (x_ref, odf_ref, odb_ref, dnz_ref, ogf_ref, ogb_ref, glz_ref, mod_ref, gd_ref, gg_ref, w_ref, gf_ref) = refs[:12]
    refs = list(refs[12:])
    col_scr = refs.pop() if ncol > 1 else None
    if with_router:
        rw_ref, xo_ref, h_ref, cmb_ref = refs
    else:
        xo_ref, h_ref = refs
    m_dn = _head_norm_gate(odf_ref[0] + odb_ref[0], dnz_ref[0], gd_ref[...])
    o_gl = ogf_ref[0] + ogb_ref[0]
    if ncol > 1:
        o_gl = _from_columns(o_gl, col_scr, ncol)
    m_gl = _head_norm_gate(o_gl, glz_ref[0], gg_ref[...])
    m = jnp.concatenate([m_dn, m_gl], axis=1).astype(BF16)
    x = x_ref[0] + mod_ref[0, 2:3, :] * jnp.dot(m, w_ref[...], preferred_element_type=F32)
    xo_ref[0] = x
    h = _rms(x, gf_ref[...]) * (1.0 + mod_ref[0, 4:5, :]) + mod_ref[0, 3:4, :]
    h_b = h.astype(BF16)
    h_ref[0] = h_b
    if with_router:
        rw_hi, rw_lo = _split_bf16(rw_ref[...], 2)
        logits = (jnp.dot(h_b, rw_hi, preferred_element_type=F32)
                  + jnp.dot(h_b, rw_lo, preferred_element_type=F32))
        lane = lax.broadcasted_iota(jnp.int32, logits.shape, 1).astype(F32)
        logits = jnp.where(lane < N_EXPERTS, logits, -jnp.inf)
        e = jnp.exp(logits - jnp.max(logits, axis=-1, keepdims=True))
        probs = e / jnp.sum(e, axis=-1, keepdims=True)
        p1 = jnp.max(probs, axis=-1, keepdims=True)
        i1 = jnp.min(jnp.where(probs == p1, lane, float(LANES)), axis=-1, keepdims=True)
        rest = jnp.where(lane == i1, -1.0, probs)
        p2 = jnp.max(rest, axis=-1, keepdims=True)
        i2 = jnp.min(jnp.where(rest == p2, lane, float(LANES)), axis=-1, keepdims=True)
        cmb_ref[0] = (jnp.where(lane == i1, p1, 0.0) + jnp.where(lane == i2, p2, 0.0)) / (p1 + p2)


def _mixout(x, odf, odb, dnz, ogf, ogb, glz, mods, mod_row, gd, gg, w_out, gf, router, tm, ncol):
    b_, t_, _ = x.shape
    with_router = router is not None
    tok = lambda wd: pl.BlockSpec((1, tm, wd), lambda b, j: (b, j, 0))
    colm = lambda wd: pl.BlockSpec((1, tm // ncol, ncol * wd), lambda b, j: (b, j, 0))
    const = lambda shp: pl.BlockSpec(shp, lambda b, j: (0,) * len(shp))
    in_specs = [tok(D_MODEL)] + [tok(DN_WIDTH)] * 3 + [colm(GLA_V_WIDTH)] * 2 + [tok(GLA_V_WIDTH)] + [
        pl.BlockSpec((1, SUBLANES, D_MODEL), lambda b, j: (mod_row(b), 0, 0)),
        const((1, LANES)), const((1, LANES)), const((D_MODEL, D_MODEL)), const((1, D_MODEL))]
    args = [x, odf, odb, dnz, ogf, ogb, glz, mods, gd.reshape(1, LANES), gg.reshape(1, LANES), w_out,
            gf.reshape(1, D_MODEL)]
    out_specs = [tok(D_MODEL), tok(D_MODEL)]
    out_shape = [jax.ShapeDtypeStruct((b_, t_, D_MODEL), F32), jax.ShapeDtypeStruct((b_, t_, D_MODEL), BF16)]
    if with_router:
        in_specs.append(const((D_MODEL, LANES)))
        args.append(router)
        out_specs.append(tok(LANES))
        out_shape.append(jax.ShapeDtypeStruct((b_, t_, LANES), F32))
    scratch = [_col_scratch(GLA_V_WIDTH // LANES, tm // ncol, ncol)] if ncol > 1 else []
    return pl.pallas_call(
        functools.partial(_mixout_kernel, with_router=with_router, ncol=ncol),
        grid=(b_, t_ // tm),
        in_specs=in_specs,
        out_specs=out_specs,
        out_shape=out_shape,
        scratch_shapes=scratch,
        compiler_params=_cparams(("parallel", "parallel")),
        name="mixout",
    )(*args)


def _ffn_kernel(x_ref, h_ref, mod_ref, wg_ref, wu_ref, wd_ref, o_ref, *, nsplit):
    h = h_ref[0]
    f = wg_ref.shape[1]
    fs = f // nsplit
    acc = None
    for s in range(nsplit):
        g = jnp.dot(h, wg_ref[:, s * fs:(s + 1) * fs], preferred_element_type=F32)
        u = jnp.dot(h, wu_ref[:, s * fs:(s + 1) * fs], preferred_element_type=F32)
        part = jnp.dot((_silu(g) * u).astype(BF16), wd_ref[s * fs:(s + 1) * fs, :], preferred_element_type=F32)
        acc = part if acc is None else acc + part
    o_ref[0] = x_ref[0] + mod_ref[0, 5:6, :] * acc


def _ffn(x, h, mods, mod_row, w_gu, w_down, tm):
    b_, t_, _ = x.shape
    f = w_down.shape[0]
    tok = lambda: pl.BlockSpec((1, tm, D_MODEL), lambda b, j: (b, j, 0))
    return pl.pallas_call(
        functools.partial(_ffn_kernel, nsplit=2),
        grid=(b_, t_ // tm),
        in_specs=[tok(), tok(),
                  pl.BlockSpec((1, SUBLANES, D_MODEL), lambda b, j: (mod_row(b), 0, 0)),
                  pl.BlockSpec((D_MODEL, f), lambda b, j: (0, 0)),
                  pl.BlockSpec((D_MODEL, f), lambda b, j: (0, 1)),
                  pl.BlockSpec((f, D_MODEL), lambda b, j: (0, 0))],
        out_specs=tok(),
        out_shape=jax.ShapeDtypeStruct((b_, t_, D_MODEL), F32),
        compiler_params=_cparams(("parallel", "parallel")),
        name="ffn",
    )(x, h, mods, w_gu, w_gu, w_down)


def _for_row_tiles(ntiles, fn):
    rows = MOE_GROUP * MOE_TILE
    single = ntiles == MOE_GROUP + 1
    keep = 1 - single.astype(jnp.int32)
    ngroup = (ntiles // MOE_GROUP) * keep
    rest = (ntiles % MOE_GROUP) * keep

    def body(i, carry):
        fn(pl.multiple_of(i * rows, rows), rows)
        return carry

    lax.fori_loop(0, ngroup, body, 0)
    for k in range(1, MOE_GROUP):
        @pl.when(rest == k)
        def _():
            fn(pl.multiple_of(ngroup * rows, rows), k * MOE_TILE)

    @pl.when(single)
    def _():
        fn(0, rows + MOE_TILE)


def _moe_kernel(cnt_ref, x_ref, h_ref, cmb_ref, mod_ref, wg_ref, wu_ref, wd_ref, fg_ref, o_ref,
                acc_ref, xe_ref, ye_ref, rkc_ref, rkr_ref, *, final_norm):
    b = pl.program_id(0)
    j = pl.program_id(1)
    e = pl.program_id(2)
    f = pl.program_id(3)
    nj = pl.num_programs(1)
    ne = pl.num_programs(2)
    nf = pl.num_programs(3)
    tm = h_ref.shape[1]
    ntiles = (cnt_ref[(b * nj + j) * ne + e] + MOE_TILE - 1) // MOE_TILE

    @pl.when((e == 0) & (f == 0))
    def _():
        acc_ref[...] = jnp.zeros_like(acc_ref)
        sb = 2 * LANES
        ri = lax.broadcasted_iota(jnp.int32, (sb, sb), 0)
        ci = lax.broadcasted_iota(jnp.int32, (sb, sb), 1)
        before = (ci < ri).astype(BF16)
        carry = jnp.zeros((1, LANES), F32)
        for s in range(tm // sb):
            sel = (cmb_ref[0, s * sb:(s + 1) * sb, :] > 0.0).astype(F32)
            rank = jnp.dot(before, sel.astype(BF16), preferred_element_type=F32) + carry
            rkc_ref[s * sb:(s + 1) * sb, :] = jnp.where(sel > 0.0, rank, -1.0)
            carry = carry + jnp.sum(sel, axis=0, keepdims=True)
        rkr_ref[...] = rkc_ref[...].T

    @pl.when(f == 0)
    def _():
        rank_row = rkr_ref[pl.ds(e, 1), :]

        def gather(r0, n):
            slot = (lax.broadcasted_iota(jnp.int32, (n, tm), 0) + r0).astype(F32)
            onehot = (rank_row == slot).astype(BF16)
            xe_ref[pl.ds(r0, n), :] = jnp.dot(onehot, h_ref[0], preferred_element_type=F32).astype(BF16)

        _for_row_tiles(ntiles, gather)

    def expert(r0, n):
        xr = xe_ref[pl.ds(r0, n), :]
        g = jnp.dot(xr, wg_ref[0], preferred_element_type=F32)
        u = jnp.dot(xr, wu_ref[0], preferred_element_type=F32)
        y = jnp.dot((_silu(g) * u).astype(BF16), wd_ref[0], preferred_element_type=F32)

        @pl.when(f == 0)
        def _():
            ye_ref[pl.ds(r0, n), :] = y

        @pl.when(f > 0)
        def _():
            ye_ref[pl.ds(r0, n), :] += y

    _for_row_tiles(ntiles, expert)

    @pl.when(f == nf - 1)
    def _():
        lane = lax.broadcasted_iota(jnp.int32, (tm, LANES), 1)
        rank_col = jnp.sum(jnp.where(lane == e, rkc_ref[...], 0.0), axis=-1, keepdims=True)
        ce = jnp.sum(jnp.where(lane == e, cmb_ref[0], 0.0), axis=-1, keepdims=True)

        def scatter(r0, n):
            slot = (lax.broadcasted_iota(jnp.int32, (tm, n), 1) + r0).astype(F32)
            onehot = (rank_col == slot).astype(BF16)
            back = jnp.dot(onehot, ye_ref[pl.ds(r0, n), :].astype(BF16), preferred_element_type=F32)
            acc_ref[...] += ce * back

        _for_row_tiles(ntiles, scatter)

    @pl.when((e == ne - 1) & (f == nf - 1))
    def _():
        y = x_ref[0] + mod_ref[0, 5:6, :] * acc_ref[...]
        if final_norm:
            y = _rms(y, fg_ref[...])
        o_ref[0] = y


def _moe(x, h, cmb, mods, mod_row, w_gu, w_down, final_gain, tm, tf):
    b_, t_, _ = x.shape
    ne, _, f2 = w_gu.shape
    fe = f2 // 2
    assert fe % tf == 0 and tm % (MOE_GROUP * MOE_TILE) == 0 and ne <= SUBLANES
    nf = fe // tf
    nj = t_ // tm
    counts = jnp.sum((cmb[..., :ne] > 0.0).reshape(b_, nj, tm, ne), axis=2, dtype=jnp.int32).reshape(-1)
    tok = lambda wd: pl.BlockSpec((1, tm, wd), lambda b, j, e, f, cnt: (b, j, 0))
    final_norm = final_gain is not None
    fg = (final_gain if final_norm else jnp.ones((D_MODEL,), F32)).reshape(1, D_MODEL)
    grid_spec = pltpu.PrefetchScalarGridSpec(
        num_scalar_prefetch=1,
        grid=(b_, nj, ne, nf),
        in_specs=[tok(D_MODEL), tok(D_MODEL), tok(LANES),
                  pl.BlockSpec((1, SUBLANES, D_MODEL), lambda b, j, e, f, cnt: (mod_row(b), 0, 0)),
                  pl.BlockSpec((1, D_MODEL, tf), lambda b, j, e, f, cnt: (e, 0, f)),
                  pl.BlockSpec((1, D_MODEL, tf), lambda b, j, e, f, cnt: (e, 0, nf + f)),
                  pl.BlockSpec((1, tf, D_MODEL), lambda b, j, e, f, cnt: (e, f, 0)),
                  pl.BlockSpec((1, D_MODEL), lambda b, j, e, f, cnt: (0, 0))],
        out_specs=tok(D_MODEL),
        scratch_shapes=[pltpu.VMEM((tm, D_MODEL), F32),
                        pltpu.VMEM((tm, D_MODEL), BF16),
                        pltpu.VMEM((tm, D_MODEL), F32),
                        pltpu.VMEM((tm, LANES), F32),
                        pltpu.VMEM((LANES, tm), F32)])
    return pl.pallas_call(
        functools.partial(_moe_kernel, final_norm=final_norm),
        grid_spec=grid_spec,
        out_shape=jax.ShapeDtypeStruct((b_, t_, D_MODEL), F32),
        compiler_params=_cparams(("parallel", "parallel", "arbitrary", "arbitrary")),
        name="moe",
    )(counts, x, h, cmb, mods, w_gu, w_gu, w_down, fg)


def _pack_w_in(w):
    offs = np.cumsum((0, QKV_W, DN_WIDTH, 2 * DN_HEADS, 2 * DN_HEADS, GLA_K_WIDTH, GLA_K_WIDTH, GLA_V_WIDTH,
                      GLA_V_WIDTH, 2 * GLA_GATE_RANK))
    seg = lambda i: w[:, offs[i]:offs[i + 1]]
    small = jnp.concatenate([seg(2), seg(3), seg(8)], axis=1)
    small = jnp.pad(small, ((0, 0), (0, SMALL_W - small.shape[1])))
    return jnp.concatenate([seg(0), seg(1), seg(4), seg(5), seg(6), seg(7), small], axis=1).astype(BF16)


def _pack_gate(w_gate2, b_gate):
    ws = []
    for d in range(2):
        lo = GLR_LANE + d * GLA_GATE_RANK
        ws.append(jnp.zeros((SMALL_W, GLA_K_WIDTH), F32).at[lo:lo + GLA_GATE_RANK].set(w_gate2[d]))
    return jnp.stack(ws), b_gate.reshape(2, 1, GLA_K_WIDTH)


def _pack_dn_params(a_log, dt_bias):
    neg_a = jnp.zeros((SMALL_W,), F32).at[GDEC_LANE:GLR_LANE].set(-jnp.exp(a_log.reshape(-1)))
    dtb = jnp.zeros((SMALL_W,), F32).at[GDEC_LANE:GLR_LANE].set(dt_bias.reshape(-1))
    return jnp.zeros((SUBLANES, SMALL_W), F32).at[0].set(neg_a).at[1].set(dtb)


def _pad_rows(a, rows):
    return jnp.pad(a, ((0, rows - a.shape[0]),) + ((0, 0),) * (a.ndim - 1))


def _mixers(x_seq, mods, mod_row, gain_mix, w_in_p, cw, ap, wgs, bgs, states, ncol, tm, tb):
    b_, t_, _ = x_seq.shape
    q, k, v, p, p_col, dnz, gqk, gv, gz = _proj(x_seq, mods, mod_row, gain_mix, w_in_p, cw, ap, tm, ncol)
    rows = t_ // ncol
    o_dn_f, o_dn_b, s_dn = _dn_scan(q, k, v, p, states[0], tb)
    o_gl_f, o_gl_b, s_gl = _gla_scan(gqk, gv, p_col, wgs, bgs, states[1], ncol, min(rows, 2 * CHUNK))
    outs = [(o_dn_f, o_gl_f), (o_dn_b, o_gl_b)]
    return outs, dnz, gz, (s_dn, s_gl)


def kernel(x, c, ctx, c_ctx, w_ada, b_ada, norm_mix, norm_ffn, w_in, conv_qkv, dn_a_log, dn_dt_bias, dn_norm,
           gla_w_gate2, gla_b_gate, gla_norm, w_out, ffn_w_gu, ffn_w_down, moe_router, moe_w_gu, moe_w_down,
           final_norm):
    b_, seq, _ = x.shape
    depth = w_ada.shape[0]
    ctx_len = ctx.shape[1]
    mod_rows = -(-(b_ + 1) // SUBLANES) * SUBLANES
    c_rows = _pad_rows(jnp.concatenate([c, c_ctx[None, :]], axis=0), mod_rows)
    lat_row = lambda b: b
    ctx_row = lambda b: b_
    tm_lat, tm_ctx = 512, ctx_len
    tb = 4 * CHUNK

    for i in range(depth):
        last = i == depth - 1
        mods = _ada(c_rows, w_ada[i], b_ada[i]).reshape(mod_rows, N_MOD, D_MODEL)
        mods = jnp.pad(mods, ((0, 0), (0, SUBLANES - N_MOD), (0, 0)))
        w_in_p = _pack_w_in(w_in[i])
        cw = _pad_rows(conv_qkv[i], SUBLANES)
        ap = _pack_dn_params(dn_a_log[i], dn_dt_bias[i])
        wgs, bgs = _pack_gate(gla_w_gate2[i], gla_b_gate[i])
        s0 = (jnp.zeros((b_, 2, DN_HEADS, DN_HEAD_DIM, DN_HEAD_DIM), F32),
              jnp.zeros((b_, 2, GLA_HEADS, GLA_DV, GLA_DK), F32))
        outs_c, dnz_c, gz_c, s_ctx = _mixers(ctx, mods, ctx_row, norm_mix[i], w_in_p, cw, ap, wgs, bgs, s0, 1,
                                             tm_ctx, tb)
        outs_l, dnz_l, gz_l, _ = _mixers(x, mods, lat_row, norm_mix[i], w_in_p, cw, ap, wgs, bgs, s_ctx, GRID_W,
                                         tm_lat, tb)
        w_out_b = w_out[i].astype(BF16)
        is_moe = i % 2 == 1
        if is_moe:
            router = jnp.pad(moe_router[i // 2], ((0, 0), (0, LANES - N_EXPERTS)))
            w_gu = moe_w_gu[i // 2].astype(BF16)
            w_dn = moe_w_down[i // 2].astype(BF16)
        else:
            router = None
            w_gu = ffn_w_gu[i // 2].astype(BF16)
            w_dn = ffn_w_down[i // 2].astype(BF16)

        def channel(tokens, outs, dnz, gz, row, tm, ncol, final_gain):
            res = _mixout(tokens, outs[0][0], outs[1][0], dnz, outs[0][1], outs[1][1], gz, mods, row, dn_norm[i],
                          gla_norm[i], w_out_b, norm_ffn[i], router, tm, ncol)
            if is_moe:
                x_mid, h, cmb = res
                tm_moe = min(1024, tokens.shape[1])
                return _moe(x_mid, h, cmb, mods, row, w_gu, w_dn, final_gain, tm_moe, D_EXPERT_TILE)
            x_mid, h = res
            y = _ffn(x_mid, h, mods, row, w_gu, w_dn, tm)
            if final_gain is not None:
                raise NotImplementedError("final norm is fused into the expert mixer only")
            return y

        x = channel(x, outs_l, dnz_l, gz_l, lat_row, tm_lat, GRID_W, final_norm if last else None)
        if not last:
            ctx = channel(ctx, outs_c, dnz_c, gz_c, ctx_row, tm_ctx, 1, None)
    return x
```

```python
import functools

import jax
import jax.numpy as jnp
import numpy as np
from jax import lax
from jax.experimental import pallas as pl
from jax.experimental.pallas import tpu as pltpu

F32 = jnp.float32
BF16 = jnp.bfloat16
HIGHEST = lax.Precision.HIGHEST

D_MODEL = 1024
GRID_W = 64
DN_HEADS = 4
DN_HEAD_DIM = 128
DN_WIDTH = DN_HEADS * DN_HEAD_DIM
CONV_W = 5
GLA_HEADS = 4
GLA_DK = 64
GLA_DV = 128
GLA_K_WIDTH = GLA_HEADS * GLA_DK
GLA_V_WIDTH = GLA_HEADS * GLA_DV
GLA_GATE_RANK = 16
GLA_GATE_NORMALIZER = 16.0
N_EXPERTS = 8
N_MOD = 6
NORM_EPS = 1e-6

LANES = 128
SUBLANES = 8
CHUNK = 64
SUB = 16
EXP_CLAMP = 80.0
D_EXPERT_TILE = 512
MOE_TILE = 64
MOE_GROUP = 4
GLA_COLS_PER_STEP = 8
VMEM_LIMIT = 56 * 1024 * 1024

QKV_W = 3 * DN_WIDTH
SMALL_W = LANES
PROJ_WIDTHS = (QKV_W, DN_WIDTH, 2 * GLA_K_WIDTH, GLA_V_WIDTH, GLA_V_WIDTH, SMALL_W)
PROJ_COLS = sum(PROJ_WIDTHS)
BETA_LANE = 0
GDEC_LANE = 2 * DN_HEADS
GLR_LANE = 4 * DN_HEADS


def _cparams(sem, vmem=VMEM_LIMIT):
    return pltpu.CompilerParams(dimension_semantics=sem, vmem_limit_bytes=vmem)


def _sigmoid(t):
    return 1.0 / (1.0 + jnp.exp(-t))


def _silu(t):
    return t * _sigmoid(t)


def _softplus(t):
    return jnp.maximum(t, 0.0) + jnp.log(1.0 + jnp.exp(-jnp.abs(t)))


def _bdot(a, b):
    return jnp.dot(a.astype(BF16), b.astype(BF16), preferred_element_type=F32)


def _bdot_nt(a, b):
    return lax.dot_general(a.astype(BF16), b.astype(BF16), (((1,), (1,)), ((), ())), preferred_element_type=F32)


def _bdot_tn(a, b):
    return lax.dot_general(a.astype(BF16), b.astype(BF16), (((0,), (0,)), ((), ())), preferred_element_type=F32)


def _split_bf16(t, terms):
    parts = []
    for _ in range(terms):
        part = t.astype(BF16)
        parts.append(part)
        t = t - part.astype(F32)
    return parts


def _dot_exact_lhs(a, x, terms=2):
    a = a.astype(BF16)
    out = None
    for part in _split_bf16(x, terms):
        d = jnp.dot(a, part, preferred_element_type=F32)
        out = d if out is None else out + d
    return out


def _dot_split(a, b):
    a_hi, a_lo = _split_bf16(a, 2)
    b_hi, b_lo = _split_bf16(b, 2)
    dot = lambda u, v: jnp.dot(u, v, preferred_element_type=F32)
    return dot(a_hi, b_hi) + (dot(a_lo, b_hi) + dot(a_hi, b_lo))


def _rms(t, gain):
    return t * lax.rsqrt(jnp.mean(t * t, axis=-1, keepdims=True) + NORM_EPS) * gain


def _ada_kernel(c_ref, w_ref, b_ref, o_ref):
    s = _silu(c_ref[...])
    o_ref[...] = jnp.dot(s, w_ref[...], precision=HIGHEST, preferred_element_type=F32) + b_ref[...]


def _ada(c_rows, w, b):
    rows = c_rows.shape[0]
    n = w.shape[1]
    tn = D_MODEL
    return pl.pallas_call(
        _ada_kernel,
        grid=(n // tn,),
        in_specs=[pl.BlockSpec((rows, D_MODEL), lambda j: (0, 0)),
                  pl.BlockSpec((D_MODEL, tn), lambda j: (0, j)),
                  pl.BlockSpec((1, tn), lambda j: (0, j))],
        out_specs=pl.BlockSpec((rows, tn), lambda j: (0, j)),
        out_shape=jax.ShapeDtypeStruct((rows, n), F32),
        compiler_params=_cparams(("arbitrary",)),
        name="ada",
    )(c_rows, w, b.reshape(1, n))


def _to_columns(val, out_ref, scr_ref, ncol):
    nrows, width = val.shape
    groups = width // LANES
    grid_rows = nrows // ncol
    pitch = _col_pitch(ncol)
    for g in range(groups):
        for r in range(grid_rows):
            scr_ref[g, r * pitch:r * pitch + ncol, :] = val[r * ncol:(r + 1) * ncol, g * LANES:(g + 1) * LANES]
    for c in range(ncol):
        for g in range(groups):
            lo = c * width + g * LANES
            out_ref[0, :, lo:lo + LANES] = scr_ref[g, pl.ds(c, grid_rows, stride=pitch), :]


def _from_columns(in_val, scr_ref, ncol):
    grid_rows = in_val.shape[0]
    width = in_val.shape[1] // ncol
    groups = width // LANES
    pitch = _col_pitch(ncol)
    for c in range(ncol):
        for g in range(groups):
            lo = c * width + g * LANES
            scr_ref[g, pl.ds(c, grid_rows, stride=pitch), :] = in_val[:, lo:lo + LANES]
    return jnp.concatenate(
        [jnp.concatenate([scr_ref[g, r * pitch:r * pitch + ncol, :] for r in range(grid_rows)], axis=0)
         for g in range(groups)], axis=1)


def _col_pitch(ncol):
    return ncol + SUBLANES // 2


def _col_scratch(groups, grid_rows, ncol):
    rows = -(-grid_rows * _col_pitch(ncol) // SUBLANES) * SUBLANES
    return pltpu.VMEM((groups, rows, LANES), F32)


def _proj_kernel(x_ref, xp_ref, xn_ref, mod_ref, gain_ref, w_ref, cw_ref, ap_ref,
                 q_ref, k_ref, v_ref, p_ref, dnz_ref, gqk_ref, gv_ref, gz_ref, *rest, ncol):
    if ncol > 1:
        pcol_ref, ext_ref, col_scr = rest
    else:
        (ext_ref,) = rest
    j = pl.program_id(1)
    nj = pl.num_programs(1)
    tm = x_ref.shape[1]
    halo = SUBLANES
    pad = CONV_W // 2
    norm_mod = lambda t: _rms(t, gain_ref[...]) * (1.0 + mod_ref[0, 1:2, :]) + mod_ref[0, 0:1, :]
    h = norm_mod(x_ref[0])
    h_all = jnp.concatenate([h, norm_mod(xp_ref[0]), norm_mod(xn_ref[0])], axis=0).astype(BF16)
    r_qkv = jnp.dot(h_all, w_ref[:, 0:QKV_W], preferred_element_type=F32)
    r = jnp.dot(h.astype(BF16), w_ref[:, QKV_W:], preferred_element_type=F32)
    o = 0
    for ref, wd in zip((dnz_ref, gqk_ref, gv_ref, gz_ref), PROJ_WIDTHS[1:5]):
        if ncol > 1 and (ref is gqk_ref or ref is gv_ref):
            _to_columns(r[:, o:o + wd], ref, col_scr, ncol)
        else:
            ref[0] = r[:, o:o + wd].astype(ref.dtype)
        o += wd
    sm = r[:, o:o + SMALL_W]

    ext_ref[0:halo, :] = jnp.where(j > 0, r_qkv[tm:tm + halo], 0.0)
    ext_ref[halo:halo + tm, :] = r_qkv[0:tm]
    ext_ref[halo + tm:2 * halo + tm, :] = jnp.where(j < nj - 1, r_qkv[tm + halo:], 0.0)
    acc = None
    for w in range(CONV_W):
        term = ext_ref[pl.ds(halo - pad + w, tm), :] * cw_ref[w:w + 1, :]
        acc = term if acc is None else acc + term
    s = _silu(acc)
    for hd in range(DN_HEADS):
        lo = hd * DN_HEAD_DIM
        qh = s[:, lo:lo + DN_HEAD_DIM]
        kh = s[:, DN_WIDTH + lo:DN_WIDTH + lo + DN_HEAD_DIM]
        q_ref[0, :, lo:lo + DN_HEAD_DIM] = (qh * lax.rsqrt(jnp.sum(qh * qh, axis=-1, keepdims=True) + NORM_EPS)
                                            * DN_HEAD_DIM ** -0.5)
        k_ref[0, :, lo:lo + DN_HEAD_DIM] = kh * lax.rsqrt(jnp.sum(kh * kh, axis=-1, keepdims=True) + NORM_EPS)
    v_ref[0] = s[:, 2 * DN_WIDTH:]
    lane = lax.broadcasted_iota(jnp.int32, sm.shape, 1)
    beta = _sigmoid(sm)
    gdec = ap_ref[0:1, :] * _softplus(sm + ap_ref[1:2, :])
    p = jnp.where(lane < GDEC_LANE, beta, jnp.where(lane < GLR_LANE, gdec, sm))
    p_ref[0] = p
    if ncol > 1:
        _to_columns(p, pcol_ref, col_scr, ncol)


def _proj(x, mods, mod_row, gain, w, cw, ap, tm, ncol):
    b_, t_, _ = x.shape
    r8 = tm // SUBLANES
    last8 = t_ // SUBLANES - 1
    tok = lambda wd: pl.BlockSpec((1, tm, wd), lambda b, j: (b, j, 0))
    colm = lambda wd: pl.BlockSpec((1, tm // ncol, ncol * wd), lambda b, j: (b, j, 0))
    const = lambda shp: pl.BlockSpec(shp, lambda b, j: (0,) * len(shp))
    out_specs = [tok(DN_WIDTH), tok(DN_WIDTH), tok(DN_WIDTH), tok(SMALL_W), tok(DN_WIDTH),
                 colm(2 * GLA_K_WIDTH), colm(GLA_V_WIDTH), tok(GLA_V_WIDTH)]
    tshape = lambda wd, dt=F32: jax.ShapeDtypeStruct((b_, t_, wd), dt)
    cshape = lambda wd: jax.ShapeDtypeStruct((b_, t_ // ncol, ncol * wd), F32)
    out_shape = [tshape(DN_WIDTH)] * 3 + [tshape(SMALL_W), tshape(DN_WIDTH, BF16), cshape(2 * GLA_K_WIDTH),
                                          cshape(GLA_V_WIDTH), tshape(GLA_V_WIDTH, BF16)]
    scratch = [pltpu.VMEM((tm + 2 * SUBLANES, QKV_W), F32)]
    if ncol > 1:
        out_specs.append(colm(SMALL_W))
        out_shape.append(cshape(SMALL_W))
        scratch.append(_col_scratch(GLA_V_WIDTH // LANES, tm // ncol, ncol))
    outs = pl.pallas_call(
        functools.partial(_proj_kernel, ncol=ncol),
        grid=(b_, t_ // tm),
        in_specs=[tok(D_MODEL),
                  pl.BlockSpec((1, SUBLANES, D_MODEL), lambda b, j: (b, jnp.maximum(j * r8 - 1, 0), 0)),
                  pl.BlockSpec((1, SUBLANES, D_MODEL), lambda b, j: (b, jnp.minimum((j + 1) * r8, last8), 0)),
                  pl.BlockSpec((1, SUBLANES, D_MODEL), lambda b, j: (mod_row(b), 0, 0)),
                  const((1, D_MODEL)), const((D_MODEL, PROJ_COLS)), const((SUBLANES, QKV_W)),
                  const((SUBLANES, SMALL_W))],
        out_specs=out_specs,
        out_shape=out_shape,
        scratch_shapes=scratch,
        compiler_params=_cparams(("parallel", "parallel")),
        name="proj",
    )(x, x, x, mods, gain.reshape(1, D_MODEL), w, cw, ap)
    q, k, v, p, dnz, gqk, gv, gz = outs[:8]
    p_col = outs[8] if ncol > 1 else p
    return q, k, v, p, p_col, dnz, gqk, gv, gz


def _order_masks(n, rev):
    ri = lax.broadcasted_iota(jnp.int32, (n, n), 0)
    ci = lax.broadcasted_iota(jnp.int32, (n, n), 1)
    if rev:
        return ri, ci, ci >= ri, ci > ri
    return ri, ci, ci <= ri, ci < ri


def _dn_kernel(qf_ref, kf_ref, vf_ref, pf_ref, qb_ref, kb_ref, vb_ref, pb_ref, s0_ref, of_ref, ob_ref, st_ref, s_scr):
    j = pl.program_id(1)
    nj = pl.num_programs(1)

    @pl.when(j == 0)
    def _():
        s_scr[...] = s0_ref[0]

    c_ = CHUNK
    tb = qf_ref.shape[1]
    nchunk = tb // c_
    streams = ((qf_ref, kf_ref, vf_ref, pf_ref, of_ref), (qb_ref, kb_ref, vb_ref, pb_ref, ob_ref))
    masks, gcum, gcum_t, order = [], [], [], []
    for d, rev in enumerate((False, True)):
        ri, ci, incl, strict = _order_masks(c_, rev)
        levels = []
        s = 1
        while s < c_:
            levels.append(strict & ((ri // (2 * s)) == (ci // (2 * s))) & ((ri // s) != (ci // s)))
            s *= 2
        masks.append(dict(incl=incl, strict=strict, eye=(ri == ci).astype(F32), levels=levels,
                          last=0 if rev else c_ - 1))
        rb, cb, incl_b, _ = _order_masks(tb, rev)
        cum_mat = incl_b & ((rb // c_) == (cb // c_))
        gcum.append(_dot_exact_lhs(cum_mat, streams[d][3][0], terms=3))
        gcum_t.append(gcum[d].T)
        order.append([nchunk - 1 - cc if rev else cc for cc in range(nchunk)])

    inst = [(d, c, h) for step in range(nchunk) for d in range(2) for c in (order[d][step],)
            for h in range(DN_HEADS)]
    pre = {}
    for d, c, h in inst:
        q_ref, k_ref, v_ref, p_ref, _ = streams[d]
        m = masks[d]
        r0 = c * c_
        lo = h * DN_HEAD_DIM
        lb = BETA_LANE + d * DN_HEADS + h
        lg = GDEC_LANE + d * DN_HEADS + h
        beta = p_ref[0, r0:r0 + c_, lb:lb + 1]
        gc = gcum[d][r0:r0 + c_, lg:lg + 1]
        gr = gcum_t[d][lg:lg + 1, r0:r0 + c_]
        glast = gcum[d][r0 + m["last"]:r0 + m["last"] + 1, lg:lg + 1]
        decay = jnp.exp(jnp.where(m["incl"], gc - gr, -jnp.inf))
        qh = q_ref[0, r0:r0 + c_, lo:lo + DN_HEAD_DIM]
        kh = k_ref[0, r0:r0 + c_, lo:lo + DN_HEAD_DIM]
        vh = v_ref[0, r0:r0 + c_, lo:lo + DN_HEAD_DIM]
        kb = kh * beta
        eg = jnp.exp(gc)
        pre[d, c, h] = dict(decay=decay, kq_lhs=jnp.concatenate([kb, qh], axis=0), kh=kh,
                            rhs=jnp.concatenate([vh * beta, kb * eg], axis=1), q_dec=qh * eg,
                            k_dec=kh * jnp.exp(glast - gc), e_last=jnp.exp(glast))
    kq = {i: _bdot_nt(pre[i]["kq_lhs"], pre[i]["kh"]) for i in inst}
    a = {i: jnp.where(masks[i[0]]["strict"], kq[i][:c_] * pre[i]["decay"], 0.0) for i in inst}
    attn = {i: kq[i][c_:] * pre[i]["decay"] for i in inst}
    t = {i: masks[i[0]]["eye"] - jnp.where(masks[i[0]]["levels"][0], a[i], 0.0) for i in inst}
    for lvl in range(1, len(masks[0]["levels"])):
        ta = {i: _bdot(t[i], jnp.where(masks[i[0]]["levels"][lvl], a[i], 0.0)) for i in inst}
        t = {i: t[i] - _bdot(ta[i], t[i]) for i in inst}
    uw = {i: _bdot(t[i], pre[i]["rhs"]) for i in inst}
    for step in range(nchunk):
        group = [(d, order[d][step], h) for d in range(2) for h in range(DN_HEADS)]
        st = {i: s_scr[i[0], i[2]] for i in group}
        ws = {i: _bdot(jnp.concatenate([uw[i][:, DN_HEAD_DIM:], pre[i]["q_dec"]], axis=0), st[i]) for i in group}
        v_new = {i: uw[i][:, :DN_HEAD_DIM] - ws[i][:c_] for i in group}
        for i in group:
            d, c, h = i
            streams[d][4][0, c * c_:(c + 1) * c_, h * DN_HEAD_DIM:(h + 1) * DN_HEAD_DIM] = (
                ws[i][c_:] + _bdot(attn[i], v_new[i]))
        for i in group:
            s_scr[i[0], i[2]] = st[i] * pre[i]["e_last"] + _bdot_tn(pre[i]["k_dec"], v_new[i])

    @pl.when(j == nj - 1)
    def _():
        st_ref[0] = s_scr[...]


def _dn_scan(q, k, v, p, s0, tb):
    b_, t_, _ = q.shape
    nb = t_ // tb
    fwd = lambda wd: pl.BlockSpec((1, tb, wd), lambda b, j: (b, j, 0))
    bwd = lambda wd: pl.BlockSpec((1, tb, wd), lambda b, j: (b, nb - 1 - j, 0))
    widths = (DN_WIDTH, DN_WIDTH, DN_WIDTH, SMALL_W)
    st_shape = (2, DN_HEADS, DN_HEAD_DIM, DN_HEAD_DIM)
    st_spec = pl.BlockSpec((1,) + st_shape, lambda b, j: (b, 0, 0, 0, 0))
    return pl.pallas_call(
        _dn_kernel,
        grid=(b_, nb),
        in_specs=[fwd(w) for w in widths] + [bwd(w) for w in widths] + [st_spec],
        out_specs=[fwd(DN_WIDTH), bwd(DN_WIDTH), st_spec],
        out_shape=[jax.ShapeDtypeStruct((b_, t_, DN_WIDTH), F32)] * 2 + [jax.ShapeDtypeStruct((b_,) + st_shape, F32)],
        scratch_shapes=[pltpu.VMEM(st_shape, F32)],
        compiler_params=_cparams(("parallel", "arbitrary")),
        name="dn",
    )(q, k, v, p, q, k, v, p, s0)


def _log_sigmoid(t):
    return jnp.minimum(t, 0.0) - jnp.log(1.0 + jnp.exp(-jnp.abs(t)))


def _gla_kernel(qkf_ref, vf_ref, pf_ref, qkb_ref, vb_ref, pb_ref, wg_ref, bg_ref, s0_ref, of_ref, ob_ref, st_ref,
                s_scr):
    col = pl.program_id(1)
    rb = pl.program_id(2)
    ncol = pl.num_programs(1)
    nrb = pl.num_programs(2)

    @pl.when((col == 0) & (rb == 0))
    def _():
        s_scr[...] = s0_ref[0]

    c_ = CHUNK
    nchunk = qkf_ref.shape[1] // c_
    ncb = pf_ref.shape[2] // SMALL_W
    nsub = c_ // SUB
    streams = ((qkf_ref, vf_ref, pf_ref, of_ref), (qkb_ref, vb_ref, pb_ref, ob_ref))
    masks, order = [], []
    for rev in (False, True):
        ri, ci, incl, _ = _order_masks(c_, rev)
        pos_r = (c_ - 1 - ri) if rev else ri
        pos_c = (c_ - 1 - ci) if rev else ci
        first_rows = [(c_ - 1 - SUB * i) if rev else SUB * i for i in range(nsub)]
        masks.append(dict(incl=incl, diag=incl & ((pos_r // SUB) == (pos_c // SUB)),
                          off=(pos_c // SUB) < (pos_r // SUB), last=0 if rev else c_ - 1, first_rows=first_rows,
                          rev=rev))
        cols = [ncb - 1 - i if rev else i for i in range(ncb)]
        chunks = [nchunk - 1 - cc if rev else cc for cc in range(nchunk)]
        order.append([(cb, c) for cb in cols for c in chunks])
    nstep = len(order[0])
    la_col = {}
    for d in range(2):
        for cb in range(ncb):
            gate = _dot_split(streams[d][2][0, :, cb * SMALL_W:(cb + 1) * SMALL_W], wg_ref[d]) + bg_ref[d]
            la_col[d, cb] = _log_sigmoid(gate) * (1.0 / GLA_GATE_NORMALIZER)
    pre = {}
    for step in range(nstep):
        for d in range(2):
            cb, c = order[d][step]
            qk_ref, v_ref, _, _ = streams[d]
            m = masks[d]
            r0 = c * c_
            la = la_col[d, cb][r0:r0 + c_, :]
            bc = _dot_exact_lhs(m["incl"], la, terms=2)
            bex = bc - la
            b_last = bc[m["last"]:m["last"] + 1, :]
            pieces = [jnp.broadcast_to(bex[fr:fr + 1, :], (SUB, GLA_K_WIDTH)) for fr in m["first_rows"]]
            refd = jnp.concatenate(pieces[::-1] if m["rev"] else pieces, axis=0)
            mid = m["first_rows"][nsub // 2]
            rmid = bex[mid:mid + 1, :]
            qk0 = cb * 2 * GLA_K_WIDTH
            qa = qk_ref[0, r0:r0 + c_, qk0:qk0 + GLA_K_WIDTH] * GLA_DK ** -0.5
            ka = qk_ref[0, r0:r0 + c_, qk0 + GLA_K_WIDTH:qk0 + 2 * GLA_K_WIDTH]
            q_in = qa * jnp.exp(bc)
            k_st = ka * jnp.exp(b_last - bc)
            q_d = qa * jnp.exp(bc - refd)
            k_d = ka * jnp.exp(jnp.minimum(refd - bc, EXP_CLAMP))
            q_o = qa * jnp.exp(jnp.minimum(bc - rmid, EXP_CLAMP))
            k_o = ka * jnp.exp(jnp.minimum(rmid - bc, EXP_CLAMP))
            e_last = jnp.exp(b_last)
            v0 = cb * GLA_V_WIDTH
            for h in range(GLA_HEADS):
                ks = slice(h * GLA_DK, (h + 1) * GLA_DK)
                vh = v_ref[0, r0:r0 + c_, v0 + h * GLA_DV:v0 + (h + 1) * GLA_DV]
                attn = (jnp.where(m["diag"], _bdot_nt(q_d[:, ks], k_d[:, ks]), 0.0)
                        + jnp.where(m["off"], _bdot_nt(q_o[:, ks], k_o[:, ks]), 0.0))
                pre[d, step, h] = dict(attn=attn, vh=vh, q_in=q_in[:, ks], e_last=e_last[:, ks],
                                       kv=_bdot_tn(vh, k_st[:, ks]))
    st = {(d, h): s_scr[d, h] for d in range(2) for h in range(GLA_HEADS)}
    for step in range(nstep):
        for d in range(2):
            cb, c = order[d][step]
            r0 = c * c_
            v0 = cb * GLA_V_WIDTH
            for h in range(GLA_HEADS):
                i = (d, step, h)
                streams[d][3][0, r0:r0 + c_, v0 + h * GLA_DV:v0 + (h + 1) * GLA_DV] = (
                    _bdot_nt(pre[i]["q_in"], st[d, h]) + _bdot(pre[i]["attn"], pre[i]["vh"]))
                st[d, h] = st[d, h] * pre[i]["e_last"] + pre[i]["kv"]
    for d in range(2):
        for h in range(GLA_HEADS):
            s_scr[d, h] = st[d, h]

    @pl.when((col == ncol - 1) & (rb == nrb - 1))
    def _():
        st_ref[0] = s_scr[...]


def _gla_scan(qk, v, p, wg, bg, s0, ncol, tr):
    b_, rows, _ = qk.shape
    nrb = rows // tr
    ncb = min(GLA_COLS_PER_STEP, ncol)
    ncblk = ncol // ncb
    assert ncb == 1 or nrb == 1
    fwd = lambda wd: pl.BlockSpec((1, tr, ncb * wd), lambda b, c, r: (b, r, c))
    bwd = lambda wd: pl.BlockSpec((1, tr, ncb * wd), lambda b, c, r: (b, nrb - 1 - r, ncblk - 1 - c))
    widths = (2 * GLA_K_WIDTH, GLA_V_WIDTH, SMALL_W)
    st_shape = (2, GLA_HEADS, GLA_DV, GLA_DK)
    st_spec = pl.BlockSpec((1,) + st_shape, lambda b, c, r: (b, 0, 0, 0, 0))
    return pl.pallas_call(
        _gla_kernel,
        grid=(b_, ncblk, nrb),
        in_specs=[fwd(w) for w in widths] + [bwd(w) for w in widths] + [
            pl.BlockSpec((2, SMALL_W, GLA_K_WIDTH), lambda b, c, r: (0, 0, 0)),
            pl.BlockSpec((2, 1, GLA_K_WIDTH), lambda b, c, r: (0, 0, 0)),
            st_spec],
        out_specs=[fwd(GLA_V_WIDTH), bwd(GLA_V_WIDTH), st_spec],
        out_shape=[jax.ShapeDtypeStruct((b_, rows, ncol * GLA_V_WIDTH), F32)] * 2 + [
            jax.ShapeDtypeStruct((b_,) + st_shape, F32)],
        scratch_shapes=[pltpu.VMEM(st_shape, F32)],
        compiler_params=_cparams(("parallel", "arbitrary", "arbitrary")),
        name="gla",
    )(qk, v, p, qk, v, p, wg, bg, s0)


def _head_norm_gate(o, z, gain):
    parts = []
    for h in range(o.shape[1] // LANES):
        hs = slice(h * LANES, (h + 1) * LANES)
        parts.append(_rms(o[:, hs], gain) * _silu(z[:, hs].astype(F32)))
    return jnp.concatenate(parts, axis=1)


def _mixout_kernel(*refs, with_router, ncol):
    (x_ref, odf_ref, odb_ref, dnz_ref, ogf_ref, ogb_ref, glz_ref, mod_ref, gd_ref, gg_ref, w_ref, gf_ref) = refs[:12]
    refs = list(refs[12:])
    col_scr = refs.pop() if ncol > 1 else None
    if with_router:
        rw_ref, xo_ref, h_ref, cmb_ref = refs
    else:
        xo_ref, h_ref = refs
    m_dn = _head_norm_gate(odf_ref[0] + odb_ref[0], dnz_ref[0], gd_ref[...])
    o_gl = ogf_ref[0] + ogb_ref[0]
    if ncol > 1:
        o_gl = _from_columns(o_gl, col_scr, ncol)
    m_gl = _head_norm_gate(o_gl, glz_ref[0], gg_ref[...])
    m = jnp.concatenate([m_dn, m_gl], axis=1).astype(BF16)
    x = x_ref[0] + mod_ref[0, 2:3, :] * jnp.dot(m, w_ref[...], preferred_element_type=F32)
    xo_ref[0] = x
    h = _rms(x, gf_ref[...]) * (1.0 + mod_ref[0, 4:5, :]) + mod_ref[0, 3:4, :]
    h_b = h.astype(BF16)
    h_ref[0] = h_b
    if with_router:
        rw_hi, rw_lo = _split_bf16(rw_ref[...], 2)
        logits = (jnp.dot(h_b, rw_hi, preferred_element_type=F32)
                  + jnp.dot(h_b, rw_lo, preferred_element_type=F32))
        lane = lax.broadcasted_iota(jnp.int32, logits.shape, 1).astype(F32)
        logits = jnp.where(lane < N_EXPERTS, logits, -jnp.inf)
        e = jnp.exp(logits - jnp.max(logits, axis=-1, keepdims=True))
        probs = e / jnp.sum(e, axis=-1, keepdims=True)
        p1 = jnp.max(probs, axis=-1, keepdims=True)
        i1 = jnp.min(jnp.where(probs == p1, lane, float(LANES)), axis=-1, keepdims=True)
        rest = jnp.where(lane == i1, -1.0, probs)
        p2 = jnp.max(rest, axis=-1, keepdims=True)
        i2 = jnp.min(jnp.where(rest == p2, lane, float(LANES)), axis=-1, keepdims=True)
        cmb_ref[0] = (jnp.where(lane == i1, p1, 0.0) + jnp.where(lane == i2, p2, 0.0)) / (p1 + p2)


def _mixout(x, odf, odb, dnz, ogf, ogb, glz, mods, mod_row, gd, gg, w_out, gf, router, tm, ncol):
    b_, t_, _ = x.shape
    with_router = router is not None
    tok = lambda wd: pl.BlockSpec((1, tm, wd), lambda b, j: (b, j, 0))
    colm = lambda wd: pl.BlockSpec((1, tm // ncol, ncol * wd), lambda b, j: (b, j, 0))
    const = lambda shp: pl.BlockSpec(shp, lambda b, j: (0,) * len(shp))
    in_specs = [tok(D_MODEL)] + [tok(DN_WIDTH)] * 3 + [colm(GLA_V_WIDTH)] * 2 + [tok(GLA_V_WIDTH)] + [
        pl.BlockSpec((1, SUBLANES, D_MODEL), lambda b, j: (mod_row(b), 0, 0)),
        const((1, LANES)), const((1, LANES)), const((D_MODEL, D_MODEL)), const((1, D_MODEL))]
    args = [x, odf, odb, dnz, ogf, ogb, glz, mods, gd.reshape(1, LANES), gg.reshape(1, LANES), w_out,
            gf.reshape(1, D_MODEL)]
    out_specs = [tok(D_MODEL), tok(D_MODEL)]
    out_shape = [jax.ShapeDtypeStruct((b_, t_, D_MODEL), F32), jax.ShapeDtypeStruct((b_, t_, D_MODEL), BF16)]
    if with_router:
        in_specs.append(const((D_MODEL, LANES)))
        args.append(router)
        out_specs.append(tok(LANES))
        out_shape.append(jax.ShapeDtypeStruct((b_, t_, LANES), F32))
    scratch = [_col_scratch(GLA_V_WIDTH // LANES, tm // ncol, ncol)] if ncol > 1 else []
    return pl.pallas_call(
        functools.partial(_mixout_kernel, with_router=with_router, ncol=ncol),
        grid=(b_, t_ // tm),
        in_specs=in_specs,
        out_specs=out_specs,
        out_shape=out_shape,
        scratch_shapes=scratch,
        compiler_params=_cparams(("parallel", "parallel")),
        name="mixout",
    )(*args)


def _ffn_kernel(x_ref, h_ref, mod_ref, wg_ref, wu_ref, wd_ref, o_ref, *, nsplit):
    h = h_ref[0]
    f = wg_ref.shape[1]
    fs = f // nsplit
    acc = None
    for s in range(nsplit):
        g = jnp.dot(h, wg_ref[:, s * fs:(s + 1) * fs], preferred_element_type=F32)
        u = jnp.dot(h, wu_ref[:, s * fs:(s + 1) * fs], preferred_element_type=F32)
        part = jnp.dot((_silu(g) * u).astype(BF16), wd_ref[s * fs:(s + 1) * fs, :], preferred_element_type=F32)
        acc = part if acc is None else acc + part
    o_ref[0] = x_ref[0] + mod_ref[0, 5:6, :] * acc


def _ffn(x, h, mods, mod_row, w_gu, w_down, tm):
    b_, t_, _ = x.shape
    f = w_down.shape[0]
    tok = lambda: pl.BlockSpec((1, tm, D_MODEL), lambda b, j: (b, j, 0))
    return pl.pallas_call(
        functools.partial(_ffn_kernel, nsplit=2),
        grid=(b_, t_ // tm),
        in_specs=[tok(), tok(),
                  pl.BlockSpec((1, SUBLANES, D_MODEL), lambda b, j: (mod_row(b), 0, 0)),
                  pl.BlockSpec((D_MODEL, f), lambda b, j: (0, 0)),
                  pl.BlockSpec((D_MODEL, f), lambda b, j: (0, 1)),
                  pl.BlockSpec((f, D_MODEL), lambda b, j: (0, 0))],
        out_specs=tok(),
        out_shape=jax.ShapeDtypeStruct((b_, t_, D_MODEL), F32),
        compiler_params=_cparams(("parallel", "parallel")),
        name="ffn",
    )(x, h, mods, w_gu, w_gu, w_down)


def _for_row_tiles(ntiles, fn):
    rows = MOE_GROUP * MOE_TILE
    single = ntiles == MOE_GROUP + 1
    keep = 1 - single.astype(jnp.int32)
    ngroup = (ntiles // MOE_GROUP) * keep
    rest = (ntiles % MOE_GROUP) * keep

    def body(i, carry):
        fn(pl.multiple_of(i * rows, rows), rows)
        return carry

    lax.fori_loop(0, ngroup, body, 0)
    for k in range(1, MOE_GROUP):
        @pl.when(rest == k)
        def _():
            fn(pl.multiple_of(ngroup * rows, rows), k * MOE_TILE)

    @pl.when(single)
    def _():
        fn(0, rows + MOE_TILE)


def _moe_kernel(cnt_ref, x_ref, h_ref, cmb_ref, mod_ref, wg_ref, wu_ref, wd_ref, fg_ref, o_ref,
                acc_ref, xe_ref, ye_ref, rkc_ref, rkr_ref, *, final_norm):
    b = pl.program_id(0)
    j = pl.program_id(1)
    e = pl.program_id(2)
    f = pl.program_id(3)
    nj = pl.num_programs(1)
    ne = pl.num_programs(2)
    nf = pl.num_programs(3)
    tm = h_ref.shape[1]
    ntiles = (cnt_ref[(b * nj + j) * ne + e] + MOE_TILE - 1) // MOE_TILE

    @pl.when((e == 0) & (f == 0))
    def _():
        acc_ref[...] = jnp.zeros_like(acc_ref)
        sb = 2 * LANES
        ri = lax.broadcasted_iota(jnp.int32, (sb, sb), 0)
        ci = lax.broadcasted_iota(jnp.int32, (sb, sb), 1)
        before = (ci < ri).astype(BF16)
        carry = jnp.zeros((1, LANES), F32)
        for s in range(tm // sb):
            sel = (cmb_ref[0, s * sb:(s + 1) * sb, :] > 0.0).astype(F32)
            rank = jnp.dot(before, sel.astype(BF16), preferred_element_type=F32) + carry
            rkc_ref[s * sb:(s + 1) * sb, :] = jnp.where(sel > 0.0, rank, -1.0)
            carry = carry + jnp.sum(sel, axis=0, keepdims=True)
        rkr_ref[...] = rkc_ref[...].T

    @pl.when(f == 0)
    def _():
        rank_row = rkr_ref[pl.ds(e, 1), :]

        def gather(r0, n):
            slot = (lax.broadcasted_iota(jnp.int32, (n, tm), 0) + r0).astype(F32)
            onehot = (rank_row == slot).astype(BF16)
            xe_ref[pl.ds(r0, n), :] = jnp.dot(onehot, h_ref[0], preferred_element_type=F32).astype(BF16)

        _for_row_tiles(ntiles, gather)

    def expert(r0, n):
        xr = xe_ref[pl.ds(r0, n), :]
        g = jnp.dot(xr, wg_ref[0], preferred_element_type=F32)
        u = jnp.dot(xr, wu_ref[0], preferred_element_type=F32)
        y = jnp.dot((_silu(g) * u).astype(BF16), wd_ref[0], preferred_element_type=F32)

        @pl.when(f == 0)
        def _():
            ye_ref[pl.ds(r0, n), :] = y

        @pl.when(f > 0)
        def _():
            ye_ref[pl.ds(r0, n), :] += y

    _for_row_tiles(ntiles, expert)

    @pl.when(f == nf - 1)
    def _():
        lane = lax.broadcasted_iota(jnp.int32, (tm, LANES), 1)
        rank_col = jnp.sum(jnp.where(lane == e, rkc_ref[...], 0.0), axis=-1, keepdims=True)
        ce = jnp.sum(jnp.where(lane == e, cmb_ref[0], 0.0), axis=-1, keepdims=True)

        def scatter(r0, n):
            slot = (lax.broadcasted_iota(jnp.int32, (tm, n), 1) + r0).astype(F32)
            onehot = (rank_col == slot).astype(BF16)
            back = jnp.dot(onehot, ye_ref[pl.ds(r0, n), :].astype(BF16), preferred_element_type=F32)
            acc_ref[...] += ce * back

        _for_row_tiles(ntiles, scatter)

    @pl.when((e == ne - 1) & (f == nf - 1))
    def _():
        y = x_ref[0] + mod_ref[0, 5:6, :] * acc_ref[...]
        if final_norm:
            y = _rms(y, fg_ref[...])
        o_ref[0] = y


def _moe(x, h, cmb, mods, mod_row, w_gu, w_down, final_gain, tm, tf):
    b_, t_, _ = x.shape
    ne, _, f2 = w_gu.shape
    fe = f2 // 2
    assert fe % tf == 0 and tm % (MOE_GROUP * MOE_TILE) == 0 and ne <= SUBLANES
    nf = fe // tf
    nj = t_ // tm
    counts = jnp.sum((cmb[..., :ne] > 0.0).reshape(b_, nj, tm, ne), axis=2, dtype=jnp.int32).reshape(-1)
    tok = lambda wd: pl.BlockSpec((1, tm, wd), lambda b, j, e, f, cnt: (b, j, 0))
    final_norm = final_gain is not None
    fg = (final_gain if final_norm else jnp.ones((D_MODEL,), F32)).reshape(1, D_MODEL)
    grid_spec = pltpu.PrefetchScalarGridSpec(
        num_scalar_prefetch=1,
        grid=(b_, nj, ne, nf),
        in_specs=[tok(D_MODEL), tok(D_MODEL), tok(LANES),
                  pl.BlockSpec((1, SUBLANES, D_MODEL), lambda b, j, e, f, cnt: (mod_row(b), 0, 0)),
                  pl.BlockSpec((1, D_MODEL, tf), lambda b, j, e, f, cnt: (e, 0, f)),
                  pl.BlockSpec((1, D_MODEL, tf), lambda b, j, e, f, cnt: (e, 0, nf + f)),
                  pl.BlockSpec((1, tf, D_MODEL), lambda b, j, e, f, cnt: (e, f, 0)),
                  pl.BlockSpec((1, D_MODEL), lambda b, j, e, f, cnt: (0, 0))],
        out_specs=tok(D_MODEL),
        scratch_shapes=[pltpu.VMEM((tm, D_MODEL), F32),
                        pltpu.VMEM((tm, D_MODEL), BF16),
                        pltpu.VMEM((tm, D_MODEL), F32),
                        pltpu.VMEM((tm, LANES), F32),
                        pltpu.VMEM((LANES, tm), F32)])
    return pl.pallas_call(
        functools.partial(_moe_kernel, final_norm=final_norm),
        grid_spec=grid_spec,
        out_shape=jax.ShapeDtypeStruct((b_, t_, D_MODEL), F32),
        compiler_params=_cparams(("parallel", "parallel", "arbitrary", "arbitrary")),
        name="moe",
    )(counts, x, h, cmb, mods, w_gu, w_gu, w_down, fg)


def _pack_w_in(w):
    offs = np.cumsum((0, QKV_W, DN_WIDTH, 2 * DN_HEADS, 2 * DN_HEADS, GLA_K_WIDTH, GLA_K_WIDTH, GLA_V_WIDTH,
                      GLA_V_WIDTH, 2 * GLA_GATE_RANK))
    seg = lambda i: w[:, offs[i]:offs[i + 1]]
    small = jnp.concatenate([seg(2), seg(3), seg(8)], axis=1)
    small = jnp.pad(small, ((0, 0), (0, SMALL_W - small.shape[1])))
    return jnp.concatenate([seg(0), seg(1), seg(4), seg(5), seg(6), seg(7), small], axis=1).astype(BF16)


def _pack_gate(w_gate2, b_gate):
    ws = []
    for d in range(2):
        lo = GLR_LANE + d * GLA_GATE_RANK
        ws.append(jnp.zeros((SMALL_W, GLA_K_WIDTH), F32).at[lo:lo + GLA_GATE_RANK].set(w_gate2[d]))
    return jnp.stack(ws), b_gate.reshape(2, 1, GLA_K_WIDTH)


def _pack_dn_params(a_log, dt_bias):
    neg_a = jnp.zeros((SMALL_W,), F32).at[GDEC_LANE:GLR_LANE].set(-jnp.exp(a_log.reshape(-1)))
    dtb = jnp.zeros((SMALL_W,), F32).at[GDEC_LANE:GLR_LANE].set(dt_bias.reshape(-1))
    return jnp.zeros((SUBLANES, SMALL_W), F32).at[0].set(neg_a).at[1].set(dtb)


def _pad_rows(a, rows):
    return jnp.pad(a, ((0, rows - a.shape[0]),) + ((0, 0),) * (a.ndim - 1))


def _mixers(x_seq, mods, mod_row, gain_mix, w_in_p, cw, ap, wgs, bgs, states, ncol, tm, tb):
    b_, t_, _ = x_seq.shape
    q, k, v, p, p_col, dnz, gqk, gv, gz = _proj(x_seq, mods, mod_row, gain_mix, w_in_p, cw, ap, tm, ncol)
    rows = t_ // ncol
    o_dn_f, o_dn_b, s_dn = _dn_scan(q, k, v, p, states[0], tb)
    o_gl_f, o_gl_b, s_gl = _gla_scan(gqk, gv, p_col, wgs, bgs, states[1], ncol, min(rows, 2 * CHUNK))
    outs = [(o_dn_f, o_gl_f), (o_dn_b, o_gl_b)]
    return outs, dnz, gz, (s_dn, s_gl)


def kernel(x, c, ctx, c_ctx, w_ada, b_ada, norm_mix, norm_ffn, w_in, conv_qkv, dn_a_log, dn_dt_bias, dn_norm,
           gla_w_gate2, gla_b_gate, gla_norm, w_out, ffn_w_gu, ffn_w_down, moe_router, moe_w_gu, moe_w_down,
           final_norm):
    b_, seq, _ = x.shape
    depth = w_ada.shape[0]
    ctx_len = ctx.shape[1]
    mod_rows = -(-(b_ + 1) // SUBLANES) * SUBLANES
    c_rows = _pad_rows(jnp.concatenate([c, c_ctx[None, :]], axis=0), mod_rows)
    lat_row = lambda b: b
    ctx_row = lambda b: b_
    tm_lat, tm_ctx = 512, ctx_len
    tb = 4 * CHUNK

    for i in range(depth):
        last = i == depth - 1
        mods = _ada(c_rows, w_ada[i], b_ada[i]).reshape(mod_rows, N_MOD, D_MODEL)
        mods = jnp.pad(mods, ((0, 0), (0, SUBLANES - N_MOD), (0, 0)))
        w_in_p = _pack_w_in(w_in[i])
        cw = _pad_rows(conv_qkv[i], SUBLANES)
        ap = _pack_dn_params(dn_a_log[i], dn_dt_bias[i])
        wgs, bgs = _pack_gate(gla_w_gate2[i], gla_b_gate[i])
        s0 = (jnp.zeros((b_, 2, DN_HEADS, DN_HEAD_DIM, DN_HEAD_DIM), F32),
              jnp.zeros((b_, 2, GLA_HEADS, GLA_DV, GLA_DK), F32))
        outs_c, dnz_c, gz_c, s_ctx = _mixers(ctx, mods, ctx_row, norm_mix[i], w_in_p, cw, ap, wgs, bgs, s0, 1,
                                             tm_ctx, tb)
        outs_l, dnz_l, gz_l, _ = _mixers(x, mods, lat_row, norm_mix[i], w_in_p, cw, ap, wgs, bgs, s_ctx, GRID_W,
                                         tm_lat, tb)
        w_out_b = w_out[i].astype(BF16)
        is_moe = i % 2 == 1
        if is_moe:
            router = jnp.pad(moe_router[i // 2], ((0, 0), (0, LANES - N_EXPERTS)))
            w_gu = moe_w_gu[i // 2].astype(BF16)
            w_dn = moe_w_down[i // 2].astype(BF16)
        else:
            router = None
            w_gu = ffn_w_gu[i // 2].astype(BF16)
            w_dn = ffn_w_down[i // 2].astype(BF16)

        def channel(tokens, outs, dnz, gz, row, tm, ncol, final_gain):
            res = _mixout(tokens, outs[0][0], outs[1][0], dnz, outs[0][1], outs[1][1], gz, mods, row, dn_norm[i],
                          gla_norm[i], w_out_b, norm_ffn[i], router, tm, ncol)
            if is_moe:
                x_mid, h, cmb = res
                tm_moe = min(1024, tokens.shape[1])
                return _moe(x_mid, h, cmb, mods, row, w_gu, w_dn, final_gain, tm_moe, D_EXPERT_TILE)
            x_mid, h = res
            y = _ffn(x_mid, h, mods, row, w_gu, w_dn, tm)
            if final_gain is not None:
                raise NotImplementedError("final norm is fused into the expert mixer only")
            return y

        x = channel(x, outs_l, dnz_l, gz_l, lat_row, tm_lat, GRID_W, final_norm if last else None)
        if not last:
            ctx = channel(ctx, outs_c, dnz_c, gz_c, ctx_row, tm_ctx, 1, None)
    return x
```
